```python
import math
import jax
import jax.numpy as jnp
from jax import lax
import numpy as np

D_MODEL = 1024
BATCH = 8
SEQ = 4096
DEPTH = 4

CTX_LEN = 256
GRID_W = 64
N_MIXERS = 4
N_MOD = 6
NORM_EPS = 1e-6

S5_GROUP = 16
S5_GROUPS = D_MODEL // S5_GROUP
S5_STATE = 64
S5_CHUNK = 128

LRU_WIDTH = D_MODEL
LRU_BLOCKS = 4
LRU_BW = LRU_WIDTH // LRU_BLOCKS
LRU_CONV = 4
LRU_C = 8.0

RET_HEADS = 4
RET_DK = 256
RET_DV = 512
RET_CHUNK = 128
RET_HK = RET_HEADS * RET_DK
RET_HV = RET_HEADS * RET_DV
RET_IN = 2 * (RET_HK + RET_HV)
ROPE_BASE = 10000.0

GDN_K_HEADS = 8
GDN_V_HEADS = 16
GDN_DK = 128
GDN_DV = 128
GDN_CONV = 4
GDN_CHUNK = 64
GDN_QK = GDN_K_HEADS * GDN_DK
GDN_VV = GDN_V_HEADS * GDN_DV
GDN_NG = 2 * 2 * GDN_V_HEADS
GDN_STATE_COLS = GDN_QK + GDN_VV + GDN_NG
GDN_IN = GDN_STATE_COLS + GDN_QK + GDN_VV

D_FF = 2816
FFN_CONV = 3

kernel_name = 'hybrid_s5_rglru_retention_gdn_diffusion_trunk'


def _flip(t, d):
    return t if (t is None or d == 0) else jnp.flip(t, axis=1)


def rmsnorm(x, g):
    x32 = x.astype(jnp.float32)
    y = x32 * lax.rsqrt(jnp.mean(x32 * x32, axis=-1, keepdims=True) + NORM_EPS)
    return (y * g.astype(jnp.float32)).astype(x.dtype)


def centred_dwconv(x, w):
    k = w.shape[0]
    return lax.conv_general_dilated(x, w[:, None, :].astype(x.dtype), window_strides=(1,),
                                    padding=[(k // 2, k - 1 - k // 2)],
                                    dimension_numbers=('NWC', 'WIO', 'NWC'),
                                    feature_group_count=x.shape[-1])


def _real_combine(e1, e2):
    a1, b1 = e1
    a2, b2 = e2
    return a1 * a2, a2 * b1 + b2


def real_linear_scan(a, b, h0):
    a_cum, b_cum = lax.associative_scan(_real_combine, (a, b), axis=1)
    return b_cum + a_cum * h0[:, None]


def _complex_combine(e1, e2):
    a1r, a1i, b1r, b1i = e1
    a2r, a2i, b2r, b2i = e2
    return (a1r * a2r - a1i * a2i, a1r * a2i + a1i * a2r,
            a2r * b1r - a2i * b1i + b2r, a2r * b1i + a2i * b1r + b2i)


def s5_discretise(lam_re, lam_im, log_step, b_re, b_im):
    lr = jnp.minimum(lam_re.astype(jnp.float32), -1e-4)
    li = lam_im.astype(jnp.float32)
    step = jnp.exp(log_step.astype(jnp.float32))[:, None]
    mag = jnp.exp(lr * step)
    ar, ai = mag * jnp.cos(li * step), mag * jnp.sin(li * step)
    den = lr * lr + li * li
    kr = ((ar - 1.0) * lr + ai * li) / den
    ki = (ai * lr - (ar - 1.0) * li) / den
    br32, bi32 = b_re.astype(jnp.float32), b_im.astype(jnp.float32)
    br = kr[..., None] * br32 - ki[..., None] * bi32
    bi = kr[..., None] * bi32 + ki[..., None] * br32
    return (ar, ai), (br, bi)


def s5_scan(u, lam_bar, b_bar, c_mat, h0, emit):
    ar, ai = lam_bar
    br, bi = b_bar
    cr, ci = c_mat
    bn, length = u.shape[:2]
    t_blk = min(S5_CHUNK, length)
    n_blk = length // t_blk
    a_shape = (bn, t_blk, S5_GROUPS, S5_STATE)
    a_r = jnp.broadcast_to(ar, a_shape)
    a_i = jnp.broadcast_to(ai, a_shape)
    u_blocks = u.reshape(bn, n_blk, t_blk, S5_GROUPS, S5_GROUP).swapaxes(0, 1)

    def step(carry, u_blk):
        hr, hi = carry
        bur = jnp.einsum('btgc,gpc->btgp', u_blk, br)
        bui = jnp.einsum('btgc,gpc->btgp', u_blk, bi)
        cum_r, cum_i, sr, si = lax.associative_scan(_complex_combine, (a_r, a_i, bur, bui), axis=1)
        st_r = sr + cum_r * hr[:, None] - cum_i * hi[:, None]
        st_i = si + cum_r * hi[:, None] + cum_i * hr[:, None]
        y = None
        if emit:
            y = jnp.einsum('btgp,gcp->btgc', st_r, cr) - jnp.einsum('btgp,gcp->btgc', st_i, ci)
        return (st_r[:, -1], st_i[:, -1]), y

    h_last, ys = lax.scan(step, h0, u_blocks)
    y = ys.swapaxes(0, 1).reshape(bn, length, S5_GROUPS, S5_GROUP) if emit else None
    return y, h_last


def s5_mixer(h_lat, h_ctx, lam_re, lam_im, log_step, b_re, b_im, c_re, c_im, d_skip, w_glu, b_glu, ctx_out):
    dt = h_lat.dtype
    bn = h_lat.shape[0]

    def groups(h):
        return h.astype(jnp.float32).reshape(h.shape[0], h.shape[1], S5_GROUPS, S5_GROUP)

    u_lat, u_ctx = groups(h_lat), groups(h_ctx)
    zero = jnp.zeros((bn, S5_GROUPS, S5_STATE), jnp.float32)
    ys_lat, ys_ctx = [], []
    for d in range(2):
        lam_bar, b_bar = s5_discretise(lam_re[d], lam_im[d], log_step[d], b_re[d], b_im[d])
        c_mat = (c_re[d].astype(jnp.float32), c_im[d].astype(jnp.float32))
        y_c, h_c = s5_scan(_flip(u_ctx, d), lam_bar, b_bar, c_mat, (zero, zero), ctx_out)
        y_l, _ = s5_scan(_flip(u_lat, d), lam_bar, b_bar, c_mat, h_c, True)
        ys_lat.append(_flip(y_l, d))
        ys_ctx.append(_flip(y_c, d))
    d_g = d_skip.astype(jnp.float32).reshape(S5_GROUPS, S5_GROUP)

    def glu(y_f, y_b, u):
        y = (y_f + y_b + d_g * u).reshape(u.shape[0], u.shape[1], D_MODEL)
        z = jax.nn.gelu(y).astype(dt)
        val, gate = jnp.split(z @ w_glu + b_glu, 2, axis=-1)
        return val * jax.nn.sigmoid(gate)

    out_ctx = glu(ys_ctx[0], ys_ctx[1], u_ctx) if ctx_out else None
    return glu(ys_lat[0], ys_lat[1], u_lat), out_ctx


def blockdiag(x, w):
    bn, length, _ = x.shape
    y = jnp.einsum('blnc,ncd->blnd', x.reshape(bn, length, LRU_BLOCKS, LRU_BW), w)
    return y.reshape(bn, length, LRU_WIDTH)


def rglru_mixer(h_lat, h_ctx, w_in, conv_w, conv_b, w_a, b_a, w_x, b_x, lam, w_out, ctx_out):
    dt = h_lat.dtype
    wd = LRU_WIDTH

    def prep(h, need_out):
        p = h @ w_in[:, :(2 * wd if need_out else wd)]
        xc = centred_dwconv(p[..., :wd], conv_w) + conv_b
        y = jax.nn.gelu(p[..., wd:]) if need_out else None
        return xc, y

    xc_l, y_l = prep(h_lat, True)
    xc_c, y_c = prep(h_ctx, ctx_out)

    def gates(xc, d):
        r = jax.nn.sigmoid(blockdiag(xc, w_a[d]) + b_a[d]).astype(jnp.float32)
        i = jax.nn.sigmoid(blockdiag(xc, w_x[d]) + b_x[d]).astype(jnp.float32)
        log_a = -LRU_C * r * jax.nn.softplus(-lam[d].astype(jnp.float32))
        return jnp.exp(log_a), jnp.sqrt(-jnp.expm1(2.0 * log_a)) * (i * xc.astype(jnp.float32))

    h0 = jnp.zeros((h_lat.shape[0], wd), jnp.float32)
    hs_lat, hs_ctx = [], []
    for d in range(2):
        a_c, b_c = gates(xc_c, d)
        hc = real_linear_scan(_flip(a_c, d), _flip(b_c, d), h0)
        a_l, b_l = gates(xc_l, d)
        hl = real_linear_scan(_flip(a_l, d), _flip(b_l, d), hc[:, -1])
        hs_lat.append(_flip(hl, d))
        hs_ctx.append(_flip(hc, d))
    out_lat = (y_l * (hs_lat[0] + hs_lat[1]).astype(dt)) @ w_out
    out_ctx = (y_c * (hs_ctx[0] + hs_ctx[1]).astype(dt)) @ w_out if ctx_out else None
    return out_lat, out_ctx


def rope_rotate(x, pos):
    half = x.shape[-1] // 2
    inv_freq = ROPE_BASE ** (-jnp.arange(half, dtype=jnp.float32) / half)
    ang = pos[:, None] * inv_freq[None, :]
    cos = jnp.cos(ang)[None, :, None, :]
    sin = jnp.sin(ang)[None, :, None, :]
    x1 = x[..., :half].astype(jnp.float32)
    x2 = x[..., half:].astype(jnp.float32)
    return jnp.concatenate([x1 * cos - x2 * sin, x1 * sin + x2 * cos], axis=-1).astype(x.dtype)


def grid_rope(x):
    length = x.shape[1]
    rows = length // GRID_W
    t = jnp.arange(length, dtype=jnp.int32)
    row = (t // GRID_W).astype(jnp.float32) - (rows - 1) / 2.0
    col = (t % GRID_W).astype(jnp.float32) - (GRID_W - 1) / 2.0
    half = x.shape[-1] // 2
    return jnp.concatenate([rope_rotate(x[..., :half], row), rope_rotate(x[..., half:], col)], axis=-1)


def retention_scan(q, k, v, r0, strict):
    emit = q is not None
    bn, length = k.shape[:2]
    c = min(RET_CHUNK, length)
    n = length // c
    log_g = jnp.log1p(-jnp.power(2.0, -5.0 - jnp.arange(RET_HEADS, dtype=jnp.float32)))
    idx = jnp.arange(c, dtype=jnp.float32)
    diff = idx[:, None] - idx[None, :]
    mask = (diff > 0) if strict else (diff >= 0)
    dmask = jnp.where(mask[None], jnp.exp(jnp.where(mask, diff, 0.0)[None] * log_g[:, None, None]), 0.0)
    xi = jnp.exp((idx + 1.0)[:, None] * log_g[None, :])
    zeta = jnp.exp((c - 1.0 - idx)[:, None] * log_g[None, :])
    g_blk = jnp.exp(c * log_g)

    def blocks(t):
        return t.reshape(bn, n, c, *t.shape[2:]).swapaxes(0, 1)

    xs = (blocks(k), blocks(v)) + ((blocks(q),) if emit else ())

    def step(r, blk):
        kc, vc = blk[0], blk[1]
        r_new = g_blk[None, :, None, None] * r + jnp.einsum('bjhd,bjhe->bhde', kc * zeta[None, :, :, None], vc)
        o = None
        if emit:
            qc = blk[2]
            s = jnp.einsum('bihd,bjhd->bhij', qc, kc) * dmask[None]
            o = (jnp.einsum('bhij,bjhe->bihe', s, vc)
                 + jnp.einsum('bihd,bhde->bihe', qc, r) * xi[None, :, :, None])
        return r_new, o

    r_last, os = lax.scan(step, r0, xs)
    out = os.swapaxes(0, 1).reshape(bn, length, RET_HEADS, RET_DV) if emit else None
    return out, r_last


def retention_mixer(h_lat, h_ctx, w_in, norm_g, w_out, ctx_out):
    dt = h_lat.dtype

    def prep(h, need_out, rotate):
        bn, length, _ = h.shape
        p = h @ w_in[:, :(RET_IN if need_out else RET_HK + RET_HV)]
        k = p[..., :RET_HK].reshape(bn, length, RET_HEADS, RET_DK)
        v = p[..., RET_HK:RET_HK + RET_HV].reshape(bn, length, RET_HEADS, RET_DV).astype(jnp.float32)
        if rotate:
            k = grid_rope(k)
        q = None
        gate = None
        if need_out:
            q = p[..., RET_HK + RET_HV:2 * RET_HK + RET_HV].reshape(bn, length, RET_HEADS, RET_DK)
            if rotate:
                q = grid_rope(q)
            q = q.astype(jnp.float32) * (RET_DK ** -0.5)
            gate = p[..., 2 * RET_HK + RET_HV:]
        return q, k.astype(jnp.float32), v, gate

    q_l, k_l, v_l, gate_l = prep(h_lat, True, True)
    q_c, k_c, v_c, gate_c = prep(h_ctx, ctx_out, False)
    r0 = jnp.zeros((h_lat.shape[0], RET_HEADS, RET_DK, RET_DV), jnp.float32)
    o_lat, o_ctx = [], []
    for d in range(2):
        strict = d == 1
        oc, r_c = retention_scan(_flip(q_c, d), _flip(k_c, d), _flip(v_c, d), r0, strict)
        ol, _ = retention_scan(_flip(q_l, d), _flip(k_l, d), _flip(v_l, d), r_c, strict)
        o_lat.append(_flip(ol, d))
        o_ctx.append(_flip(oc, d))

    def head_out(o_f, o_b, gate):
        o = o_f + o_b
        o = o * lax.rsqrt(jnp.mean(o * o, axis=-1, keepdims=True) + NORM_EPS)
        o = (o.reshape(o.shape[0], o.shape[1], RET_HV) * norm_g.astype(jnp.float32)).astype(dt)
        return (jax.nn.silu(gate) * o) @ w_out

    out_ctx = head_out(o_ctx[0], o_ctx[1], gate_c) if ctx_out else None
    return head_out(o_lat[0], o_lat[1], gate_l), out_ctx


def l2norm(x):
    return x * lax.rsqrt(jnp.sum(x * x, axis=-1, keepdims=True) + 1e-6)


def gated_delta_chunked(q, k, v, g, beta, s0):
    emit = q is not None
    bn, length, nh = g.shape
    c = min(GDN_CHUNK, length)
    n = length // c

    def blocks(t):
        return jnp.moveaxis(t.reshape(bn, n, c, nh, *t.shape[3:]), 3, 1)

    k, v, beta = blocks(k), blocks(v), blocks(beta)
    g = jnp.cumsum(blocks(g), axis=-1)
    k_beta = k * beta[..., None]
    idx = jnp.arange(c)
    incl = idx[:, None] >= idx[None, :]
    strict = idx[:, None] > idx[None, :]
    gdiff = g[..., :, None] - g[..., None, :]
    decay = jnp.where(incl, jnp.exp(jnp.where(incl, gdiff, 0.0)), 0.0)
    m = jnp.where(strict, jnp.einsum('bhnid,bhnjd->bhnij', k_beta, k) * decay, 0.0)
    a_mat = m + jnp.eye(c, dtype=jnp.float32)
    u = lax.linalg.triangular_solve(a_mat, v * beta[..., None], left_side=True, lower=True, unit_diagonal=True)
    w = lax.linalg.triangular_solve(a_mat, k_beta * jnp.exp(g)[..., None], left_side=True, lower=True,
                                    unit_diagonal=True)
    xs = (k, u, w, g)
    if emit:
        xs = xs + (blocks(q) * (GDN_DK ** -0.5), decay)
    xs = tuple(jnp.moveaxis(t, 2, 0) for t in xs)

    def step(s, blk):
        kc, uc, wc, gc = blk[0], blk[1], blk[2], blk[3]
        v_new = uc - jnp.einsum('bhcd,bhde->bhce', wc, s)
        g_last = gc[..., -1]
        s_new = (s * jnp.exp(g_last)[..., None, None]
                 + jnp.einsum('bhcd,bhce->bhde', kc * jnp.exp(g_last[..., None] - gc)[..., None], v_new))
        o = None
        if emit:
            qc, dc = blk[4], blk[5]
            attn = jnp.einsum('bhid,bhjd->bhij', qc, kc) * dc
            o = (jnp.einsum('bhid,bhde->bhie', qc * jnp.exp(gc)[..., None], s)
                 + jnp.einsum('bhij,bhje->bhie', attn, v_new))
        return s_new, o

    s_last, os = lax.scan(step, s0, xs)
    out = os.transpose(1, 0, 3, 2, 4).reshape(bn, length, nh, GDN_DV) if emit else None
    return out, s_last


def gdn_mixer(h_lat, h_ctx, w_in, conv_w, a_log, dt_bias, norm_g, w_out, ctx_out):
    dt = h_lat.dtype
    rep = GDN_V_HEADS // GDN_K_HEADS

    def prep(h, need_out):
        bn, length, _ = h.shape
        p = h @ w_in[:, :(GDN_IN if need_out else GDN_STATE_COLS)]
        kv = jax.nn.silu(centred_dwconv(p[..., :GDN_QK + GDN_VV], conv_w[:, :GDN_QK + GDN_VV]))
        k = l2norm(kv[..., :GDN_QK].reshape(bn, length, GDN_K_HEADS, GDN_DK).astype(jnp.float32))
        k = jnp.repeat(k, rep, axis=2)
        v = kv[..., GDN_QK:].reshape(bn, length, GDN_V_HEADS, GDN_DV).astype(jnp.float32)
        ba = p[..., GDN_QK + GDN_VV:GDN_STATE_COLS].astype(jnp.float32).reshape(bn, length, 2, 2, GDN_V_HEADS)
        q = None
        z = None
        if need_out:
            q = jax.nn.silu(centred_dwconv(p[..., GDN_STATE_COLS:GDN_STATE_COLS + GDN_QK],
                                           conv_w[:, GDN_QK + GDN_VV:]))
            q = jnp.repeat(l2norm(q.reshape(bn, length, GDN_K_HEADS, GDN_DK).astype(jnp.float32)), rep, axis=2)
            z = p[..., GDN_STATE_COLS + GDN_QK:]
        return q, k, v, ba, z

    q_l, k_l, v_l, ba_l, z_l = prep(h_lat, True)
    q_c, k_c, v_c, ba_c, z_c = prep(h_ctx, ctx_out)
    s0 = jnp.zeros((h_lat.shape[0], GDN_V_HEADS, GDN_DK, GDN_DV), jnp.float32)
    o_lat, o_ctx = [], []
    for d in range(2):
        a_d = jnp.exp(a_log[d].astype(jnp.float32))
        dtb = dt_bias[d].astype(jnp.float32)

        def gates(ba):
            return (-a_d * jax.nn.softplus(ba[:, :, d, 1] + dtb), jax.nn.sigmoid(ba[:, :, d, 0]))

        g_c, b_c = gates(ba_c)
        g_l, b_l = gates(ba_l)
        oc, s_c = gated_delta_chunked(_flip(q_c, d), _flip(k_c, d), _flip(v_c, d), _flip(g_c, d), _flip(b_c, d), s0)
        ol, _ = gated_delta_chunked(_flip(q_l, d), _flip(k_l, d), _flip(v_l, d), _flip(g_l, d), _flip(b_l, d), s_c)
        o_lat.append(_flip(ol, d))
        o_ctx.append(_flip(oc, d))

    def head_out(o_f, o_b, z):
        o = o_f + o_b
        o = o * lax.rsqrt(jnp.mean(o * o, axis=-1, keepdims=True) + NORM_EPS) * norm_g.astype(jnp.float32)
        o = o.astype(dt) * jax.nn.silu(z).reshape(o.shape)
        return o.reshape(o.shape[0], o.shape[1], GDN_VV) @ w_out

    out_ctx = head_out(o_ctx[0], o_ctx[1], z_c) if ctx_out else None
    return head_out(o_lat[0], o_lat[1], z_l), out_ctx


def conv_ffn(h, w_in, conv_w, conv_b, w_out):
    gate, up = jnp.split(h @ w_in, 2, axis=-1)
    return (jax.nn.gelu(centred_dwconv(gate, conv_w) + conv_b) * up) @ w_out


def setup_inputs(seed: int = 0) -> dict:
    key = jax.random.key(seed)
    keys = iter(jax.random.split(key, 64))
    f32 = jnp.float32

    def nrm(shape, scale):
        return scale * jax.random.normal(next(keys), shape, f32)

    def unif(shape, lo, hi):
        return jax.random.uniform(next(keys), shape, f32, lo, hi)

    n_a, n_b, n_c, n_d = [len(range(m, DEPTH, N_MIXERS)) for m in range(N_MIXERS)]
    d = D_MODEL
    g, p, gc = S5_GROUPS, S5_STATE, S5_GROUP
    wd = LRU_WIDTH
    s5_n = jnp.arange(p, dtype=f32)
    lru_a0 = unif((n_b, 2, wd), 0.9, 0.999) ** (1.0 / LRU_C)
    gdn_dt = jnp.exp(unif((n_d, 2, GDN_V_HEADS), math.log(1e-3), math.log(1e-1)))
    return {
        'x': nrm((BATCH, SEQ, d), 1.0),
        'c': nrm((BATCH, d), 1.0),
        'ctx': nrm((BATCH, CTX_LEN, d), 1.0),
        'c_ctx': nrm((d,), 1.0),
        'norm1_g': 1.0 + nrm((DEPTH, d), 0.02),
        'norm2_g': 1.0 + nrm((DEPTH, d), 0.02),
        'mod_w': nrm((DEPTH, d, N_MOD * d), 0.5 * d ** -0.5),
        'mod_b': nrm((DEPTH, N_MOD * d), 0.02),
        'ffn_w_in': nrm((DEPTH, d, 2 * D_FF), d ** -0.5),
        'ffn_conv_w': nrm((DEPTH, FFN_CONV, D_FF), FFN_CONV ** -0.5),
        'ffn_conv_b': nrm((DEPTH, D_FF), 0.02),
        'ffn_w_out': nrm((DEPTH, D_FF, d), D_FF ** -0.5),
        's5_lam_re': -0.5 + nrm((n_a, 2, g, p), 0.01),
        's5_lam_im': math.pi * s5_n + nrm((n_a, 2, g, p), 0.01),
        's5_log_step': unif((n_a, 2, g), math.log(1e-3), math.log(1e-1)),
        's5_b_re': nrm((n_a, 2, g, p, gc), (2 * gc) ** -0.5),
        's5_b_im': nrm((n_a, 2, g, p, gc), (2 * gc) ** -0.5),
        's5_c_re': nrm((n_a, 2, g, gc, p), p ** -0.5),
        's5_c_im': nrm((n_a, 2, g, gc, p), p ** -0.5),
        's5_d': nrm((n_a, d), 1.0),
        's5_w_glu': nrm((n_a, d, 2 * d), d ** -0.5),
        's5_b_glu': nrm((n_a, 2 * d), 0.02),
        'lru_w_in': nrm((n_b, d, 2 * wd), d ** -0.5),
        'lru_conv_w': nrm((n_b, LRU_CONV, wd), LRU_CONV ** -0.5),
        'lru_conv_b': nrm((n_b, wd), 0.02),
        'lru_w_a': nrm((n_b, 2, LRU_BLOCKS, LRU_BW, LRU_BW), LRU_BW ** -0.5),
        'lru_b_a': nrm((n_b, 2, wd), 0.02),
        'lru_w_x': nrm((n_b, 2, LRU_BLOCKS, LRU_BW, LRU_BW), LRU_BW ** -0.5),
        'lru_b_x': nrm((n_b, 2, wd), 0.02),
        'lru_lam': jnp.log(lru_a0) - jnp.log1p(-lru_a0),
        'lru_w_out': nrm((n_b, wd, d), wd ** -0.5),
        'ret_w_in': nrm((n_c, d, RET_IN), d ** -0.5),
        'ret_norm_g': 1.0 + nrm((n_c, RET_HV), 0.02),
        'ret_w_out': nrm((n_c, RET_HV, d), RET_HV ** -0.5),
        'gdn_w_in': nrm((n_d, d, GDN_IN), d ** -0.5),
        'gdn_conv_w': nrm((n_d, GDN_CONV, 2 * GDN_QK + GDN_VV), GDN_CONV ** -0.5),
        'gdn_a_log': jnp.log(unif((n_d, 2, GDN_V_HEADS), 1.0, 16.0)),
        'gdn_dt_bias': gdn_dt + jnp.log(-jnp.expm1(-gdn_dt)),
        'gdn_norm_g': 1.0 + nrm((n_d, GDN_DV), 0.02),
        'gdn_w_out': nrm((n_d, GDN_VV, d), GDN_VV ** -0.5),
        'final_norm_g': 1.0 + nrm((d,), 0.02),
    }


def reference(x, c, ctx, c_ctx, norm1_g, norm2_g, mod_w, mod_b, ffn_w_in, ffn_conv_w, ffn_conv_b, ffn_w_out,
              s5_lam_re, s5_lam_im, s5_log_step, s5_b_re, s5_b_im, s5_c_re, s5_c_im, s5_d, s5_w_glu, s5_b_glu,
              lru_w_in, lru_conv_w, lru_conv_b, lru_w_a, lru_b_a, lru_w_x, lru_b_x, lru_lam, lru_w_out,
              ret_w_in, ret_norm_g, ret_w_out,
              gdn_w_in, gdn_conv_w, gdn_a_log, gdn_dt_bias, gdn_norm_g, gdn_w_out,
              final_norm_g):
    for i in range(DEPTH):
        kind, j = i % N_MIXERS, i // N_MIXERS
        ctx_out = i < DEPTH - 1
        mod = jax.nn.silu(c) @ mod_w[i] + mod_b[i]
        mod_c = jax.nn.silu(c_ctx) @ mod_w[i] + mod_b[i]
        sh1, sc1, g1, sh2, sc2, g2 = jnp.split(mod[:, None, :], N_MOD, axis=-1)
        csh1, csc1, cg1, csh2, csc2, cg2 = jnp.split(mod_c[None, None, :], N_MOD, axis=-1)
        h_lat = rmsnorm(x, norm1_g[i]) * (1.0 + sc1) + sh1
        h_ctx = rmsnorm(ctx, norm1_g[i]) * (1.0 + csc1) + csh1
        if kind == 0:
            y_lat, y_ctx = s5_mixer(h_lat, h_ctx, s5_lam_re[j], s5_lam_im[j], s5_log_step[j], s5_b_re[j],
                                    s5_b_im[j], s5_c_re[j], s5_c_im[j], s5_d[j], s5_w_glu[j], s5_b_glu[j], ctx_out)
        elif kind == 1:
            y_lat, y_ctx = rglru_mixer(h_lat, h_ctx, lru_w_in[j], lru_conv_w[j], lru_conv_b[j], lru_w_a[j],
                                       lru_b_a[j], lru_w_x[j], lru_b_x[j], lru_lam[j], lru_w_out[j], ctx_out)
        elif kind == 2:
            y_lat, y_ctx = retention_mixer(h_lat, h_ctx, ret_w_in[j], ret_norm_g[j], ret_w_out[j], ctx_out)
        else:
            y_lat, y_ctx = gdn_mixer(h_lat, h_ctx, gdn_w_in[j], gdn_conv_w[j], gdn_a_log[j], gdn_dt_bias[j],
                                     gdn_norm_g[j], gdn_w_out[j], ctx_out)
        x = x + g1 * y_lat
        x = x + g2 * conv_ffn(rmsnorm(x, norm2_g[i]) * (1.0 + sc2) + sh2,
                              ffn_w_in[i], ffn_conv_w[i], ffn_conv_b[i], ffn_w_out[i])
        if ctx_out:
            ctx = ctx + cg1 * y_ctx
            ctx = ctx + cg2 * conv_ffn(rmsnorm(ctx, norm2_g[i]) * (1.0 + csc2) + csh2,
                                       ffn_w_in[i], ffn_conv_w[i], ffn_conv_b[i], ffn_w_out[i])
    return rmsnorm(x, final_norm_g)
```

```python
import functools
import math

import numpy as np
import jax
import jax.numpy as jnp
from jax import lax
from jax.experimental import pallas as pl
from jax.experimental.pallas import tpu as pltpu

F32 = jnp.float32
BF16 = jnp.bfloat16

NORM_EPS = 1e-6
SUBLANES = 8
LANES = 128
VMEM_LIMIT_BYTES = 56 * 1024 * 1024
ROW_TILE_CAP = 1024

S5_GROUP = 16
S5_STATE = 64
S5_GB = 16
LRU_BLOCKS = 4
LRU_C = 8.0
RET_HEADS = 4
RET_DK = 256
RET_DV = 512
RET_CHUNK = 128
ROPE_BASE = 10000.0
GRID_W = 64
GDN_K_HEADS = 8
GDN_V_HEADS = 16
GDN_DK = 128
GDN_DV = 128
GDN_CHUNK = 64
N_MOD = 6


def _cparams(*sem):
    return pltpu.CompilerParams(dimension_semantics=sem, vmem_limit_bytes=VMEM_LIMIT_BYTES)


def _row_tile(rows, cap=None):
    cap = ROW_TILE_CAP if cap is None else cap
    tm = min(cap, rows)
    assert rows % tm == 0 and tm % SUBLANES == 0, (rows, tm)
    return tm


def _dot(a, b):
    return jnp.dot(a, b, preferred_element_type=F32)


def _dot_tn(a, b):
    return lax.dot_general(a, b, (((0,), (0,)), ((), ())), preferred_element_type=F32)


def _dot_nt(a, b):
    return lax.dot_general(a, b, (((1,), (1,)), ((), ())), preferred_element_type=F32)


def _gelu(x):
    return jax.nn.gelu(x, approximate=True)


def _sigmoid(x):
    return 1.0 / (1.0 + jnp.exp(-x))


def _silu(x):
    return x * _sigmoid(x)


def _tile8(v, rows):
    n = v.shape[-1]
    return jnp.broadcast_to(v[None], (rows // SUBLANES, SUBLANES, n)).reshape(rows, n)


def _norm_mod(x, g, sc, sh):
    ms = jnp.mean(x * x, axis=-1, keepdims=True)
    y = x * lax.rsqrt(ms + NORM_EPS) * g
    rows = x.shape[0]
    return y * (1.0 + _tile8(sc, rows)) + _tile8(sh, rows)


def _shift_rows(p, off, rows):
    if off % SUBLANES == 0:
        return p[off:off + rows]
    n = p.shape[0]
    base = (off // SUBLANES) * SUBLANES
    rolled = pltpu.roll(p, (n - (off - base)) % n, axis=0)
    return rolled[base:base + rows]


def _conv_rows(p_ext, cw, rows, rs, hb):
    k_taps = cw.shape[0]
    acc = None
    for k in range(k_taps):
        term = cw[k:k + 1, :] * _shift_rows(p_ext, hb + (k - k_taps // 2) * rs, rows)
        acc = term if acc is None else acc + term
    return acc


def _fill_h(h_ref, x_ref, xb_ref, xa_ref, g_ref, sc_ref, sh_ref, first, last, hb, ha):
    tm = x_ref.shape[0]
    g, sc, sh = g_ref[...], sc_ref[...], sh_ref[...]
    h_ref[hb:hb + tm, :] = _norm_mod(x_ref[...], g, sc, sh).astype(BF16)
    if hb:
        hbv = _norm_mod(xb_ref[...], g, sc, sh)
        h_ref[0:hb, :] = jnp.where(first, 0.0, hbv).astype(BF16)
    if ha:
        hav = _norm_mod(xa_ref[...], g, sc, sh)
        h_ref[hb + tm:hb + tm + ha, :] = jnp.where(last, 0.0, hav).astype(BF16)


def _halo_specs(tm, hb, ha, d, n_rows):
    specs = [pl.BlockSpec((tm, d), lambda i, j: (i, 0))]
    if hb:
        specs.append(pl.BlockSpec((hb, d), lambda i, j: (jnp.maximum(i * (tm // hb) - 1, 0), 0)))
    if ha:
        nblk = n_rows // ha
        specs.append(pl.BlockSpec((ha, d), lambda i, j: (jnp.minimum((i + 1) * (tm // ha), nblk - 1), 0)))
    return specs


def _mod_spec(chunk, d, tiles_per_mod):
    return pl.BlockSpec((None, SUBLANES, d), lambda i, j: (i // tiles_per_mod, 0, chunk))


def _mod_kernel(c_ref, w_ref, b_ref, o_ref):
    a = _silu(c_ref[...]).astype(BF16)
    o_ref[...] = _dot(a, w_ref[...].astype(BF16)) + b_ref[...]


def _modulation(c16, mod_w, mod_b):
    depth, d, n = mod_w.shape
    tn = n // 4
    return pl.pallas_call(
        _mod_kernel,
        grid=(depth, n // tn),
        in_specs=[pl.BlockSpec((16, d), lambda l, j: (0, 0)),
                  pl.BlockSpec((None, d, tn), lambda l, j: (l, 0, j)),
                  pl.BlockSpec((None, 1, tn), lambda l, j: (l, 0, j))],
        out_specs=pl.BlockSpec((None, 16, tn), lambda l, j: (l, 0, j)),
        out_shape=jax.ShapeDtypeStruct((depth, 16, n), F32),
        compiler_params=_cparams("arbitrary", "arbitrary"),
        name="modulation",
    )(c16, mod_w, mod_b.reshape(depth, 1, n))


def _ffn_kernel(x_ref, xb_ref, xa_ref, g_ref, sh_ref, sc_ref, gt_ref, wg_ref, wu_ref, cw_ref, cb_ref, wo_ref,
                fg_ref, o_ref, h_ref, acc_ref, *, rs, tiles_per_seq, final_norm):
    i, j = pl.program_id(0), pl.program_id(1)
    tm = x_ref.shape[0]
    hb = ha = SUBLANES

    @pl.when(j == 0)
    def _():
        first = i % tiles_per_seq == 0
        last = i % tiles_per_seq == tiles_per_seq - 1
        _fill_h(h_ref, x_ref, xb_ref, xa_ref, g_ref, sc_ref, sh_ref, first, last, hb, ha)
        acc_ref[...] = jnp.zeros_like(acc_ref)

    gate = _dot(h_ref[...], wg_ref[...])
    up = _dot(h_ref[hb:hb + tm, :], wu_ref[...])
    gc = _conv_rows(gate, cw_ref[...], tm, rs, hb) + cb_ref[...]
    a = (_gelu(gc) * up).astype(BF16)
    acc_ref[...] += _dot(a, wo_ref[...])

    @pl.when(j == pl.num_programs(1) - 1)
    def _():
        y = x_ref[...] + _tile8(gt_ref[...], tm) * acc_ref[...]
        if final_norm:
            ms = jnp.mean(y * y, axis=-1, keepdims=True)
            y = y * lax.rsqrt(ms + NORM_EPS) * fg_ref[...]
        o_ref[...] = y


def _ffn(x, modt, tiles_per_mod, norm_g, w_in, conv_w, conv_b, w_out, final_g, *, rs, seq_rows, final_norm=False):
    n_rows, d = x.shape
    f = w_out.shape[0]
    tm = _row_tile(seq_rows)
    tf = 2 * LANES
    nj = f // tf
    assert f % tf == 0
    hb = ha = SUBLANES
    kern = functools.partial(_ffn_kernel, rs=rs, tiles_per_seq=seq_rows // tm, final_norm=final_norm)
    in_specs = _halo_specs(tm, hb, ha, d, n_rows) + [
        pl.BlockSpec((1, d), lambda i, j: (0, 0)),
        _mod_spec(3, d, tiles_per_mod), _mod_spec(4, d, tiles_per_mod), _mod_spec(5, d, tiles_per_mod),
        pl.BlockSpec((d, tf), lambda i, j: (0, j)),
        pl.BlockSpec((d, tf), lambda i, j: (0, j + nj)),
        pl.BlockSpec((conv_w.shape[0], tf), lambda i, j: (0, j)),
        pl.BlockSpec((1, tf), lambda i, j: (0, j)),
        pl.BlockSpec((tf, d), lambda i, j: (j, 0)),
        pl.BlockSpec((1, d), lambda i, j: (0, 0)),
    ]
    return pl.pallas_call(
        kern,
        grid=(n_rows // tm, nj),
        in_specs=in_specs,
        out_specs=pl.BlockSpec((tm, d), lambda i, j: (i, 0)),
        out_shape=jax.ShapeDtypeStruct((n_rows, d), F32),
        scratch_shapes=[pltpu.VMEM((tm + hb + ha, d), BF16), pltpu.VMEM((tm, d), F32)],
        compiler_params=_cparams("parallel", "arbitrary"),
        name="conv_ffn",
    )(x, x, x, norm_g.reshape(1, d), modt, modt, modt, w_in, w_in, conv_w, conv_b.reshape(1, f), w_out,
      final_g.reshape(1, d))


def _outproj_kernel(*refs, n_pro, prologue, glu):
    pro_refs = refs[:n_pro]
    if glu:
        xc_ref, gt_ref, wv_ref, wg_ref, bv_ref, bg_ref, o_ref, a_ref = refs[n_pro:]
    else:
        xc_ref, gt_ref, wv_ref, o_ref, a_ref = refs[n_pro:]

    @pl.when(pl.program_id(1) == 0)
    def _():
        a_ref[...] = prologue(*pro_refs).astype(BF16)

    y = _dot(a_ref[...], wv_ref[...])
    if glu:
        y = (y + bv_ref[...]) * jax.nn.sigmoid(_dot(a_ref[...], wg_ref[...]) + bg_ref[...])
    o_ref[...] = xc_ref[...] + _tile8(gt_ref[...], o_ref.shape[0]) * y


def _outproj(prologue, pro_args, pro_specs, k_dim, x, modt, tiles_per_mod, w, bias=None, *, tm, name):
    n_rows, d = x.shape
    tn = 4 * LANES
    nj = d // tn
    glu = bias is not None
    kern = functools.partial(_outproj_kernel, n_pro=len(pro_args), prologue=prologue, glu=glu)
    specs = list(pro_specs) + [pl.BlockSpec((tm, tn), lambda i, j: (i, j)),
                               pl.BlockSpec((None, SUBLANES, tn), lambda i, j: (i // tiles_per_mod, 0, 2 * nj + j)),
                               pl.BlockSpec((k_dim, tn), lambda i, j: (0, j))]
    args = list(pro_args) + [x, modt, w]
    if glu:
        specs += [pl.BlockSpec((k_dim, tn), lambda i, j: (0, j + nj)),
                  pl.BlockSpec((1, tn), lambda i, j: (0, j)),
                  pl.BlockSpec((1, tn), lambda i, j: (0, j + nj))]
        args += [w, bias, bias]
    return pl.pallas_call(
        kern,
        grid=(n_rows // tm, nj),
        in_specs=specs,
        out_specs=pl.BlockSpec((tm, tn), lambda i, j: (i, j)),
        out_shape=jax.ShapeDtypeStruct((n_rows, d), F32),
        scratch_shapes=[pltpu.VMEM((tm, k_dim), BF16)],
        compiler_params=_cparams("parallel", "arbitrary"),
        name=name,
    )(*args)


def _row_spec(tm, n):
    return pl.BlockSpec((tm, n), lambda i, j: (i, 0))


def _const_spec(shape):
    nd = len(shape)
    return pl.BlockSpec(shape, lambda i, j: (0,) * nd)


def _s5_weights(lam_re, lam_im, log_step, b_re, b_im, c_re, c_im):
    g, p = lam_re.shape
    gc = b_re.shape[-1]
    lr = jnp.minimum(lam_re.astype(F32), -1e-4)
    li = lam_im.astype(F32)
    step = jnp.exp(log_step.astype(F32))[:, None]
    mag = jnp.exp(lr * step)
    ar, ai = mag * jnp.cos(li * step), mag * jnp.sin(li * step)
    den = lr * lr + li * li
    kr = ((ar - 1.0) * lr + ai * li) / den
    ki = (ai * lr - (ar - 1.0) * li) / den
    br32, bi32 = b_re.astype(F32), b_im.astype(F32)
    br = kr[..., None] * br32 - ki[..., None] * bi32
    bi = kr[..., None] * bi32 + ki[..., None] * br32
    nb = g // S5_GB
    eye = jnp.eye(S5_GB, dtype=F32)

    def pack_b(b):
        b4 = b.reshape(nb, S5_GB, p, gc)
        return jnp.einsum('blpc,lm->blcmp', b4, eye).reshape(nb, S5_GB * gc, S5_GB * p)

    def pack_c(cm):
        c4 = cm.astype(F32).reshape(nb, S5_GB, gc, p)
        return jnp.einsum('blcp,lm->blpmc', c4, eye).reshape(nb, S5_GB * p, S5_GB * gc)

    bm = jnp.concatenate([pack_b(br), pack_b(bi)], axis=-1).astype(BF16)
    cm = jnp.concatenate([pack_c(c_re), -pack_c(c_im)], axis=1).astype(BF16)
    lam = jnp.stack([ar.reshape(nb, S5_GB * p), ai.reshape(nb, S5_GB * p)], axis=1)
    lam = jnp.broadcast_to(lam[:, :, None, :], (nb, 2, SUBLANES, S5_GB * p))
    return bm, cm, lam


def _s5_scan_kernel(x_ref, g_ref, sh_ref, sc_ref, bm_ref, cm_ref, lam_ref, s0_ref, y_ref, sf_ref, st_ref, bu_ref,
                    *, rev, emit):
    i = pl.program_id(0)
    tm = x_ref.shape[0]
    nt = tm // SUBLANES
    nb, kin, ns2 = bm_ref.shape
    ns = ns2 // 2

    @pl.when(i == 0)
    def _():
        st_ref[...] = s0_ref[...]
        if not emit:
            y_ref[...] = jnp.zeros_like(y_ref)

    h = _norm_mod(x_ref[...], g_ref[...], sc_ref[...], sh_ref[...]).astype(BF16)
    for gb in range(nb):
        bu_ref[...] = _dot(h[:, gb * kin:(gb + 1) * kin], bm_ref[gb])
        ar, ai = lam_ref[gb, 0], lam_ref[gb, 1]

        def step(t, carry):
            sr, si = carry
            tt = nt - 1 - t if rev else t
            r0 = pl.multiple_of(tt * SUBLANES, SUBLANES)
            nr = ar * sr - ai * si + bu_ref[pl.ds(r0, SUBLANES), 0:ns]
            ni = ar * si + ai * sr + bu_ref[pl.ds(r0, SUBLANES), ns:ns2]
            if emit:
                bu_ref[pl.ds(r0, SUBLANES), 0:ns] = nr
                bu_ref[pl.ds(r0, SUBLANES), ns:ns2] = ni
            return nr, ni

        sr, si = lax.fori_loop(0, nt, step, (st_ref[gb, 0], st_ref[gb, 1]), unroll=4)
        st_ref[gb, 0] = sr
        st_ref[gb, 1] = si
        if emit:
            y_ref[:, gb * kin:(gb + 1) * kin] = _dot(bu_ref[...].astype(BF16), cm_ref[gb])

    @pl.when(i == pl.num_programs(0) - 1)
    def _():
        sf_ref[...] = st_ref[...]


def _s5_scan(x, modt, norm_g, bm, cm, lam, s0, *, rev, emit=True):
    n_rows, d = x.shape
    tm = _row_tile(n_rows, cap=ROW_TILE_CAP // 2)
    ntile = n_rows // tm
    tile = (lambda i: (ntile - 1 - i, 0)) if rev else (lambda i: (i, 0))
    mod = lambda k: pl.BlockSpec((None, SUBLANES, d), lambda i: (0, 0, k))
    full = lambda a: pl.BlockSpec(a.shape, lambda i: (0,) * a.ndim)
    y, sf = pl.pallas_call(
        functools.partial(_s5_scan_kernel, rev=rev, emit=emit),
        grid=(ntile,),
        in_specs=[pl.BlockSpec((tm, d), tile), pl.BlockSpec((1, d), lambda i: (0, 0)), mod(0), mod(1),
                  full(bm), full(cm), full(lam), full(s0)],
        out_specs=[pl.BlockSpec((tm, d), tile if emit else (lambda i: (0, 0))), full(s0)],
        out_shape=[jax.ShapeDtypeStruct((n_rows if emit else tm, d), F32), jax.ShapeDtypeStruct(s0.shape, F32)],
        scratch_shapes=[pltpu.VMEM(s0.shape, F32), pltpu.VMEM((tm, bm.shape[-1]), F32)],
        compiler_params=_cparams("arbitrary"),
        name="s5_scan_bwd" if rev else "s5_scan_fwd",
    )(x, norm_g.reshape(1, d), modt, modt, bm, cm, lam, s0)
    return (y if emit else None), sf


def _s5_out_prologue(x_ref, yf_ref, yb_ref, g_ref, sh_ref, sc_ref, dk_ref):
    u = _norm_mod(x_ref[...], g_ref[...], sc_ref[...], sh_ref[...])
    return _gelu(yf_ref[...] + yb_ref[...] + dk_ref[...] * u)


def _s5_mixer(x, x_ctx, modt, modt_ctx, norm_g, prm, *, ctx_out):
    d = x.shape[-1]
    w = [_s5_weights(prm['lam_re'][dr], prm['lam_im'][dr], prm['log_step'][dr], prm['b_re'][dr], prm['b_im'][dr],
                     prm['c_re'][dr], prm['c_im'][dr]) for dr in range(2)]
    zero = jnp.zeros(w[0][2].shape, F32)
    ys, ys_ctx = [], []
    for dr in range(2):
        bm, cm, lam = w[dr]
        yc, sc = _s5_scan(x_ctx, modt_ctx, norm_g, bm, cm, lam, zero, rev=dr == 1, emit=ctx_out)
        yl, _ = _s5_scan(x, modt, norm_g, bm, cm, lam, sc, rev=dr == 1)
        ys.append(yl)
        ys_ctx.append(yc)
    w_glu = prm['w_glu'].astype(BF16)
    b_glu = prm['b_glu'].reshape(1, -1)

    def out(xa, ya, mt):
        tm = _row_tile(xa.shape[0])
        pro_args = [xa, ya[0], ya[1], norm_g.reshape(1, d), mt, mt, prm['d'].reshape(1, d)]
        pro_specs = [_row_spec(tm, d), _row_spec(tm, d), _row_spec(tm, d), _const_spec((1, d)),
                     _mod_spec(0, d, 10 ** 9), _mod_spec(1, d, 10 ** 9), _const_spec((1, d))]
        return _outproj(_s5_out_prologue, pro_args, pro_specs, d, xa, mt, 10 ** 9, w_glu, b_glu, tm=tm, name="s5_out")

    return out(x, ys, modt), (out(x_ctx, ys_ctx, modt_ctx) if ctx_out else None)


def _lru_in_kernel(x_ref, xb_ref, xa_ref, g_ref, sh_ref, sc_ref, wr_ref, wy_ref, cw_ref, cb_ref, xc_ref, yg_ref, h_ref,
                   *, rs, hb, ha):
    i, j = pl.program_id(0), pl.program_id(1)
    tm = x_ref.shape[0]

    @pl.when(j == 0)
    def _():
        _fill_h(h_ref, x_ref, xb_ref, xa_ref, g_ref, sc_ref, sh_ref, i == 0, i == pl.num_programs(0) - 1, hb, ha)

    p = _dot(h_ref[...], wr_ref[...])
    xc_ref[...] = _conv_rows(p, cw_ref[...], tm, rs, hb) + cb_ref[...]
    yg_ref[...] = _gelu(_dot(h_ref[hb:hb + tm, :], wy_ref[...]))


def _lru_in(x, modt, norm_g, w_in, conv_w, conv_b):
    n_rows, d = x.shape
    wd = conv_w.shape[-1]
    rs = SUBLANES
    hb, ha = 2 * rs, rs
    tm = _row_tile(n_rows)
    tn = 4 * LANES
    nj = wd // tn
    specs = _halo_specs(tm, hb, ha, d, n_rows) + [
        _const_spec((1, d)), _mod_spec(0, d, 10 ** 9), _mod_spec(1, d, 10 ** 9),
        pl.BlockSpec((d, tn), lambda i, j: (0, j)), pl.BlockSpec((d, tn), lambda i, j: (0, j + nj)),
        pl.BlockSpec((conv_w.shape[0], tn), lambda i, j: (0, j)), pl.BlockSpec((1, tn), lambda i, j: (0, j))]
    out_spec = pl.BlockSpec((tm, tn), lambda i, j: (i, j))
    return pl.pallas_call(
        functools.partial(_lru_in_kernel, rs=rs, hb=hb, ha=ha),
        grid=(n_rows // tm, nj),
        in_specs=specs,
        out_specs=[out_spec, out_spec],
        out_shape=[jax.ShapeDtypeStruct((n_rows, wd), F32)] * 2,
        scratch_shapes=[pltpu.VMEM((tm + hb + ha, d), BF16)],
        compiler_params=_cparams("parallel", "arbitrary"),
        name="lru_in",
    )(x, x, x, norm_g.reshape(1, d), modt, modt, w_in, w_in, conv_w, conv_b.reshape(1, wd))


def _lru_scan_kernel(xc_ref, wa_ref, wx_ref, ba_ref, bx_ref, nsp_ref, h0_ref, hs_ref, hf_ref, st_ref, a_ref, b_ref,
                     *, rev, emit):
    i = pl.program_id(0)
    tm = xc_ref.shape[0]
    nt = tm // SUBLANES
    nblk, bw, _ = wa_ref.shape

    @pl.when(i == 0)
    def _():
        st_ref[...] = h0_ref[...]
        if not emit:
            hs_ref[...] = jnp.zeros_like(hs_ref)

    for k in range(nblk):
        cols = slice(k * bw, (k + 1) * bw)
        xc = xc_ref[:, cols]
        xcb = xc.astype(BF16)
        r = jax.nn.sigmoid(_dot(xcb, wa_ref[k]) + ba_ref[:, cols])
        gi = jax.nn.sigmoid(_dot(xcb, wx_ref[k]) + bx_ref[:, cols])
        log_a = nsp_ref[:, cols] * r
        a = jnp.exp(log_a)
        a_ref[:, cols] = a
        b_ref[:, cols] = jnp.sqrt(1.0 - a * a) * (gi * xc)

    def step(t, h):
        tt = nt - 1 - t if rev else t
        r0 = pl.multiple_of(tt * SUBLANES, SUBLANES)
        h = a_ref[pl.ds(r0, SUBLANES), :] * h + b_ref[pl.ds(r0, SUBLANES), :]
        if emit:
            hs_ref[pl.ds(r0, SUBLANES), :] = h
        return h

    st_ref[...] = lax.fori_loop(0, nt, step, st_ref[...], unroll=8)

    @pl.when(i == pl.num_programs(0) - 1)
    def _():
        hf_ref[...] = st_ref[...]


def _lru_scan(xc, wa, wx, ba, bx, nsp, h0, *, rev, emit=True):
    n_rows, wd = xc.shape
    tm = _row_tile(n_rows)
    ntile = n_rows // tm
    tile = (lambda i: (ntile - 1 - i, 0)) if rev else (lambda i: (i, 0))
    full = lambda a: pl.BlockSpec(a.shape, lambda i: (0,) * a.ndim)
    hs, hf = pl.pallas_call(
        functools.partial(_lru_scan_kernel, rev=rev, emit=emit),
        grid=(ntile,),
        in_specs=[pl.BlockSpec((tm, wd), tile), full(wa), full(wx), full(ba), full(bx), full(nsp), full(h0)],
        out_specs=[pl.BlockSpec((tm, wd), tile if emit else (lambda i: (0, 0))), full(h0)],
        out_shape=[jax.ShapeDtypeStruct((n_rows if emit else tm, wd), F32), jax.ShapeDtypeStruct(h0.shape, F32)],
        scratch_shapes=[pltpu.VMEM(h0.shape, F32), pltpu.VMEM((tm, wd), F32), pltpu.VMEM((tm, wd), F32)],
        compiler_params=_cparams("arbitrary"),
        name="lru_scan_bwd" if rev else "lru_scan_fwd",
    )(xc, wa, wx, ba, bx, nsp, h0)
    return (hs if emit else None), hf


def _lru_out_prologue(yg_ref, hf_ref, hb_ref):
    return yg_ref[...] * (hf_ref[...] + hb_ref[...])


def _lru_mixer(x, x_ctx, modt, modt_ctx, norm_g, prm, *, ctx_out):
    d = x.shape[-1]
    w_in = prm['w_in'].astype(BF16)
    wd = prm['conv_w'].shape[-1]
    xc_l, yg_l = _lru_in(x, modt, norm_g, w_in, prm['conv_w'], prm['conv_b'])
    xc_c, yg_c = _lru_in(x_ctx, modt_ctx, norm_g, w_in, prm['conv_w'], prm['conv_b'])
    zero = jnp.zeros((SUBLANES, wd), F32)
    hs_l, hs_c = [], []
    for dr in range(2):
        wa, wx = prm['w_a'][dr].astype(BF16), prm['w_x'][dr].astype(BF16)
        ba, bx = prm['b_a'][dr].reshape(1, wd), prm['b_x'][dr].reshape(1, wd)
        nsp = (-LRU_C * jax.nn.softplus(-prm['lam'][dr].astype(F32))).reshape(1, wd)
        hc, hfin = _lru_scan(xc_c, wa, wx, ba, bx, nsp, zero, rev=dr == 1, emit=ctx_out)
        hl, _ = _lru_scan(xc_l, wa, wx, ba, bx, nsp, hfin, rev=dr == 1)
        hs_l.append(hl)
        hs_c.append(hc)
    w_out = prm['w_out'].astype(BF16)

    def out(xa, yg, hs, mt):
        tm = _row_tile(xa.shape[0])
        specs = [_row_spec(tm, wd)] * 3
        return _outproj(_lru_out_prologue, [yg, hs[0], hs[1]], specs, wd, xa, mt, 10 ** 9, w_out, tm=tm, name="lru_out")

    return out(x, yg_l, hs_l, modt), (out(x_ctx, yg_c, hs_c, modt_ctx) if ctx_out else None)


def _rope_tables(length):
    rows = length // GRID_W
    t = jnp.arange(length, dtype=jnp.int32)
    row = (t // GRID_W).astype(F32) - (rows - 1) / 2.0
    col = (t % GRID_W).astype(F32) - (GRID_W - 1) / 2.0
    quarter = RET_DK // 4
    inv_freq = ROPE_BASE ** (-jnp.arange(quarter, dtype=F32) / quarter)
    ar, ac = row[:, None] * inv_freq[None, :], col[:, None] * inv_freq[None, :]
    cos = jnp.concatenate([jnp.cos(ar), jnp.cos(ar), jnp.cos(ac), jnp.cos(ac)], axis=-1)
    sin = jnp.concatenate([-jnp.sin(ar), jnp.sin(ar), -jnp.sin(ac), jnp.sin(ac)], axis=-1)
    return cos, sin


def _ret_in_kernel(x_ref, g_ref, sh_ref, sc_ref, w_ref, cos_ref, sin_ref, p_ref, h_ref, *, rotate, n_k, n_v):
    j = pl.program_id(1)
    tn = w_ref.shape[1]

    @pl.when(j == 0)
    def _():
        h_ref[...] = _norm_mod(x_ref[...], g_ref[...], sc_ref[...], sh_ref[...]).astype(BF16)

    p = _dot(h_ref[...], w_ref[...])
    is_k = j < n_k
    is_q = (j >= n_k + n_v) & (j < 2 * n_k + n_v)

    @pl.when(is_k | is_q)
    def _():
        scale = jnp.where(is_q, RET_DK ** -0.5, 1.0)
        if rotate:
            for s in range(tn // LANES):
                cols = slice(s * LANES, (s + 1) * LANES)
                tcols = slice((s * LANES) % RET_DK, (s * LANES) % RET_DK + LANES)
                ps = p[:, cols]
                rot = pltpu.roll(ps, LANES // 2, axis=1)
                p_ref[:, cols] = (ps * cos_ref[:, tcols] + rot * sin_ref[:, tcols]) * scale
        else:
            p_ref[...] = p * scale

    @pl.when(jnp.logical_not(is_k | is_q))
    def _():
        p_ref[...] = p


def _ret_in(x, modt, tiles_per_mod, norm_g, w_in, seq_rows, *, rotate):
    n_rows, d = x.shape
    n_out = w_in.shape[1]
    tm = _row_tile(seq_rows)
    tn = 4 * LANES
    tiles_per_seq = seq_rows // tm
    cos, sin = _rope_tables(seq_rows)
    hk, hv = RET_HEADS * RET_DK, RET_HEADS * RET_DV
    tab_spec = pl.BlockSpec((tm, RET_DK), lambda i, j: (i % tiles_per_seq, 0))
    return pl.pallas_call(
        functools.partial(_ret_in_kernel, rotate=rotate, n_k=hk // tn, n_v=hv // tn),
        grid=(n_rows // tm, n_out // tn),
        in_specs=[_row_spec(tm, d), _const_spec((1, d)), _mod_spec(0, d, tiles_per_mod), _mod_spec(1, d, tiles_per_mod),
                  pl.BlockSpec((d, tn), lambda i, j: (0, j)), tab_spec, tab_spec],
        out_specs=pl.BlockSpec((tm, tn), lambda i, j: (i, j)),
        out_shape=jax.ShapeDtypeStruct((n_rows, n_out), F32),
        scratch_shapes=[pltpu.VMEM((tm, d), BF16)],
        compiler_params=_cparams("parallel", "arbitrary"),
        name="ret_in",
    )(x, norm_g.reshape(1, d), modt, modt, w_in, cos, sin)


def _ret_tables(c):
    log_g = np.log1p(-np.power(2.0, -5.0 - np.arange(RET_HEADS, dtype=np.float64)))
    idx = np.arange(c, dtype=np.float64)
    diff = idx[:, None] - idx[None, :]
    fwd = np.where(diff >= 0, np.exp(np.where(diff >= 0, diff, 0.0)[None] * log_g[:, None, None]), 0.0)
    bwd = np.where(diff < 0, np.exp(np.where(diff < 0, -diff, 0.0)[None] * log_g[:, None, None]), 0.0)
    xi_f = np.exp((idx + 1.0)[None, :] * log_g[:, None])
    zeta_f = np.exp((c - 1.0 - idx)[None, :] * log_g[:, None])
    xi_b = np.exp((c - idx)[None, :] * log_g[:, None])
    zeta_b = np.exp(idx[None, :] * log_g[:, None])
    dmask = np.stack([fwd, bwd]).astype(np.float32)
    xi = np.stack([xi_f, xi_b])[..., None].astype(np.float32)
    zeta = np.stack([zeta_f, zeta_b])[..., None].astype(np.float32)
    g_blk = [float(v) for v in np.exp(c * log_g).astype(np.float32)]
    return jnp.asarray(dmask), jnp.asarray(xi), jnp.asarray(zeta), g_blk


def _ret_scan_kernel(k_ref, v0_ref, v1_ref, q_ref, dm_ref, xi_ref, zt_ref, r0_ref, o_ref, rf_ref, r_ref, *, g_blk, emit):
    c = pl.program_id(2)

    @pl.when(c == 0)
    def _():
        r_ref[...] = r0_ref[...]
        if not emit:
            o_ref[...] = jnp.zeros_like(o_ref)

    hv_half = v0_ref.shape[1] // RET_DV
    for h in range(RET_HEADS):
        kh = k_ref[:, h * RET_DK:(h + 1) * RET_DK]
        v_ref = v0_ref if h < hv_half else v1_ref
        hh = h % hv_half
        vh = v_ref[:, hh * RET_DV:(hh + 1) * RET_DV].astype(BF16)
        r_old = r_ref[h]
        if emit:
            qh = q_ref[:, h * RET_DK:(h + 1) * RET_DK].astype(BF16)
            s = _dot_nt(qh, kh.astype(BF16)) * dm_ref[h]
            o = _dot(s.astype(BF16), vh) + _dot(qh, r_old.astype(BF16)) * xi_ref[h]
            o_ref[:, h * RET_DV:(h + 1) * RET_DV] = o
        r_ref[h] = g_blk[h] * r_old + _dot_tn((kh * zt_ref[h]).astype(BF16), vh)

    @pl.when(c == pl.num_programs(2) - 1)
    def _():
        rf_ref[...] = r_ref[...]


def _ret_scan(p, r0, batch, seq_rows, *, emit=True):
    c = min(RET_CHUNK, seq_rows)
    nc = seq_rows // c
    hk, hv = RET_HEADS * RET_DK, RET_HEADS * RET_DV
    dmask, xi, zeta, g_blk = _ret_tables(c)
    n_rows = p.shape[0]

    def rows(d, b, cc):
        return b * nc + jnp.where(d == 0, cc, nc - 1 - cc)

    kcol = lambda blk: pl.BlockSpec((c, hk), lambda d, b, cc: (rows(d, b, cc), blk))
    tab = lambda a: pl.BlockSpec((None,) + a.shape[1:], lambda d, b, cc: (d,) + (0,) * (a.ndim - 1))
    st_spec = pl.BlockSpec((None, None, RET_HEADS, RET_DK, RET_DV), lambda d, b, cc: (d, b, 0, 0, 0))
    assert hv == 2 * hk
    o, rf = pl.pallas_call(
        functools.partial(_ret_scan_kernel, g_blk=g_blk, emit=emit),
        grid=(2, batch, nc),
        in_specs=[kcol(0), kcol(1), kcol(2), kcol(3), tab(dmask), tab(xi), tab(zeta), st_spec],
        out_specs=[pl.BlockSpec((None, c, hv), (lambda d, b, cc: (d, rows(d, b, cc), 0)) if emit else
                                (lambda d, b, cc: (0, 0, 0))), st_spec],
        out_shape=[jax.ShapeDtypeStruct((2, n_rows if emit else c, hv), F32), jax.ShapeDtypeStruct(r0.shape, F32)],
        scratch_shapes=[pltpu.VMEM((RET_HEADS, RET_DK, RET_DV), F32)],
        compiler_params=_cparams("arbitrary", "arbitrary", "arbitrary"),
        name="ret_scan",
    )(p, p, p, p, dmask, xi, zeta, r0)
    return o, rf


def _ret_out_prologue(of_ref, ob_ref, gate_ref, ng_ref):
    o = of_ref[...] + ob_ref[...]
    parts = []
    for h in range(RET_HEADS):
        oh = o[:, h * RET_DV:(h + 1) * RET_DV]
        parts.append(oh * lax.rsqrt(jnp.mean(oh * oh, axis=-1, keepdims=True) + NORM_EPS))
    on = (jnp.concatenate(parts, axis=-1) * ng_ref[...])
    return jax.nn.silu(gate_ref[...]) * on


def _ret_mixer(x, x_ctx, modt, modt_ctx, norm_g, prm, batch, seq, seq_ctx, *, ctx_out):
    d = x.shape[-1]
    w_in = prm['w_in'].astype(BF16)
    hk, hv = RET_HEADS * RET_DK, RET_HEADS * RET_DV
    tiles_per_mod = seq // _row_tile(seq)
    p_l = _ret_in(x, modt, tiles_per_mod, norm_g, w_in, seq, rotate=True)
    p_c = _ret_in(x_ctx, modt_ctx, 10 ** 9, norm_g, w_in, seq_ctx, rotate=False)
    r0 = jnp.zeros((2, batch, RET_HEADS, RET_DK, RET_DV), F32)
    o_c, r_c = _ret_scan(p_c, r0, batch, seq_ctx, emit=ctx_out)
    o_l, _ = _ret_scan(p_l, r_c, batch, seq)
    w_out = prm['w_out'].astype(BF16)
    ng = prm['norm_g'].reshape(1, hv)

    def out(xa, pa, oa, mt, tpm, seq_rows):
        tm = _row_tile(seq_rows, cap=ROW_TILE_CAP // 4)
        gcol = (2 * hk + hv) // hv
        specs = [pl.BlockSpec((None, tm, hv), lambda i, j: (0, i, 0)), pl.BlockSpec((None, tm, hv), lambda i, j: (1, i, 0)),
                 pl.BlockSpec((tm, hv), lambda i, j: (i, gcol)), _const_spec((1, hv))]
        return _outproj(_ret_out_prologue, [oa, oa, pa, ng], specs, hv, xa, mt, tpm * (_row_tile(seq_rows) // tm), w_out,
                        tm=tm, name="ret_out")

    y_l = out(x, p_l, o_l, modt, tiles_per_mod, seq)
    y_c = out(x_ctx, p_c, o_c, modt_ctx, 10 ** 9, seq_ctx) if ctx_out else None
    return y_l, y_c


GDN_HEAD_ORDER = (GDN_K_HEADS, GDN_K_HEADS, GDN_V_HEADS, GDN_V_HEADS, 2)


def _gdn_in_kernel(x_ref, xb_ref, xa_ref, g_ref, sh_ref, sc_ref, w_ref, cw_ref, o_ref, h_ref, *, tiles_per_seq, n_kq, n_conv):
    i, j = pl.program_id(0), pl.program_id(1)
    tm = x_ref.shape[0]
    hb = ha = SUBLANES
    heads = o_ref.shape[0]

    @pl.when(j == 0)
    def _():
        first = i % tiles_per_seq == 0
        last = i % tiles_per_seq == tiles_per_seq - 1
        _fill_h(h_ref, x_ref, xb_ref, xa_ref, g_ref, sc_ref, sh_ref, first, last, hb, ha)

    @pl.when(j < n_conv)
    def _():
        p = _dot(h_ref[...], w_ref[...])
        a = jax.nn.silu(_conv_rows(p, cw_ref[...], tm, 1, hb))

        @pl.when(j < n_kq)
        def _():
            scale = jnp.where(j >= n_kq // 2, GDN_DK ** -0.5, 1.0)
            for s in range(heads):
                ah = a[:, s * LANES:(s + 1) * LANES]
                o_ref[s] = ah * (lax.rsqrt(jnp.sum(ah * ah, axis=-1, keepdims=True) + 1e-6) * scale)

        @pl.when(j >= n_kq)
        def _():
            for s in range(heads):
                o_ref[s] = a[:, s * LANES:(s + 1) * LANES]

    @pl.when(j >= n_conv)
    def _():
        p = _dot(h_ref[hb:hb + tm, :], w_ref[...])
        for s in range(heads):
            o_ref[s] = p[:, s * LANES:(s + 1) * LANES]


def _gdn_in(x, modt, tiles_per_mod, norm_g, w_all, cw_all, seq_rows):
    n_rows, d = x.shape
    n_out = w_all.shape[1]
    tm = _row_tile(seq_rows)
    tn = 2 * LANES
    hb = ha = SUBLANES
    heads = tn // LANES
    n_kq = 2 * GDN_K_HEADS * GDN_DK // tn
    n_conv = n_kq + GDN_V_HEADS * GDN_DV // tn
    kern = functools.partial(_gdn_in_kernel, tiles_per_seq=seq_rows // tm, n_kq=n_kq, n_conv=n_conv)
    specs = _halo_specs(tm, hb, ha, d, n_rows) + [
        _const_spec((1, d)), _mod_spec(0, d, tiles_per_mod), _mod_spec(1, d, tiles_per_mod),
        pl.BlockSpec((d, tn), lambda i, j: (0, j)), pl.BlockSpec((cw_all.shape[0], tn), lambda i, j: (0, j))]
    return pl.pallas_call(
        kern,
        grid=(n_rows // tm, n_out // tn),
        in_specs=specs,
        out_specs=pl.BlockSpec((heads, tm, LANES), lambda i, j: (j, i, 0)),
        out_shape=jax.ShapeDtypeStruct((n_out // LANES, n_rows, LANES), F32),
        scratch_shapes=[pltpu.VMEM((tm + hb + ha, d), BF16)],
        compiler_params=_cparams("parallel", "arbitrary"),
        name="gdn_in",
    )(x, x, x, norm_g.reshape(1, d), modt, modt, w_all, cw_all)


def _split3(a):
    hi = a.astype(BF16)
    lo = (a - hi.astype(F32)).astype(BF16)
    return hi, lo


def _dot3(a, b):
    ah, al = _split3(a)
    bh, bl = _split3(b)
    return _dot(ah, bh) + (_dot(ah, bl) + _dot(al, bh))


def _unit_tri_inverse(m):
    n = m.shape[0]
    eye = (lax.broadcasted_iota(jnp.int32, (n, n), 0) == lax.broadcasted_iota(jnp.int32, (n, n), 1)).astype(F32)
    x = -m
    t = eye + x
    steps = int(math.log2(n))
    assert 2 ** steps == n
    for _ in range(steps - 1):
        x = _dot3(x, x)
        t = t + _dot3(x, t)
    return t


def _cumsum_rows(x, rev):
    n = x.shape[0]
    row = lax.broadcasted_iota(jnp.int32, x.shape, 0)
    shift = 1
    while shift < n:
        if rev:
            x = x + jnp.where(row < n - shift, pltpu.roll(x, n - shift, axis=0), 0.0)
        else:
            x = x + jnp.where(row >= shift, pltpu.roll(x, shift, axis=0), 0.0)
        shift *= 2
    return x


def _gdn_scan_kernel(k_ref, q_ref, v_ref, ba_ref, an_ref, dtb_ref, s0_ref, o_ref, sf_ref, s_ref, gc_ref, gct_ref, bt_ref,
                     *, rev, emit, dr):
    cidx = pl.program_id(1)
    c = k_ref.shape[1]
    rep = GDN_V_HEADS // GDN_K_HEADS

    @pl.when(cidx == 0)
    def _():
        s_ref[...] = s0_ref[...]
        if not emit:
            o_ref[...] = jnp.zeros_like(o_ref)

    ba = ba_ref[...]
    sp = jnp.maximum(ba + dtb_ref[...], 0.0) + jnp.log(1.0 + jnp.exp(-jnp.abs(ba + dtb_ref[...])))
    gc = _cumsum_rows(an_ref[...] * sp, rev)
    gc_ref[...] = gc
    gct_ref[...] = jnp.concatenate([gc, gc], axis=0).T
    bt_ref[...] = jax.nn.sigmoid(ba)

    lane = lax.broadcasted_iota(jnp.int32, (c, LANES), 1)
    ii = lax.broadcasted_iota(jnp.int32, (c, c), 0)
    jj = lax.broadcasted_iota(jnp.int32, (c, c), 1)
    diff = (jj - ii) if rev else (ii - jj)
    incl, strict = diff >= 0, diff > 0
    last = 0 if rev else c - 1

    def kv_head(kh, carry):
        k = k_ref[kh]
        k16 = k.astype(BF16)
        q = q_ref[kh] if emit else None
        for e in range(rep):
            hv = kh * rep + e
            ig = dr * 2 * GDN_V_HEADS + GDN_V_HEADS + hv
            ib = dr * 2 * GDN_V_HEADS + hv
            gcol = jnp.sum(jnp.where(lane == ig, gc_ref[...], 0.0), axis=1, keepdims=True)
            bcol = jnp.sum(jnp.where(lane == ib, bt_ref[...], 0.0), axis=1, keepdims=True)
            grow = gct_ref[pl.ds(ig, 1), 0:c]
            dec = jnp.where(incl, jnp.exp(jnp.where(incl, gcol - grow, 0.0)), 0.0)
            kb = k * bcol
            if emit:
                kk = _dot_nt(jnp.concatenate([kb, q], axis=0).astype(BF16), k16)
                attn = kk[c:] * dec
            else:
                kk = _dot_nt(kb.astype(BF16), k16)
            m = jnp.where(strict, kk[:c] * dec, 0.0)
            t = _unit_tri_inverse(m)
            eg = jnp.exp(gcol)
            v = v_ref[hv]
            uw = _dot3(t, jnp.concatenate([v * bcol, kb * eg], axis=1))
            s_old = s_ref[hv]
            s16 = s_old.astype(BF16)
            vnew = uw[:, :GDN_DV] - _dot(uw[:, GDN_DV:].astype(BF16), s16)
            vn16 = vnew.astype(BF16)
            glast = grow[:, last:last + 1]
            if emit:
                o_ref[hv] = _dot((q * eg).astype(BF16), s16) + _dot(attn.astype(BF16), vn16)
            kd = (k * jnp.exp(glast - gcol)).astype(BF16)
            s_ref[hv] = s_old * jnp.exp(glast) + _dot_tn(kd, vn16)
        return carry

    lax.fori_loop(0, GDN_K_HEADS, kv_head, 0)

    @pl.when(cidx == pl.num_programs(1) - 1)
    def _():
        sf_ref[...] = s_ref[...]


def _gdn_scan(kqvz, an, dtb, s0, batch, seq_rows, *, dr, emit=True):
    c = min(GDN_CHUNK, seq_rows)
    nc = seq_rows // c
    n_rows = kqvz.shape[1]
    rev = dr == 1
    rows = (lambda b, cc: b * nc + nc - 1 - cc) if rev else (lambda b, cc: b * nc + cc)
    kh, vh = GDN_K_HEADS, GDN_V_HEADS
    ba_head = 2 * kh + 2 * vh
    st_spec = pl.BlockSpec((None, vh, GDN_DK, GDN_DV), lambda b, cc: (b, 0, 0, 0))
    vec = pl.BlockSpec((1, LANES), lambda b, cc: (0, 0))
    o, sf = pl.pallas_call(
        functools.partial(_gdn_scan_kernel, rev=rev, emit=emit, dr=dr),
        grid=(batch, nc),
        in_specs=[pl.BlockSpec((kh, c, LANES), lambda b, cc: (0, rows(b, cc), 0)),
                  pl.BlockSpec((kh, c, LANES), lambda b, cc: (1, rows(b, cc), 0)),
                  pl.BlockSpec((vh, c, LANES), lambda b, cc: (1, rows(b, cc), 0)),
                  pl.BlockSpec((None, c, LANES), lambda b, cc: (ba_head, rows(b, cc), 0)),
                  vec, vec, st_spec],
        out_specs=[pl.BlockSpec((vh, c, LANES), (lambda b, cc: (0, rows(b, cc), 0)) if emit else (lambda b, cc: (0, 0, 0))),
                   st_spec],
        out_shape=[jax.ShapeDtypeStruct((vh, n_rows if emit else c, LANES), F32), jax.ShapeDtypeStruct(s0.shape, F32)],
        scratch_shapes=[pltpu.VMEM((vh, GDN_DK, GDN_DV), F32), pltpu.VMEM((c, LANES), F32),
                        pltpu.VMEM((LANES, 2 * c), F32), pltpu.VMEM((c, LANES), F32)],
        compiler_params=_cparams("arbitrary", "arbitrary"),
        name="gdn_scan_bwd" if rev else "gdn_scan_fwd",
    )(kqvz, kqvz, kqvz, kqvz, an, dtb, s0)
    return o, sf


def _gdn_out_prologue(of_ref, ob_ref, z_ref, ng_ref):
    parts = []
    for h in range(GDN_V_HEADS):
        o = of_ref[h] + ob_ref[h]
        on = o * lax.rsqrt(jnp.mean(o * o, axis=-1, keepdims=True) + NORM_EPS) * ng_ref[...]
        parts.append(on * jax.nn.silu(z_ref[h]))
    return jnp.concatenate(parts, axis=-1)


def _gdn_mixer(x, x_ctx, modt, modt_ctx, norm_g, prm, batch, seq, seq_ctx, *, ctx_out):
    d = x.shape[-1]
    qk, vv = GDN_K_HEADS * GDN_DK, GDN_V_HEADS * GDN_DV
    ng2 = 2 * 2 * GDN_V_HEADS
    w_in, conv_w = prm['w_in'], prm['conv_w']
    pad = 2 * LANES - ng2
    w_all = jnp.concatenate([w_in[:, :qk], w_in[:, qk + vv + ng2:2 * qk + vv + ng2], w_in[:, qk:qk + vv],
                             w_in[:, 2 * qk + vv + ng2:], w_in[:, qk + vv:qk + vv + ng2],
                             jnp.zeros((d, pad), w_in.dtype)], axis=1).astype(BF16)
    cw_all = jnp.concatenate([conv_w[:, :qk], conv_w[:, qk + vv:], conv_w[:, qk:qk + vv],
                              jnp.zeros((conv_w.shape[0], vv + 2 * LANES), conv_w.dtype)], axis=1)
    tiles_per_mod = seq // _row_tile(seq)
    p_l = _gdn_in(x, modt, tiles_per_mod, norm_g, w_all, cw_all, seq)
    p_c = _gdn_in(x_ctx, modt_ctx, 10 ** 9, norm_g, w_all, cw_all, seq_ctx)
    s0 = jnp.zeros((batch, GDN_V_HEADS, GDN_DK, GDN_DV), F32)
    os_l, os_c = [], []
    for dr in range(2):
        an = jnp.zeros((LANES,), F32).at[dr * 2 * GDN_V_HEADS + GDN_V_HEADS:(dr + 1) * 2 * GDN_V_HEADS].set(
            -jnp.exp(prm['a_log'][dr].astype(F32))).reshape(1, LANES)
        dtb = jnp.zeros((LANES,), F32).at[dr * 2 * GDN_V_HEADS + GDN_V_HEADS:(dr + 1) * 2 * GDN_V_HEADS].set(
            prm['dt_bias'][dr].astype(F32)).reshape(1, LANES)
        oc, sc = _gdn_scan(p_c, an, dtb, s0, batch, seq_ctx, dr=dr, emit=ctx_out)
        ol, _ = _gdn_scan(p_l, an, dtb, sc, batch, seq, dr=dr)
        os_l.append(ol)
        os_c.append(oc)
    w_out = prm['w_out'].astype(BF16)
    ng = prm['norm_g'].reshape(1, GDN_DV)
    vh = GDN_V_HEADS
    z_blk = (2 * GDN_K_HEADS + vh) // vh

    def out(xa, pa, oa, mt, tpm, seq_rows):
        tm = _row_tile(seq_rows, cap=ROW_TILE_CAP // 4)
        hspec = lambda blk: pl.BlockSpec((vh, tm, LANES), lambda i, j: (blk, i, 0))
        specs = [hspec(0), hspec(0), hspec(z_blk), _const_spec((1, GDN_DV))]
        return _outproj(_gdn_out_prologue, [oa[0], oa[1], pa, ng], specs, vv, xa, mt, tpm * (_row_tile(seq_rows) // tm),
                        w_out, tm=tm, name="gdn_out")

    y_l = out(x, p_l, os_l, modt, tiles_per_mod, seq)
    y_c = out(x_ctx, p_c, os_c, modt_ctx, 10 ** 9, seq_ctx) if ctx_out else None
    return y_l, y_c


def kernel(x, c, ctx, c_ctx, norm1_g, norm2_g, mod_w, mod_b, ffn_w_in, ffn_conv_w, ffn_conv_b, ffn_w_out, s5_lam_re, s5_lam_im, s5_log_step, s5_b_re, s5_b_im, s5_c_re, s5_c_im, s5_d, s5_w_glu, s5_b_glu, lru_w_in, lru_conv_w, lru_conv_b, lru_w_a, lru_b_a, lru_w_x, lru_b_x, lru_lam, lru_w_out, ret_w_in, ret_norm_g, ret_w_out, gdn_w_in, gdn_conv_w, gdn_a_log, gdn_dt_bias, gdn_norm_g, gdn_w_out, final_norm_g):
    B, L, D = x.shape
    Lc = ctx.shape[1]
    depth = mod_w.shape[0]
    assert B == SUBLANES and depth == 4, "time-major layers put the batch on the 8 sublanes; four mixer kinds"
    c16 = jnp.concatenate([c, c_ctx[None], jnp.zeros((16 - B - 1, D), F32)], 0)
    mods = _modulation(c16, mod_w, mod_b)

    def ffn(i, xa, mt, tiles_per_mod, rs, seq_rows, final_norm=False):
        return _ffn(xa, mt, tiles_per_mod, norm2_g[i], ffn_w_in[i].astype(BF16), ffn_conv_w[i], ffn_conv_b[i],
                    ffn_w_out[i].astype(BF16), final_norm_g, rs=rs, seq_rows=seq_rows, final_norm=final_norm)

    xt = jnp.swapaxes(x, 0, 1).reshape(L * B, D)
    ct = jnp.swapaxes(ctx, 0, 1).reshape(Lc * B, D)
    s5p = dict(lam_re=s5_lam_re[0], lam_im=s5_lam_im[0], log_step=s5_log_step[0], b_re=s5_b_re[0], b_im=s5_b_im[0],
               c_re=s5_c_re[0], c_im=s5_c_im[0], d=s5_d[0], w_glu=s5_w_glu[0], b_glu=s5_b_glu[0])
    lrup = dict(w_in=lru_w_in[0], conv_w=lru_conv_w[0], conv_b=lru_conv_b[0], w_a=lru_w_a[0], b_a=lru_b_a[0],
                w_x=lru_w_x[0], b_x=lru_b_x[0], lam=lru_lam[0], w_out=lru_w_out[0])
    for i, (mixer, prm) in enumerate(((_s5_mixer, s5p), (_lru_mixer, lrup))):
        mt = mods[i, :B][None]
        mtc = jnp.broadcast_to(mods[i, B][None, None, :], (1, SUBLANES, N_MOD * D))
        xt, ct = mixer(xt, ct, mt, mtc, norm1_g[i], prm, ctx_out=True)
        xt = ffn(i, xt, mt, 10 ** 9, SUBLANES, L * B)
        ct = ffn(i, ct, mtc, 10 ** 9, SUBLANES, Lc * B)

    xb = jnp.swapaxes(xt.reshape(L, B, D), 0, 1).reshape(B * L, D)
    cb = jnp.swapaxes(ct.reshape(Lc, B, D), 0, 1).reshape(B * Lc, D)
    retp = dict(w_in=ret_w_in[0], norm_g=ret_norm_g[0], w_out=ret_w_out[0])
    gdnp = dict(w_in=gdn_w_in[0], conv_w=gdn_conv_w[0], a_log=gdn_a_log[0], dt_bias=gdn_dt_bias[0],
                norm_g=gdn_norm_g[0], w_out=gdn_w_out[0])
    tiles_per_batch = L // _row_tile(L)
    for i, (mixer, prm) in ((2, (_ret_mixer, retp)), (3, (_gdn_mixer, gdnp))):
        ctx_out = i < depth - 1
        mt = jnp.broadcast_to(mods[i, :B][:, None, :], (B, SUBLANES, N_MOD * D))
        mtc = jnp.broadcast_to(mods[i, B][None, None, :], (1, SUBLANES, N_MOD * D))
        xb, cb_new = mixer(xb, cb, mt, mtc, norm1_g[i], prm, B, L, Lc, ctx_out=ctx_out)
        xb = ffn(i, xb, mt, tiles_per_batch, 1, L, final_norm=not ctx_out)
        if ctx_out:
            cb = ffn(i, cb_new, mtc, 10 ** 9, 1, Lc)
    return xb.reshape(B, L, D)
```

```python
import functools
import math

import numpy as np
import jax
import jax.numpy as jnp
from jax import lax
from jax.experimental import pallas as pl
from jax.experimental.pallas import tpu as pltpu

F32 = jnp.float32
BF16 = jnp.bfloat16

NORM_EPS = 1e-6
SUBLANES = 8
LANES = 128
VMEM_LIMIT_BYTES = 56 * 1024 * 1024
ROW_TILE_CAP = 1024

S5_GROUP = 16
S5_STATE = 64
S5_GB = 16
LRU_BLOCKS = 4
LRU_C = 8.0
RET_HEADS = 4
RET_DK = 256
RET_DV = 512
RET_CHUNK = 128
ROPE_BASE = 10000.0
GRID_W = 64
GDN_K_HEADS = 8
GDN_V_HEADS = 16
GDN_DK = 128
GDN_DV = 128
GDN_CHUNK = 64
N_MOD = 6


def _cparams(*sem):
    return pltpu.CompilerParams(dimension_semantics=sem, vmem_limit_bytes=VMEM_LIMIT_BYTES)


def _row_tile(rows, cap=None):
    cap = ROW_TILE_CAP if cap is None else cap
    tm = min(cap, rows)
    assert rows % tm == 0 and tm % SUBLANES == 0, (rows, tm)
    return tm


def _dot(a, b):
    return jnp.dot(a, b, preferred_element_type=F32)


def _dot_tn(a, b):
    return lax.dot_general(a, b, (((0,), (0,)), ((), ())), preferred_element_type=F32)


def _dot_nt(a, b):
    return lax.dot_general(a, b, (((1,), (1,)), ((), ())), preferred_element_type=F32)


def _gelu(x):
    return jax.nn.gelu(x, approximate=True)


def _sigmoid(x):
    return 1.0 / (1.0 + jnp.exp(-x))


def _silu(x):
    return x * _sigmoid(x)


def _tile8(v, rows):
    n = v.shape[-1]
    return jnp.broadcast_to(v[None], (rows // SUBLANES, SUBLANES, n)).reshape(rows, n)


def _norm_mod(x, g, sc, sh):
    ms = jnp.mean(x * x, axis=-1, keepdims=True)
    y = x * lax.rsqrt(ms + NORM_EPS) * g
    rows = x.shape[0]
    return y * (1.0 + _tile8(sc, rows)) + _tile8(sh, rows)


def _shift_rows(p, off, rows):
    if off % SUBLANES == 0:
        return p[off:off + rows]
    n = p.shape[0]
    base = (off // SUBLANES) * SUBLANES
    rolled = pltpu.roll(p, (n - (off - base)) % n, axis=0)
    return rolled[base:base + rows]


def _conv_rows(p_ext, cw, rows, rs, hb):
    k_taps = cw.shape[0]
    acc = None
    for k in range(k_taps):
        term = cw[k:k + 1, :] * _shift_rows(p_ext, hb + (k - k_taps // 2) * rs, rows)
        acc = term if acc is None else acc + term
    return acc


def _fill_h(h_ref, x_ref, xb_ref, xa_ref, g_ref, sc_ref, sh_ref, first, last, hb, ha):
    tm = x_ref.shape[0]
    g, sc, sh = g_ref[...], sc_ref[...], sh_ref[...]
    h_ref[hb:hb + tm, :] = _norm_mod(x_ref[...], g, sc, sh).astype(BF16)
    if hb:
        hbv = _norm_mod(xb_ref[...], g, sc, sh)
        h_ref[0:hb, :] = jnp.where(first, 0.0, hbv).astype(BF16)
    if ha:
        hav = _norm_mod(xa_ref[...], g, sc, sh)
        h_ref[hb + tm:hb + tm + ha, :] = jnp.where(last, 0.0, hav).astype(BF16)


def _halo_specs(tm, hb, ha, d, n_rows):
    specs = [pl.BlockSpec((tm, d), lambda i, j: (i, 0))]
    if hb:
        specs.append(pl.BlockSpec((hb, d), lambda i, j: (jnp.maximum(i * (tm // hb) - 1, 0), 0)))
    if ha:
        nblk = n_rows // ha
        specs.append(pl.BlockSpec((ha, d), lambda i, j: (jnp.minimum((i + 1) * (tm // ha), nblk - 1), 0)))
    return specs


def _mod_spec(chunk, d, tiles_per_mod):
    return pl.BlockSpec((None, SUBLANES, d), lambda i, j: (i // tiles_per_mod, 0, chunk))


def _mod_kernel(c_ref, w_ref, b_ref, o_ref):
    a = _silu(c_ref[...]).astype(BF16)
    o_ref[...] = _dot(a, w_ref[...].astype(BF16)) + b_ref[...]


def _modulation(c16, mod_w, mod_b):
    depth, d, n = mod_w.shape
    tn = n // 4
    return pl.pallas_call(
        _mod_kernel,
        grid=(depth, n // tn),
        in_specs=[pl.BlockSpec((16, d), lambda l, j: (0, 0)),
                  pl.BlockSpec((None, d, tn), lambda l, j: (l, 0, j)),
                  pl.BlockSpec((None, 1, tn), lambda l, j: (l, 0, j))],
        out_specs=pl.BlockSpec((None, 16, tn), lambda l, j: (l, 0, j)),
        out_shape=jax.ShapeDtypeStruct((depth, 16, n), F32),
        compiler_params=_cparams("arbitrary", "arbitrary"),
        name="modulation",
    )(c16, mod_w, mod_b.reshape(depth, 1, n))


def _ffn_kernel(x_ref, xb_ref, xa_ref, g_ref, sh_ref, sc_ref, gt_ref, wg_ref, wu_ref, cw_ref, cb_ref, wo_ref,
                fg_ref, o_ref, h_ref, acc_ref, *, rs, tiles_per_seq, final_norm):
    i, j = pl.program_id(0), pl.program_id(1)
    tm = x_ref.shape[0]
    hb = ha = SUBLANES

    @pl.when(j == 0)
    def _():
        first = i % tiles_per_seq == 0
        last = i % tiles_per_seq == tiles_per_seq - 1
        _fill_h(h_ref, x_ref, xb_ref, xa_ref, g_ref, sc_ref, sh_ref, first, last, hb, ha)
        acc_ref[...] = jnp.zeros_like(acc_ref)

    gate = _dot(h_ref[...], wg_ref[...])
    up = _dot(h_ref[hb:hb + tm, :], wu_ref[...])
    gc = _conv_rows(gate, cw_ref[...], tm, rs, hb) + cb_ref[...]
    a = (_gelu(gc) * up).astype(BF16)
    acc_ref[...] += _dot(a, wo_ref[...])

    @pl.when(j == pl.num_programs(1) - 1)
    def _():
        y = x_ref[...] + _tile8(gt_ref[...], tm) * acc_ref[...]
        if final_norm:
            ms = jnp.mean(y * y, axis=-1, keepdims=True)
            y = y * lax.rsqrt(ms + NORM_EPS) * fg_ref[...]
        o_ref[...] = y


def _ffn(x, modt, tiles_per_mod, norm_g, w_in, conv_w, conv_b, w_out, final_g, *, rs, seq_rows, final_norm=False):
    n_rows, d = x.shape
    f = w_out.shape[0]
    tm = _row_tile(seq_rows)
    tf = 2 * LANES
    nj = f // tf
    assert f % tf == 0
    hb = ha = SUBLANES
    kern = functools.partial(_ffn_kernel, rs=rs, tiles_per_seq=seq_rows // tm, final_norm=final_norm)
    in_specs = _halo_specs(tm, hb, ha, d, n_rows) + [
        pl.BlockSpec((1, d), lambda i, j: (0, 0)),
        _mod_spec(3, d, tiles_per_mod), _mod_spec(4, d, tiles_per_mod), _mod_spec(5, d, tiles_per_mod),
        pl.BlockSpec((d, tf), lambda i, j: (0, j)),
        pl.BlockSpec((d, tf), lambda i, j: (0, j + nj)),
        pl.BlockSpec((conv_w.shape[0], tf), lambda i, j: (0, j)),
        pl.BlockSpec((1, tf), lambda i, j: (0, j)),
        pl.BlockSpec((tf, d), lambda i, j: (j, 0)),
        pl.BlockSpec((1, d), lambda i, j: (0, 0)),
    ]
    return pl.pallas_call(
        kern,
        grid=(n_rows // tm, nj),
        in_specs=in_specs,
        out_specs=pl.BlockSpec((tm, d), lambda i, j: (i, 0)),
        out_shape=jax.ShapeDtypeStruct((n_rows, d), F32),
        scratch_shapes=[pltpu.VMEM((tm + hb + ha, d), BF16), pltpu.VMEM((tm, d), F32)],
        compiler_params=_cparams("parallel", "arbitrary"),
        name="conv_ffn",
    )(x, x, x, norm_g.reshape(1, d), modt, modt, modt, w_in, w_in, conv_w, conv_b.reshape(1, f), w_out,
      final_g.reshape(1, d))


def _outproj_kernel(*refs, n_pro, prologue, glu):
    pro_refs = refs[:n_pro]
    if glu:
        xc_ref, gt_ref, wv_ref, wg_ref, bv_ref, bg_ref, o_ref, a_ref = refs[n_pro:]
    else:
        xc_ref, gt_ref, wv_ref, o_ref, a_ref = refs[n_pro:]

    @pl.when(pl.program_id(1) == 0)
    def _():
        a_ref[...] = prologue(*pro_refs).astype(BF16)

    y = _dot(a_ref[...], wv_ref[...])
    if glu:
        y = (y + bv_ref[...]) * jax.nn.sigmoid(_dot(a_ref[...], wg_ref[...]) + bg_ref[...])
    o_ref[...] = xc_ref[...] + _tile8(gt_ref[...], o_ref.shape[0]) * y


def _outproj(prologue, pro_args, pro_specs, k_dim, x, modt, tiles_per_mod, w, bias=None, *, tm, name):
    n_rows, d = x.shape
    tn = 4 * LANES
    nj = d // tn
    glu = bias is not None
    kern = functools.partial(_outproj_kernel, n_pro=len(pro_args), prologue=prologue, glu=glu)
    specs = list(pro_specs) + [pl.BlockSpec((tm, tn), lambda i, j: (i, j)),
                               pl.BlockSpec((None, SUBLANES, tn), lambda i, j: (i // tiles_per_mod, 0, 2 * nj + j)),
                               pl.BlockSpec((k_dim, tn), lambda i, j: (0, j))]
    args = list(pro_args) + [x, modt, w]
    if glu:
        specs += [pl.BlockSpec((k_dim, tn), lambda i, j: (0, j + nj)),
                  pl.BlockSpec((1, tn), lambda i, j: (0, j)),
                  pl.BlockSpec((1, tn), lambda i, j: (0, j + nj))]
        args += [w, bias, bias]
    return pl.pallas_call(
        kern,
        grid=(n_rows // tm, nj),
        in_specs=specs,
        out_specs=pl.BlockSpec((tm, tn), lambda i, j: (i, j)),
        out_shape=jax.ShapeDtypeStruct((n_rows, d), F32),
        scratch_shapes=[pltpu.VMEM((tm, k_dim), BF16)],
        compiler_params=_cparams("parallel", "arbitrary"),
        name=name,
    )(*args)


def _row_spec(tm, n):
    return pl.BlockSpec((tm, n), lambda i, j: (i, 0))


def _const_spec(shape):
    nd = len(shape)
    return pl.BlockSpec(shape, lambda i, j: (0,) * nd)


def _s5_weights(lam_re, lam_im, log_step, b_re, b_im, c_re, c_im):
    g, p = lam_re.shape
    gc = b_re.shape[-1]
    lr = jnp.minimum(lam_re.astype(F32), -1e-4)
    li = lam_im.astype(F32)
    step = jnp.exp(log_step.astype(F32))[:, None]
    mag = jnp.exp(lr * step)
    ar, ai = mag * jnp.cos(li * step), mag * jnp.sin(li * step)
    den = lr * lr + li * li
    kr = ((ar - 1.0) * lr + ai * li) / den
    ki = (ai * lr - (ar - 1.0) * li) / den
    br32, bi32 = b_re.astype(F32), b_im.astype(F32)
    br = kr[..., None] * br32 - ki[..., None] * bi32
    bi = kr[..., None] * bi32 + ki[..., None] * br32
    nb = g // S5_GB
    eye = jnp.eye(S5_GB, dtype=F32)

    def pack_b(b):
        b4 = b.reshape(nb, S5_GB, p, gc)
        return jnp.einsum('blpc,lm->blcmp', b4, eye).reshape(nb, S5_GB * gc, S5_GB * p)

    def pack_c(cm):
        c4 = cm.astype(F32).reshape(nb, S5_GB, gc, p)
        return jnp.einsum('blcp,lm->blpmc', c4, eye).reshape(nb, S5_GB * p, S5_GB * gc)

    bm = jnp.concatenate([pack_b(br), pack_b(bi)], axis=-1).astype(BF16)
    cm = jnp.concatenate([pack_c(c_re), -pack_c(c_im)], axis=1).astype(BF16)
    lam = jnp.stack([ar.reshape(nb, S5_GB * p), ai.reshape(nb, S5_GB * p)], axis=1)
    lam = jnp.broadcast_to(lam[:, :, None, :], (nb, 2, SUBLANES, S5_GB * p))
    return bm, cm, lam


def _s5_scan_kernel(x_ref, g_ref, sh_ref, sc_ref, bm_ref, cm_ref, lam_ref, s0_ref, y_ref, sf_ref, st_ref, bu_ref,
                    *, rev, emit):
    i = pl.program_id(0)
    tm = x_ref.shape[0]
    nt = tm // SUBLANES
    nb, kin, ns2 = bm_ref.shape
    ns = ns2 // 2

    @pl.when(i == 0)
    def _():
        st_ref[...] = s0_ref[...]
        if not emit:
            y_ref[...] = jnp.zeros_like(y_ref)

    h = _norm_mod(x_ref[...], g_ref[...], sc_ref[...], sh_ref[...]).astype(BF16)
    for gb in range(nb):
        bu_ref[...] = _dot(h[:, gb * kin:(gb + 1) * kin], bm_ref[gb])
        ar, ai = lam_ref[gb, 0], lam_ref[gb, 1]

        def step(t, carry):
            sr, si = carry
            tt = nt - 1 - t if rev else t
            r0 = pl.multiple_of(tt * SUBLANES, SUBLANES)
            nr = ar * sr - ai * si + bu_ref[pl.ds(r0, SUBLANES), 0:ns]
            ni = ar * si + ai * sr + bu_ref[pl.ds(r0, SUBLANES), ns:ns2]
            if emit:
                bu_ref[pl.ds(r0, SUBLANES), 0:ns] = nr
                bu_ref[pl.ds(r0, SUBLANES), ns:ns2] = ni
            return nr, ni

        sr, si = lax.fori_loop(0, nt, step, (st_ref[gb, 0], st_ref[gb, 1]), unroll=4)
        st_ref[gb, 0] = sr
        st_ref[gb, 1] = si
        if emit:
            y_ref[:, gb * kin:(gb + 1) * kin] = _dot(bu_ref[...].astype(BF16), cm_ref[gb])

    @pl.when(i == pl.num_programs(0) - 1)
    def _():
        sf_ref[...] = st_ref[...]


def _s5_scan(x, modt, norm_g, bm, cm, lam, s0, *, rev, emit=True):
    n_rows, d = x.shape
    tm = _row_tile(n_rows, cap=ROW_TILE_CAP // 2)
    ntile = n_rows // tm
    tile = (lambda i: (ntile - 1 - i, 0)) if rev else (lambda i: (i, 0))
    mod = lambda k: pl.BlockSpec((None, SUBLANES, d), lambda i: (0, 0, k))
    full = lambda a: pl.BlockSpec(a.shape, lambda i: (0,) * a.ndim)
    y, sf = pl.pallas_call(
        functools.partial(_s5_scan_kernel, rev=rev, emit=emit),
        grid=(ntile,),
        in_specs=[pl.BlockSpec((tm, d), tile), pl.BlockSpec((1, d), lambda i: (0, 0)), mod(0), mod(1),
                  full(bm), full(cm), full(lam), full(s0)],
        out_specs=[pl.BlockSpec((tm, d), tile if emit else (lambda i: (0, 0))), full(s0)],
        out_shape=[jax.ShapeDtypeStruct((n_rows if emit else tm, d), F32), jax.ShapeDtypeStruct(s0.shape, F32)],
        scratch_shapes=[pltpu.VMEM(s0.shape, F32), pltpu.VMEM((tm, bm.shape[-1]), F32)],
        compiler_params=_cparams("arbitrary"),
        name="s5_scan_bwd" if rev else "s5_scan_fwd",
    )(x, norm_g.reshape(1, d), modt, modt, bm, cm, lam, s0)
    return (y if emit else None), sf


def _s5_out_prologue(x_ref, yf_ref, yb_ref, g_ref, sh_ref, sc_ref, dk_ref):
    u = _norm_mod(x_ref[...], g_ref[...], sc_ref[...], sh_ref[...])
    return _gelu(yf_ref[...] + yb_ref[...] + dk_ref[...] * u)


def _s5_mixer(x, x_ctx, modt, modt_ctx, norm_g, prm, *, ctx_out):
    d = x.shape[-1]
    w = [_s5_weights(prm['lam_re'][dr], prm['lam_im'][dr], prm['log_step'][dr], prm['b_re'][dr], prm['b_im'][dr],
                     prm['c_re'][dr], prm['c_im'][dr]) for dr in range(2)]
    zero = jnp.zeros(w[0][2].shape, F32)
    ys, ys_ctx = [], []
    for dr in range(2):
        bm, cm, lam = w[dr]
        yc, sc = _s5_scan(x_ctx, modt_ctx, norm_g, bm, cm, lam, zero, rev=dr == 1, emit=ctx_out)
        yl, _ = _s5_scan(x, modt, norm_g, bm, cm, lam, sc, rev=dr == 1)
        ys.append(yl)
        ys_ctx.append(yc)
    w_glu = prm['w_glu'].astype(BF16)
    b_glu = prm['b_glu'].reshape(1, -1)

    def out(xa, ya, mt):
        tm = _row_tile(xa.shape[0])
        pro_args = [xa, ya[0], ya[1], norm_g.reshape(1, d), mt, mt, prm['d'].reshape(1, d)]
        pro_specs = [_row_spec(tm, d), _row_spec(tm, d), _row_spec(tm, d), _const_spec((1, d)),
                     _mod_spec(0, d, 10 ** 9), _mod_spec(1, d, 10 ** 9), _const_spec((1, d))]
        return _outproj(_s5_out_prologue, pro_args, pro_specs, d, xa, mt, 10 ** 9, w_glu, b_glu, tm=tm, name="s5_out")

    return out(x, ys, modt), (out(x_ctx, ys_ctx, modt_ctx) if ctx_out else None)


def _lru_in_kernel(x_ref, xb_ref, xa_ref, g_ref, sh_ref, sc_ref, wr_ref, wy_ref, cw_ref, cb_ref, xc_ref, yg_ref, h_ref,
                   *, rs, hb, ha):
    i, j = pl.program_id(0), pl.program_id(1)
    tm = x_ref.shape[0]

    @pl.when(j == 0)
    def _():
        _fill_h(h_ref, x_ref, xb_ref, xa_ref, g_ref, sc_ref, sh_ref, i == 0, i == pl.num_programs(0) - 1, hb, ha)

    p = _dot(h_ref[...], wr_ref[...])
    xc_ref[...] = _conv_rows(p, cw_ref[...], tm, rs, hb) + cb_ref[...]
    yg_ref[...] = _gelu(_dot(h_ref[hb:hb + tm, :], wy_ref[...]))


def _lru_in(x, modt, norm_g, w_in, conv_w, conv_b):
    n_rows, d = x.shape
    wd = conv_w.shape[-1]
    rs = SUBLANES
    hb, ha = 2 * rs, rs
    tm = _row_tile(n_rows)
    tn = 4 * LANES
    nj = wd // tn
    specs = _halo_specs(tm, hb, ha, d, n_rows) + [
        _const_spec((1, d)), _mod_spec(0, d, 10 ** 9), _mod_spec(1, d, 10 ** 9),
        pl.BlockSpec((d, tn), lambda i, j: (0, j)), pl.BlockSpec((d, tn), lambda i, j: (0, j + nj)),
        pl.BlockSpec((conv_w.shape[0], tn), lambda i, j: (0, j)), pl.BlockSpec((1, tn), lambda i, j: (0, j))]
    out_spec = pl.BlockSpec((tm, tn), lambda i, j: (i, j))
    return pl.pallas_call(
        functools.partial(_lru_in_kernel, rs=rs, hb=hb, ha=ha),
        grid=(n_rows // tm, nj),
        in_specs=specs,
        out_specs=[out_spec, out_spec],
        out_shape=[jax.ShapeDtypeStruct((n_rows, wd), F32)] * 2,
        scratch_shapes=[pltpu.VMEM((tm + hb + ha, d), BF16)],
        compiler_params=_cparams("parallel", "arbitrary"),
        name="lru_in",
    )(x, x, x, norm_g.reshape(1, d), modt, modt, w_in, w_in, conv_w, conv_b.reshape(1, wd))


def _lru_scan_kernel(xc_ref, wa_ref, wx_ref, ba_ref, bx_ref, nsp_ref, h0_ref, hs_ref, hf_ref, st_ref, a_ref, b_ref,
                     *, rev, emit):
    i = pl.program_id(0)
    tm = xc_ref.shape[0]
    nt = tm // SUBLANES
    nblk, bw, _ = wa_ref.shape

    @pl.when(i == 0)
    def _():
        st_ref[...] = h0_ref[...]
        if not emit:
            hs_ref[...] = jnp.zeros_like(hs_ref)

    for k in range(nblk):
        cols = slice(k * bw, (k + 1) * bw)
        xc = xc_ref[:, cols]
        xcb = xc.astype(BF16)
        r = jax.nn.sigmoid(_dot(xcb, wa_ref[k]) + ba_ref[:, cols])
        gi = jax.nn.sigmoid(_dot(xcb, wx_ref[k]) + bx_ref[:, cols])
        log_a = nsp_ref[:, cols] * r
        a = jnp.exp(log_a)
        a_ref[:, cols] = a
        b_ref[:, cols] = jnp.sqrt(1.0 - a * a) * (gi * xc)

    def step(t, h):
        tt = nt - 1 - t if rev else t
        r0 = pl.multiple_of(tt * SUBLANES, SUBLANES)
        h = a_ref[pl.ds(r0, SUBLANES), :] * h + b_ref[pl.ds(r0, SUBLANES), :]
        if emit:
            hs_ref[pl.ds(r0, SUBLANES), :] = h
        return h

    st_ref[...] = lax.fori_loop(0, nt, step, st_ref[...], unroll=8)

    @pl.when(i == pl.num_programs(0) - 1)
    def _():
        hf_ref[...] = st_ref[...]


def _lru_scan(xc, wa, wx, ba, bx, nsp, h0, *, rev, emit=True):
    n_rows, wd = xc.shape
    tm = _row_tile(n_rows)
    ntile = n_rows // tm
    tile = (lambda i: (ntile - 1 - i, 0)) if rev else (lambda i: (i, 0))
    full = lambda a: pl.BlockSpec(a.shape, lambda i: (0,) * a.ndim)
    hs, hf = pl.pallas_call(
        functools.partial(_lru_scan_kernel, rev=rev, emit=emit),
        grid=(ntile,),
        in_specs=[pl.BlockSpec((tm, wd), tile), full(wa), full(wx), full(ba), full(bx), full(nsp), full(h0)],
        out_specs=[pl.BlockSpec((tm, wd), tile if emit else (lambda i: (0, 0))), full(h0)],
        out_shape=[jax.ShapeDtypeStruct((n_rows if emit else tm, wd), F32), jax.ShapeDtypeStruct(h0.shape, F32)],
        scratch_shapes=[pltpu.VMEM(h0.shape, F32), pltpu.VMEM((tm, wd), F32), pltpu.VMEM((tm, wd), F32)],
        compiler_params=_cparams("arbitrary"),
        name="lru_scan_bwd" if rev else "lru_scan_fwd",
    )(xc, wa, wx, ba, bx, nsp, h0)
    return (hs if emit else None), hf


def _lru_out_prologue(yg_ref, hf_ref, hb_ref):
    return yg_ref[...] * (hf_ref[...] + hb_ref[...])


def _lru_mixer(x, x_ctx, modt, modt_ctx, norm_g, prm, *, ctx_out):
    d = x.shape[-1]
    w_in = prm['w_in'].astype(BF16)
    wd = prm['conv_w'].shape[-1]
    xc_l, yg_l = _lru_in(x, modt, norm_g, w_in, prm['conv_w'], prm['conv_b'])
    xc_c, yg_c = _lru_in(x_ctx, modt_ctx, norm_g, w_in, prm['conv_w'], prm['conv_b'])
    zero = jnp.zeros((SUBLANES, wd), F32)
    hs_l, hs_c = [], []
    for dr in range(2):
        wa, wx = prm['w_a'][dr].astype(BF16), prm['w_x'][dr].astype(BF16)
        ba, bx = prm['b_a'][dr].reshape(1, wd), prm['b_x'][dr].reshape(1, wd)
        nsp = (-LRU_C * jax.nn.softplus(-prm['lam'][dr].astype(F32))).reshape(1, wd)
        hc, hfin = _lru_scan(xc_c, wa, wx, ba, bx, nsp, zero, rev=dr == 1, emit=ctx_out)
        hl, _ = _lru_scan(xc_l, wa, wx, ba, bx, nsp, hfin, rev=dr == 1)
        hs_l.append(hl)
        hs_c.append(hc)
    w_out = prm['w_out'].astype(BF16)

    def out(xa, yg, hs, mt):
        tm = _row_tile(xa.shape[0])
        specs = [_row_spec(tm, wd)] * 3
        return _outproj(_lru_out_prologue, [yg, hs[0], hs[1]], specs, wd, xa, mt, 10 ** 9, w_out, tm=tm, name="lru_out")

    return out(x, yg_l, hs_l, modt), (out(x_ctx, yg_c, hs_c, modt_ctx) if ctx_out else None)


def _rope_tables(length):
    rows = length // GRID_W
    t = jnp.arange(length, dtype=jnp.int32)
    row = (t // GRID_W).astype(F32) - (rows - 1) / 2.0
    col = (t % GRID_W).astype(F32) - (GRID_W - 1) / 2.0
    quarter = RET_DK // 4
    inv_freq = ROPE_BASE ** (-jnp.arange(quarter, dtype=F32) / quarter)
    ar, ac = row[:, None] * inv_freq[None, :], col[:, None] * inv_freq[None, :]
    cos = jnp.concatenate([jnp.cos(ar), jnp.cos(ar), jnp.cos(ac), jnp.cos(ac)], axis=-1)
    sin = jnp.concatenate([-jnp.sin(ar), jnp.sin(ar), -jnp.sin(ac), jnp.sin(ac)], axis=-1)
    return cos, sin


def _ret_in_kernel(x_ref, g_ref, sh_ref, sc_ref, w_ref, cos_ref, sin_ref, p_ref, h_ref, *, rotate, n_k, n_v):
    j = pl.program_id(1)
    tn = w_ref.shape[1]

    @pl.when(j == 0)
    def _():
        h_ref[...] = _norm_mod(x_ref[...], g_ref[...], sc_ref[...], sh_ref[...]).astype(BF16)

    p = _dot(h_ref[...], w_ref[...])
    is_k = j < n_k
    is_q = (j >= n_k + n_v) & (j < 2 * n_k + n_v)

    @pl.when(is_k | is_q)
    def _():
        scale = jnp.where(is_q, RET_DK ** -0.5, 1.0)
        if rotate:
            for s in range(tn // LANES):
                cols = slice(s * LANES, (s + 1) * LANES)
                tcols = slice((s * LANES) % RET_DK, (s * LANES) % RET_DK + LANES)
                ps = p[:, cols]
                rot = pltpu.roll(ps, LANES // 2, axis=1)
                p_ref[:, cols] = (ps * cos_ref[:, tcols] + rot * sin_ref[:, tcols]) * scale
        else:
            p_ref[...] = p * scale

    @pl.when(jnp.logical_not(is_k | is_q))
    def _():
        p_ref[...] = p


def _ret_in(x, modt, tiles_per_mod, norm_g, w_in, seq_rows, *, rotate):
    n_rows, d = x.shape
    n_out = w_in.shape[1]
    tm = _row_tile(seq_rows)
    tn = 4 * LANES
    tiles_per_seq = seq_rows // tm
    cos, sin = _rope_tables(seq_rows)
    hk, hv = RET_HEADS * RET_DK, RET_HEADS * RET_DV
    tab_spec = pl.BlockSpec((tm, RET_DK), lambda i, j: (i % tiles_per_seq, 0))
    return pl.pallas_call(
        functools.partial(_ret_in_kernel, rotate=rotate, n_k=hk // tn, n_v=hv // tn),
        grid=(n_rows // tm, n_out // tn),
        in_specs=[_row_spec(tm, d), _const_spec((1, d)), _mod_spec(0, d, tiles_per_mod), _mod_spec(1, d, tiles_per_mod),
                  pl.BlockSpec((d, tn), lambda i, j: (0, j)), tab_spec, tab_spec],
        out_specs=pl.BlockSpec((tm, tn), lambda i, j: (i, j)),
        out_shape=jax.ShapeDtypeStruct((n_rows, n_out), F32),
        scratch_shapes=[pltpu.VMEM((tm, d), BF16)],
        compiler_params=_cparams("parallel", "arbitrary"),
        name="ret_in",
    )(x, norm_g.reshape(1, d), modt, modt, w_in, cos, sin)


def _ret_tables(c):
    log_g = np.log1p(-np.power(2.0, -5.0 - np.arange(RET_HEADS, dtype=np.float64)))
    idx = np.arange(c, dtype=np.float64)
    diff = idx[:, None] - idx[None, :]
    fwd = np.where(diff >= 0, np.exp(np.where(diff >= 0, diff, 0.0)[None] * log_g[:, None, None]), 0.0)
    bwd = np.where(diff < 0, np.exp(np.where(diff < 0, -diff, 0.0)[None] * log_g[:, None, None]), 0.0)
    xi_f = np.exp((idx + 1.0)[None, :] * log_g[:, None])
    zeta_f = np.exp((c - 1.0 - idx)[None, :] * log_g[:, None])
    xi_b = np.exp((c - idx)[None, :] * log_g[:, None])
    zeta_b = np.exp(idx[None, :] * log_g[:, None])
    dmask = np.stack([fwd, bwd]).astype(np.float32)
    xi = np.stack([xi_f, xi_b])[..., None].astype(np.float32)
    zeta = np.stack([zeta_f, zeta_b])[..., None].astype(np.float32)
    g_blk = [float(v) for v in np.exp(c * log_g).astype(np.float32)]
    return jnp.asarray(dmask), jnp.asarray(xi), jnp.asarray(zeta), g_blk


def _ret_scan_kernel(k_ref, v0_ref, v1_ref, q_ref, dm_ref, xi_ref, zt_ref, r0_ref, o_ref, rf_ref, r_ref, *, g_blk, emit):
    c = pl.program_id(2)

    @pl.when(c == 0)
    def _():
        r_ref[...] = r0_ref[...]
        if not emit:
            o_ref[...] = jnp.zeros_like(o_ref)

    hv_half = v0_ref.shape[1] // RET_DV
    for h in range(RET_HEADS):
        kh = k_ref[:, h * RET_DK:(h + 1) * RET_DK]
        v_ref = v0_ref if h < hv_half else v1_ref
        hh = h % hv_half
        vh = v_ref[:, hh * RET_DV:(hh + 1) * RET_DV].astype(BF16)
        r_old = r_ref[h]
        if emit:
            qh = q_ref[:, h * RET_DK:(h + 1) * RET_DK].astype(BF16)
            s = _dot_nt(qh, kh.astype(BF16)) * dm_ref[h]
            o = _dot(s.astype(BF16), vh) + _dot(qh, r_old.astype(BF16)) * xi_ref[h]
            o_ref[:, h * RET_DV:(h + 1) * RET_DV] = o
        r_ref[h] = g_blk[h] * r_old + _dot_tn((kh * zt_ref[h]).astype(BF16), vh)

    @pl.when(c == pl.num_programs(2) - 1)
    def _():
        rf_ref[...] = r_ref[...]


def _ret_scan(p, r0, batch, seq_rows, *, emit=True):
    c = min(RET_CHUNK, seq_rows)
    nc = seq_rows // c
    hk, hv = RET_HEADS * RET_DK, RET_HEADS * RET_DV
    dmask, xi, zeta, g_blk = _ret_tables(c)
    n_rows = p.shape[0]

    def rows(d, b, cc):
        return b * nc + jnp.where(d == 0, cc, nc - 1 - cc)

    kcol = lambda blk: pl.BlockSpec((c, hk), lambda d, b, cc: (rows(d, b, cc), blk))
    tab = lambda a: pl.BlockSpec((None,) + a.shape[1:], lambda d, b, cc: (d,) + (0,) * (a.ndim - 1))
    st_spec = pl.BlockSpec((None, None, RET_HEADS, RET_DK, RET_DV), lambda d, b, cc: (d, b, 0, 0, 0))
    assert hv == 2 * hk
    o, rf = pl.pallas_call(
        functools.partial(_ret_scan_kernel, g_blk=g_blk, emit=emit),
        grid=(2, batch, nc),
        in_specs=[kcol(0), kcol(1), kcol(2), kcol(3), tab(dmask), tab(xi), tab(zeta), st_spec],
        out_specs=[pl.BlockSpec((None, c, hv), (lambda d, b, cc: (d, rows(d, b, cc), 0)) if emit else
                                (lambda d, b, cc: (0, 0, 0))), st_spec],
        out_shape=[jax.ShapeDtypeStruct((2, n_rows if emit else c, hv), F32), jax.ShapeDtypeStruct(r0.shape, F32)],
        scratch_shapes=[pltpu.VMEM((RET_HEADS, RET_DK, RET_DV), F32)],
        compiler_params=_cparams("arbitrary", "arbitrary", "arbitrary"),
        name="ret_scan",
    )(p, p, p, p, dmask, xi, zeta, r0)
    return o, rf


def _ret_out_prologue(of_ref, ob_ref, gate_ref, ng_ref):
    o = of_ref[...] + ob_ref[...]
    parts = []
    for h in range(RET_HEADS):
        oh = o[:, h * RET_DV:(h + 1) * RET_DV]
        parts.append(oh * lax.rsqrt(jnp.mean(oh * oh, axis=-1, keepdims=True) + NORM_EPS))
    on = (jnp.concatenate(parts, axis=-1) * ng_ref[...])
    return jax.nn.silu(gate_ref[...]) * on


def _ret_mixer(x, x_ctx, modt, modt_ctx, norm_g, prm, batch, seq, seq_ctx, *, ctx_out):
    d = x.shape[-1]
    w_in = prm['w_in'].astype(BF16)
    hk, hv = RET_HEADS * RET_DK, RET_HEADS * RET_DV
    tiles_per_mod = seq // _row_tile(seq)
    p_l = _ret_in(x, modt, tiles_per_mod, norm_g, w_in, seq, rotate=True)
    p_c = _ret_in(x_ctx, modt_ctx, 10 ** 9, norm_g, w_in, seq_ctx, rotate=False)
    r0 = jnp.zeros((2, batch, RET_HEADS, RET_DK, RET_DV), F32)
    o_c, r_c = _ret_scan(p_c, r0, batch, seq_ctx, emit=ctx_out)
    o_l, _ = _ret_scan(p_l, r_c, batch, seq)
    w_out = prm['w_out'].astype(BF16)
    ng = prm['norm_g'].reshape(1, hv)

    def out(xa, pa, oa, mt, tpm, seq_rows):
        tm = _row_tile(seq_rows, cap=ROW_TILE_CAP // 4)
        gcol = (2 * hk + hv) // hv
        specs = [pl.BlockSpec((None, tm, hv), lambda i, j: (0, i, 0)), pl.BlockSpec((None, tm, hv), lambda i, j: (1, i, 0)),
                 pl.BlockSpec((tm, hv), lambda i, j: (i, gcol)), _const_spec((1, hv))]
        return _outproj(_ret_out_prologue, [oa, oa, pa, ng], specs, hv, xa, mt, tpm * (_row_tile(seq_rows) // tm), w_out,
                        tm=tm, name="ret_out")

    y_l = out(x, p_l, o_l, modt, tiles_per_mod, seq)
    y_c = out(x_ctx, p_c, o_c, modt_ctx, 10 ** 9, seq_ctx) if ctx_out else None
    return y_l, y_c


GDN_IN_TILE = 4 * LANES


def _gdn_in_kernel(x_ref, xb_ref, xa_ref, g_ref, sh_ref, sc_ref, w_ref, cw_ref, o_ref, h_ref, *, tiles_per_seq, n_kq, n_conv):
    i, j = pl.program_id(0), pl.program_id(1)
    tm = x_ref.shape[0]
    hb = ha = SUBLANES
    heads = o_ref.shape[0]

    @pl.when(j == 0)
    def _():
        first = i % tiles_per_seq == 0
        last = i % tiles_per_seq == tiles_per_seq - 1
        _fill_h(h_ref, x_ref, xb_ref, xa_ref, g_ref, sc_ref, sh_ref, first, last, hb, ha)

    @pl.when(j < n_conv)
    def _():
        p = _dot(h_ref[...], w_ref[...])
        a = jax.nn.silu(_conv_rows(p, cw_ref[...], tm, 1, hb))

        @pl.when(j < n_kq)
        def _():
            scale = jnp.where(j >= n_kq // 2, GDN_DK ** -0.5, 1.0)
            for s in range(heads):
                ah = a[:, s * LANES:(s + 1) * LANES]
                o_ref[s] = ah * (lax.rsqrt(jnp.sum(ah * ah, axis=-1, keepdims=True) + 1e-6) * scale)

        @pl.when(j >= n_kq)
        def _():
            for s in range(heads):
                o_ref[s] = a[:, s * LANES:(s + 1) * LANES]

    @pl.when(j >= n_conv)
    def _():
        p = _dot(h_ref[hb:hb + tm, :], w_ref[...])
        for s in range(heads):
            o_ref[s] = p[:, s * LANES:(s + 1) * LANES]


def _gdn_in(x, modt, tiles_per_mod, norm_g, w_all, cw_all, seq_rows):
    n_rows, d = x.shape
    n_out = w_all.shape[1]
    tm = _row_tile(seq_rows)
    tn = GDN_IN_TILE
    hb = ha = SUBLANES
    heads = tn // LANES
    n_kq = 2 * GDN_K_HEADS * GDN_DK // tn
    n_conv = n_kq + GDN_V_HEADS * GDN_DV // tn
    kern = functools.partial(_gdn_in_kernel, tiles_per_seq=seq_rows // tm, n_kq=n_kq, n_conv=n_conv)
    specs = _halo_specs(tm, hb, ha, d, n_rows) + [
        _const_spec((1, d)), _mod_spec(0, d, tiles_per_mod), _mod_spec(1, d, tiles_per_mod),
        pl.BlockSpec((d, tn), lambda i, j: (0, j)), pl.BlockSpec((cw_all.shape[0], tn), lambda i, j: (0, j))]
    return pl.pallas_call(
        kern,
        grid=(n_rows // tm, n_out // tn),
        in_specs=specs,
        out_specs=pl.BlockSpec((heads, tm, LANES), lambda i, j: (j, i, 0)),
        out_shape=jax.ShapeDtypeStruct((n_out // LANES, n_rows, LANES), F32),
        scratch_shapes=[pltpu.VMEM((tm + hb + ha, d), BF16)],
        compiler_params=_cparams("parallel", "arbitrary"),
        name="gdn_in",
    )(x, x, x, norm_g.reshape(1, d), modt, modt, w_all, cw_all)


def _split3(a):
    hi = a.astype(BF16)
    lo = (a - hi.astype(F32)).astype(BF16)
    return hi, lo


def _dot3s(a, b):
    ah, al = a
    bh, bl = b
    return _dot(ah, bh) + (_dot(ah, bl) + _dot(al, bh))


def _cumsum_rows(x, rev):
    n = x.shape[0]
    row = lax.broadcasted_iota(jnp.int32, x.shape, 0)
    shift = 1
    while shift < n:
        if rev:
            x = x + jnp.where(row < n - shift, pltpu.roll(x, n - shift, axis=0), 0.0)
        else:
            x = x + jnp.where(row >= shift, pltpu.roll(x, shift, axis=0), 0.0)
        shift *= 2
    return x


def _gdn_scan_kernel(k_ref, q_ref, v_ref, ba_ref, an_ref, dtb_ref, s0_ref, o_ref, sf_ref, s_ref, *, rev, emit, dr):
    cidx = pl.program_id(1)
    c = k_ref.shape[1]
    rep = GDN_V_HEADS // GDN_K_HEADS

    @pl.when(cidx == 0)
    def _():
        s_ref[...] = s0_ref[...]
        if not emit:
            o_ref[...] = jnp.zeros_like(o_ref)

    ba = ba_ref[...]
    sp = jnp.maximum(ba + dtb_ref[...], 0.0) + jnp.log(1.0 + jnp.exp(-jnp.abs(ba + dtb_ref[...])))
    gc = _cumsum_rows(an_ref[...] * sp, rev)
    gct = jnp.concatenate([gc, gc], axis=0).T
    bt = jax.nn.sigmoid(ba)

    ii = lax.broadcasted_iota(jnp.int32, (c, c), 0)
    jj = lax.broadcasted_iota(jnp.int32, (c, c), 1)
    diff = (jj - ii) if rev else (ii - jj)
    incl, strict = diff >= 0, diff > 0
    eye = (diff == 0).astype(F32)
    last = 0 if rev else c - 1
    nk, nv = GDN_K_HEADS, GDN_V_HEADS

    ks = [k_ref[h] for h in range(nk)]
    k16 = [k.astype(BF16) for k in ks]
    if emit:
        qs = [q_ref[h] for h in range(nk)]
        gram = [_dot_nt(jnp.concatenate([ks[h], qs[h]], axis=0).astype(BF16), k16[h]) for h in range(nk)]
    else:
        gram = [_dot_nt(k16[h], k16[h]) for h in range(nk)]

    gcol, bcol, grow, attn, x = [], [], [], [], []
    for hv in range(nv):
        ig = dr * 2 * nv + nv + hv
        ib = dr * 2 * nv + hv
        gcol.append(gc[:, ig:ig + 1])
        bcol.append(bt[:, ib:ib + 1])
        grow.append(gct[ig:ig + 1, 0:c])
        dec = jnp.where(incl, jnp.exp(jnp.where(incl, gcol[hv] - grow[hv], 0.0)), 0.0)
        kk = gram[hv // rep]
        x.append(jnp.where(strict, -(kk[:c] * bcol[hv]) * dec, 0.0))
        attn.append(kk[c:] * dec if emit else None)

    xs = [_split3(xi) for xi in x]
    t = [eye + xi for xi in x]
    for _ in range(int(math.log2(c)) - 1):
        x = [_dot3s(s, s) for s in xs]
        xs = [_split3(xi) for xi in x]
        t = [ti + _dot3s(s, _split3(ti)) for s, ti in zip(xs, t)]

    for hv in range(nv):
        kh = hv // rep
        eg = jnp.exp(gcol[hv])
        kb = ks[kh] * bcol[hv]
        uw = _dot3s(_split3(t[hv]), _split3(jnp.concatenate([v_ref[hv] * bcol[hv], kb * eg], axis=1)))
        s_old = s_ref[hv]
        s16 = s_old.astype(BF16)
        glast = grow[hv][:, last:last + 1]
        if emit:
            ws = _dot(jnp.concatenate([uw[:, GDN_DV:], qs[kh] * eg], axis=0).astype(BF16), s16)
            vn16 = (uw[:, :GDN_DV] - ws[:c]).astype(BF16)
            o_ref[hv] = ws[c:] + _dot(attn[hv].astype(BF16), vn16)
        else:
            vn16 = (uw[:, :GDN_DV] - _dot(uw[:, GDN_DV:].astype(BF16), s16)).astype(BF16)
        kd = (ks[kh] * jnp.exp(glast - gcol[hv])).astype(BF16)
        s_ref[hv] = s_old * jnp.exp(glast) + _dot_tn(kd, vn16)

    @pl.when(cidx == pl.num_programs(1) - 1)
    def _():
        sf_ref[...] = s_ref[...]


def _gdn_scan(kqvz, an, dtb, s0, batch, seq_rows, *, dr, emit=True):
    c = min(GDN_CHUNK, seq_rows)
    nc = seq_rows // c
    n_rows = kqvz.shape[1]
    rev = dr == 1
    rows = (lambda b, cc: b * nc + nc - 1 - cc) if rev else (lambda b, cc: b * nc + cc)
    kh, vh = GDN_K_HEADS, GDN_V_HEADS
    ba_head = 2 * kh + 2 * vh
    st_spec = pl.BlockSpec((None, vh, GDN_DK, GDN_DV), lambda b, cc: (b, 0, 0, 0))
    vec = pl.BlockSpec((1, LANES), lambda b, cc: (0, 0))
    o, sf = pl.pallas_call(
        functools.partial(_gdn_scan_kernel, rev=rev, emit=emit, dr=dr),
        grid=(batch, nc),
        in_specs=[pl.BlockSpec((kh, c, LANES), lambda b, cc: (0, rows(b, cc), 0)),
                  pl.BlockSpec((kh, c, LANES), lambda b, cc: (1, rows(b, cc), 0)),
                  pl.BlockSpec((vh, c, LANES), lambda b, cc: (1, rows(b, cc), 0)),
                  pl.BlockSpec((None, c, LANES), lambda b, cc: (ba_head, rows(b, cc), 0)),
                  vec, vec, st_spec],
        out_specs=[pl.BlockSpec((vh, c, LANES), (lambda b, cc: (0, rows(b, cc), 0)) if emit else (lambda b, cc: (0, 0, 0))),
                   st_spec],
        out_shape=[jax.ShapeDtypeStruct((vh, n_rows if emit else c, LANES), F32), jax.ShapeDtypeStruct(s0.shape, F32)],
        scratch_shapes=[pltpu.VMEM((vh, GDN_DK, GDN_DV), F32)],
        compiler_params=_cparams("arbitrary", "arbitrary"),
        name="gdn_scan_bwd" if rev else "gdn_scan_fwd",
    )(kqvz, kqvz, kqvz, kqvz, an, dtb, s0)
    return o, sf


def _gdn_out_prologue(of_ref, ob_ref, z_ref, ng_ref):
    parts = []
    for h in range(GDN_V_HEADS):
        o = of_ref[h] + ob_ref[h]
        on = o * lax.rsqrt(jnp.mean(o * o, axis=-1, keepdims=True) + NORM_EPS) * ng_ref[...]
        parts.append(on * jax.nn.silu(z_ref[h]))
    return jnp.concatenate(parts, axis=-1)


def _gdn_mixer(x, x_ctx, modt, modt_ctx, norm_g, prm, batch, seq, seq_ctx, *, ctx_out):
    d = x.shape[-1]
    qk, vv = GDN_K_HEADS * GDN_DK, GDN_V_HEADS * GDN_DV
    ng2 = 2 * 2 * GDN_V_HEADS
    w_in, conv_w = prm['w_in'], prm['conv_w']
    pad = GDN_IN_TILE - ng2
    w_all = jnp.concatenate([w_in[:, :qk], w_in[:, qk + vv + ng2:2 * qk + vv + ng2], w_in[:, qk:qk + vv],
                             w_in[:, 2 * qk + vv + ng2:], w_in[:, qk + vv:qk + vv + ng2],
                             jnp.zeros((d, pad), w_in.dtype)], axis=1).astype(BF16)
    cw_all = jnp.concatenate([conv_w[:, :qk], conv_w[:, qk + vv:], conv_w[:, qk:qk + vv],
                              jnp.zeros((conv_w.shape[0], vv + GDN_IN_TILE), conv_w.dtype)], axis=1)
    tiles_per_mod = seq // _row_tile(seq)
    p_l = _gdn_in(x, modt, tiles_per_mod, norm_g, w_all, cw_all, seq)
    p_c = _gdn_in(x_ctx, modt_ctx, 10 ** 9, norm_g, w_all, cw_all, seq_ctx)
    s0 = jnp.zeros((batch, GDN_V_HEADS, GDN_DK, GDN_DV), F32)
    os_l, os_c = [], []
    for dr in range(2):
        an = jnp.zeros((LANES,), F32).at[dr * 2 * GDN_V_HEADS + GDN_V_HEADS:(dr + 1) * 2 * GDN_V_HEADS].set(
            -jnp.exp(prm['a_log'][dr].astype(F32))).reshape(1, LANES)
        dtb = jnp.zeros((LANES,), F32).at[dr * 2 * GDN_V_HEADS + GDN_V_HEADS:(dr + 1) * 2 * GDN_V_HEADS].set(
            prm['dt_bias'][dr].astype(F32)).reshape(1, LANES)
        oc, sc = _gdn_scan(p_c, an, dtb, s0, batch, seq_ctx, dr=dr, emit=ctx_out)
        ol, _ = _gdn_scan(p_l, an, dtb, sc, batch, seq, dr=dr)
        os_l.append(ol)
        os_c.append(oc)
    w_out = prm['w_out'].astype(BF16)
    ng = prm['norm_g'].reshape(1, GDN_DV)
    vh = GDN_V_HEADS
    z_blk = (2 * GDN_K_HEADS + vh) // vh

    def out(xa, pa, oa, mt, tpm, seq_rows):
        tm = _row_tile(seq_rows, cap=ROW_TILE_CAP // 4)
        hspec = lambda blk: pl.BlockSpec((vh, tm, LANES), lambda i, j: (blk, i, 0))
        specs = [hspec(0), hspec(0), hspec(z_blk), _const_spec((1, GDN_DV))]
        return _outproj(_gdn_out_prologue, [oa[0], oa[1], pa, ng], specs, vv, xa, mt, tpm * (_row_tile(seq_rows) // tm),
                        w_out, tm=tm, name="gdn_out")

    y_l = out(x, p_l, os_l, modt, tiles_per_mod, seq)
    y_c = out(x_ctx, p_c, os_c, modt_ctx, 10 ** 9, seq_ctx) if ctx_out else None
    return y_l, y_c


def kernel(x, c, ctx, c_ctx, norm1_g, norm2_g, mod_w, mod_b, ffn_w_in, ffn_conv_w, ffn_conv_b, ffn_w_out, s5_lam_re, s5_lam_im, s5_log_step, s5_b_re, s5_b_im, s5_c_re, s5_c_im, s5_d, s5_w_glu, s5_b_glu, lru_w_in, lru_conv_w, lru_conv_b, lru_w_a, lru_b_a, lru_w_x, lru_b_x, lru_lam, lru_w_out, ret_w_in, ret_norm_g, ret_w_out, gdn_w_in, gdn_conv_w, gdn_a_log, gdn_dt_bias, gdn_norm_g, gdn_w_out, final_norm_g):
    B, L, D = x.shape
    Lc = ctx.shape[1]
    depth = mod_w.shape[0]
    assert B == SUBLANES and depth == 4, "time-major layers put the batch on the 8 sublanes; four mixer kinds"
    c16 = jnp.concatenate([c, c_ctx[None], jnp.zeros((16 - B - 1, D), F32)], 0)
    mods = _modulation(c16, mod_w, mod_b)

    def ffn(i, xa, mt, tiles_per_mod, rs, seq_rows, final_norm=False):
        return _ffn(xa, mt, tiles_per_mod, norm2_g[i], ffn_w_in[i].astype(BF16), ffn_conv_w[i], ffn_conv_b[i],
                    ffn_w_out[i].astype(BF16), final_norm_g, rs=rs, seq_rows=seq_rows, final_norm=final_norm)

    xt = jnp.swapaxes(x, 0, 1).reshape(L * B, D)
    ct = jnp.swapaxes(ctx, 0, 1).reshape(Lc * B, D)
    s5p = dict(lam_re=s5_lam_re[0], lam_im=s5_lam_im[0], log_step=s5_log_step[0], b_re=s5_b_re[0], b_im=s5_b_im[0],
               c_re=s5_c_re[0], c_im=s5_c_im[0], d=s5_d[0], w_glu=s5_w_glu[0], b_glu=s5_b_glu[0])
    lrup = dict(w_in=lru_w_in[0], conv_w=lru_conv_w[0], conv_b=lru_conv_b[0], w_a=lru_w_a[0], b_a=lru_b_a[0],
                w_x=lru_w_x[0], b_x=lru_b_x[0], lam=lru_lam[0], w_out=lru_w_out[0])
    for i, (mixer, prm) in enumerate(((_s5_mixer, s5p), (_lru_mixer, lrup))):
        mt = mods[i, :B][None]
        mtc = jnp.broadcast_to(mods[i, B][None, None, :], (1, SUBLANES, N_MOD * D))
        xt, ct = mixer(xt, ct, mt, mtc, norm1_g[i], prm, ctx_out=True)
        xt = ffn(i, xt, mt, 10 ** 9, SUBLANES, L * B)
        ct = ffn(i, ct, mtc, 10 ** 9, SUBLANES, Lc * B)

    xb = jnp.swapaxes(xt.reshape(L, B, D), 0, 1).reshape(B * L, D)
    cb = jnp.swapaxes(ct.reshape(Lc, B, D), 0, 1).reshape(B * Lc, D)
    retp = dict(w_in=ret_w_in[0], norm_g=ret_norm_g[0], w_out=ret_w_out[0])
    gdnp = dict(w_in=gdn_w_in[0], conv_w=gdn_conv_w[0], a_log=gdn_a_log[0], dt_bias=gdn_dt_bias[0],
                norm_g=gdn_norm_g[0], w_out=gdn_w_out[0])
    tiles_per_batch = L // _row_tile(L)
    for i, (mixer, prm) in ((2, (_ret_mixer, retp)), (3, (_gdn_mixer, gdnp))):
        ctx_out = i < depth - 1
        mt = jnp.broadcast_to(mods[i, :B][:, None, :], (B, SUBLANES, N_MOD * D))
        mtc = jnp.broadcast_to(mods[i, B][None, None, :], (1, SUBLANES, N_MOD * D))
        xb, cb_new = mixer(xb, cb, mt, mtc, norm1_g[i], prm, B, L, Lc, ctx_out=ctx_out)
        xb = ffn(i, xb, mt, tiles_per_batch, 1, L, final_norm=not ctx_out)
        if ctx_out:
            cb = ffn(i, cb_new, mtc, 10 ** 9, 1, Lc)
    return xb.reshape(B, L, D)
```

```python
import functools
import math

import numpy as np
import jax
import jax.numpy as jnp
from jax import lax
from jax.experimental import pallas as pl
from jax.experimental.pallas import tpu as pltpu

F32 = jnp.float32
BF16 = jnp.bfloat16

NORM_EPS = 1e-6
SUBLANES = 8
LANES = 128
VMEM_LIMIT_BYTES = 56 * 1024 * 1024
ROW_TILE_CAP = 1024

S5_GROUP = 16
S5_STATE = 64
S5_GB = 16
LRU_BLOCKS = 4
LRU_C = 8.0
RET_HEADS = 4
RET_DK = 256
RET_DV = 512
RET_CHUNK = 128
ROPE_BASE = 10000.0
GRID_W = 64
GDN_K_HEADS = 8
GDN_V_HEADS = 16
GDN_DK = 128
GDN_DV = 128
GDN_CHUNK = 64
N_MOD = 6


def _cparams(*sem):
    return pltpu.CompilerParams(dimension_semantics=sem, vmem_limit_bytes=VMEM_LIMIT_BYTES)


def _row_tile(rows, cap=None):
    cap = ROW_TILE_CAP if cap is None else cap
    tm = min(cap, rows)
    assert rows % tm == 0 and tm % SUBLANES == 0, (rows, tm)
    return tm


def _dot(a, b):
    return jnp.dot(a, b, preferred_element_type=F32)


def _dot_tn(a, b):
    return lax.dot_general(a, b, (((0,), (0,)), ((), ())), preferred_element_type=F32)


def _dot_nt(a, b):
    return lax.dot_general(a, b, (((1,), (1,)), ((), ())), preferred_element_type=F32)


def _gelu(x):
    return jax.nn.gelu(x, approximate=True)


def _sigmoid(x):
    return 1.0 / (1.0 + jnp.exp(-x))


def _silu(x):
    return x * _sigmoid(x)


def _tile8(v, rows):
    n = v.shape[-1]
    return jnp.broadcast_to(v[None], (rows // SUBLANES, SUBLANES, n)).reshape(rows, n)


def _norm_mod(x, g, sc, sh):
    ms = jnp.mean(x * x, axis=-1, keepdims=True)
    y = x * lax.rsqrt(ms + NORM_EPS) * g
    rows = x.shape[0]
    return y * (1.0 + _tile8(sc, rows)) + _tile8(sh, rows)


def _shift_rows(p, off, rows):
    if off % SUBLANES == 0:
        return p[off:off + rows]
    n = p.shape[0]
    base = (off // SUBLANES) * SUBLANES
    rolled = pltpu.roll(p, (n - (off - base)) % n, axis=0)
    return rolled[base:base + rows]


def _conv_rows(p_ext, cw, rows, rs, hb):
    k_taps = cw.shape[0]
    acc = None
    for k in range(k_taps):
        term = cw[k:k + 1, :] * _shift_rows(p_ext, hb + (k - k_taps // 2) * rs, rows)
        acc = term if acc is None else acc + term
    return acc


def _fill_h(h_ref, x_ref, xb_ref, xa_ref, g_ref, sc_ref, sh_ref, first, last, hb, ha):
    tm = x_ref.shape[0]
    g, sc, sh = g_ref[...], sc_ref[...], sh_ref[...]
    h_ref[hb:hb + tm, :] = _norm_mod(x_ref[...], g, sc, sh).astype(BF16)
    if hb:
        hbv = _norm_mod(xb_ref[...], g, sc, sh)
        h_ref[0:hb, :] = jnp.where(first, 0.0, hbv).astype(BF16)
    if ha:
        hav = _norm_mod(xa_ref[...], g, sc, sh)
        h_ref[hb + tm:hb + tm + ha, :] = jnp.where(last, 0.0, hav).astype(BF16)


def _halo_specs(tm, hb, ha, d, n_rows):
    specs = [pl.BlockSpec((tm, d), lambda i, j: (i, 0))]
    if hb:
        specs.append(pl.BlockSpec((hb, d), lambda i, j: (jnp.maximum(i * (tm // hb) - 1, 0), 0)))
    if ha:
        nblk = n_rows // ha
        specs.append(pl.BlockSpec((ha, d), lambda i, j: (jnp.minimum((i + 1) * (tm // ha), nblk - 1), 0)))
    return specs


def _mod_spec(chunk, d, tiles_per_mod):
    return pl.BlockSpec((None, SUBLANES, d), lambda i, j: (i // tiles_per_mod, 0, chunk))


def _mod_kernel(c_ref, w_ref, b_ref, o_ref):
    a = _silu(c_ref[...]).astype(BF16)
    o_ref[...] = _dot(a, w_ref[...].astype(BF16)) + b_ref[...]


def _modulation(c16, mod_w, mod_b):
    depth, d, n = mod_w.shape
    tn = n // 4
    return pl.pallas_call(
        _mod_kernel,
        grid=(depth, n // tn),
        in_specs=[pl.BlockSpec((16, d), lambda l, j: (0, 0)),
                  pl.BlockSpec((None, d, tn), lambda l, j: (l, 0, j)),
                  pl.BlockSpec((None, 1, tn), lambda l, j: (l, 0, j))],
        out_specs=pl.BlockSpec((None, 16, tn), lambda l, j: (l, 0, j)),
        out_shape=jax.ShapeDtypeStruct((depth, 16, n), F32),
        compiler_params=_cparams("arbitrary", "arbitrary"),
        name="modulation",
    )(c16, mod_w, mod_b.reshape(depth, 1, n))


def _ffn_kernel(x_ref, xb_ref, xa_ref, g_ref, sh_ref, sc_ref, gt_ref, wg_ref, wu_ref, cw_ref, cb_ref, wo_ref,
                fg_ref, o_ref, h_ref, acc_ref, *, rs, tiles_per_seq, final_norm):
    i, j = pl.program_id(0), pl.program_id(1)
    tm = x_ref.shape[0]
    hb = ha = SUBLANES

    @pl.when(j == 0)
    def _():
        first = i % tiles_per_seq == 0
        last = i % tiles_per_seq == tiles_per_seq - 1
        _fill_h(h_ref, x_ref, xb_ref, xa_ref, g_ref, sc_ref, sh_ref, first, last, hb, ha)
        acc_ref[...] = jnp.zeros_like(acc_ref)

    gate = _dot(h_ref[...], wg_ref[...])
    up = _dot(h_ref[hb:hb + tm, :], wu_ref[...])
    gc = _conv_rows(gate, cw_ref[...], tm, rs, hb) + cb_ref[...]
    a = (_gelu(gc) * up).astype(BF16)
    acc_ref[...] += _dot(a, wo_ref[...])

    @pl.when(j == pl.num_programs(1) - 1)
    def _():
        y = x_ref[...] + _tile8(gt_ref[...], tm) * acc_ref[...]
        if final_norm:
            ms = jnp.mean(y * y, axis=-1, keepdims=True)
            y = y * lax.rsqrt(ms + NORM_EPS) * fg_ref[...]
        o_ref[...] = y


def _ffn(x, modt, tiles_per_mod, norm_g, w_in, conv_w, conv_b, w_out, final_g, *, rs, seq_rows, final_norm=False):
    n_rows, d = x.shape
    f = w_out.shape[0]
    tm = _row_tile(seq_rows)
    tf = 2 * LANES
    nj = f // tf
    assert f % tf == 0
    hb = ha = SUBLANES
    kern = functools.partial(_ffn_kernel, rs=rs, tiles_per_seq=seq_rows // tm, final_norm=final_norm)
    in_specs = _halo_specs(tm, hb, ha, d, n_rows) + [
        pl.BlockSpec((1, d), lambda i, j: (0, 0)),
        _mod_spec(3, d, tiles_per_mod), _mod_spec(4, d, tiles_per_mod), _mod_spec(5, d, tiles_per_mod),
        pl.BlockSpec((d, tf), lambda i, j: (0, j)),
        pl.BlockSpec((d, tf), lambda i, j: (0, j + nj)),
        pl.BlockSpec((conv_w.shape[0], tf), lambda i, j: (0, j)),
        pl.BlockSpec((1, tf), lambda i, j: (0, j)),
        pl.BlockSpec((tf, d), lambda i, j: (j, 0)),
        pl.BlockSpec((1, d), lambda i, j: (0, 0)),
    ]
    return pl.pallas_call(
        kern,
        grid=(n_rows // tm, nj),
        in_specs=in_specs,
        out_specs=pl.BlockSpec((tm, d), lambda i, j: (i, 0)),
        out_shape=jax.ShapeDtypeStruct((n_rows, d), F32),
        scratch_shapes=[pltpu.VMEM((tm + hb + ha, d), BF16), pltpu.VMEM((tm, d), F32)],
        compiler_params=_cparams("parallel", "arbitrary"),
        name="conv_ffn",
    )(x, x, x, norm_g.reshape(1, d), modt, modt, modt, w_in, w_in, conv_w, conv_b.reshape(1, f), w_out,
      final_g.reshape(1, d))


def _outproj_kernel(*refs, n_pro, prologue, glu):
    pro_refs = refs[:n_pro]
    if glu:
        x_ref, gt_ref, w_ref, b_ref, o_ref = refs[n_pro:]
    else:
        x_ref, gt_ref, w_ref, o_ref = refs[n_pro:]
    d = o_ref.shape[1]
    a = prologue(*pro_refs).astype(BF16)
    y = _dot(a, w_ref[:, 0:d])
    if glu:
        y = (y + b_ref[:, 0:d]) * jax.nn.sigmoid(_dot(a, w_ref[:, d:2 * d]) + b_ref[:, d:2 * d])
    o_ref[...] = x_ref[...] + _tile8(gt_ref[...], o_ref.shape[0]) * y


def _outproj(prologue, pro_args, pro_specs, k_dim, x, modt, tiles_per_mod, w, bias=None, *, tm, name):
    n_rows, d = x.shape
    glu = bias is not None
    kern = functools.partial(_outproj_kernel, n_pro=len(pro_args), prologue=prologue, glu=glu)
    specs = list(pro_specs) + [pl.BlockSpec((tm, d), lambda i, j: (i, 0)),
                               pl.BlockSpec((None, SUBLANES, d), lambda i, j: (i // tiles_per_mod, 0, 2)),
                               pl.BlockSpec(w.shape, lambda i, j: (0, 0))]
    args = list(pro_args) + [x, modt, w]
    if glu:
        specs += [pl.BlockSpec(bias.shape, lambda i, j: (0, 0))]
        args += [bias]
    return pl.pallas_call(
        kern,
        grid=(n_rows // tm, 1),
        in_specs=specs,
        out_specs=pl.BlockSpec((tm, d), lambda i, j: (i, 0)),
        out_shape=jax.ShapeDtypeStruct((n_rows, d), F32),
        compiler_params=_cparams("parallel", "arbitrary"),
        name=name,
    )(*args)


def _row_spec(tm, n):
    return pl.BlockSpec((tm, n), lambda i, j: (i, 0))


def _const_spec(shape):
    nd = len(shape)
    return pl.BlockSpec(shape, lambda i, j: (0,) * nd)


def _s5_weights(lam_re, lam_im, log_step, b_re, b_im, c_re, c_im):
    g, p = lam_re.shape
    gc = b_re.shape[-1]
    lr = jnp.minimum(lam_re.astype(F32), -1e-4)
    li = lam_im.astype(F32)
    step = jnp.exp(log_step.astype(F32))[:, None]
    mag = jnp.exp(lr * step)
    ar, ai = mag * jnp.cos(li * step), mag * jnp.sin(li * step)
    den = lr * lr + li * li
    kr = ((ar - 1.0) * lr + ai * li) / den
    ki = (ai * lr - (ar - 1.0) * li) / den
    br32, bi32 = b_re.astype(F32), b_im.astype(F32)
    br = kr[..., None] * br32 - ki[..., None] * bi32
    bi = kr[..., None] * bi32 + ki[..., None] * br32
    nb = g // S5_GB
    eye = jnp.eye(S5_GB, dtype=F32)

    def pack_b(b):
        b4 = b.reshape(nb, S5_GB, p, gc)
        return jnp.einsum('blpc,lm->blcmp', b4, eye).reshape(nb, S5_GB * gc, S5_GB * p)

    def pack_c(cm):
        c4 = cm.astype(F32).reshape(nb, S5_GB, gc, p)
        return jnp.einsum('blcp,lm->blpmc', c4, eye).reshape(nb, S5_GB * p, S5_GB * gc)

    bm = jnp.concatenate([pack_b(br), pack_b(bi)], axis=-1).astype(BF16)
    cm = jnp.concatenate([pack_c(c_re), -pack_c(c_im)], axis=1).astype(BF16)
    lam = jnp.stack([ar.reshape(nb, S5_GB * p), ai.reshape(nb, S5_GB * p)], axis=1)
    lam = jnp.broadcast_to(lam[:, :, None, :], (nb, 2, SUBLANES, S5_GB * p))
    return bm, cm, lam


def _s5_scan_kernel(x_ref, g_ref, sh_ref, sc_ref, bm_ref, cm_ref, lam_ref, s0_ref, y_ref, sf_ref, st_ref, bu_ref,
                    *, rev, emit):
    i = pl.program_id(0)
    tm = x_ref.shape[0]
    nt = tm // SUBLANES
    nb, kin, ns2 = bm_ref.shape
    ns = ns2 // 2

    @pl.when(i == 0)
    def _():
        st_ref[...] = s0_ref[...]
        if not emit:
            y_ref[...] = jnp.zeros_like(y_ref)

    h = _norm_mod(x_ref[...], g_ref[...], sc_ref[...], sh_ref[...]).astype(BF16)
    for gb in range(nb):
        bu_ref[...] = _dot(h[:, gb * kin:(gb + 1) * kin], bm_ref[gb])
        ar, ai = lam_ref[gb, 0], lam_ref[gb, 1]

        def step(t, carry):
            sr, si = carry
            tt = nt - 1 - t if rev else t
            r0 = pl.multiple_of(tt * SUBLANES, SUBLANES)
            nr = ar * sr - ai * si + bu_ref[pl.ds(r0, SUBLANES), 0:ns]
            ni = ar * si + ai * sr + bu_ref[pl.ds(r0, SUBLANES), ns:ns2]
            if emit:
                bu_ref[pl.ds(r0, SUBLANES), 0:ns] = nr
                bu_ref[pl.ds(r0, SUBLANES), ns:ns2] = ni
            return nr, ni

        sr, si = lax.fori_loop(0, nt, step, (st_ref[gb, 0], st_ref[gb, 1]), unroll=4)
        st_ref[gb, 0] = sr
        st_ref[gb, 1] = si
        if emit:
            y_ref[:, gb * kin:(gb + 1) * kin] = _dot(bu_ref[...].astype(BF16), cm_ref[gb])

    @pl.when(i == pl.num_programs(0) - 1)
    def _():
        sf_ref[...] = st_ref[...]


def _s5_scan(x, modt, norm_g, bm, cm, lam, s0, *, rev, emit=True):
    n_rows, d = x.shape
    tm = _row_tile(n_rows, cap=ROW_TILE_CAP // 2)
    ntile = n_rows // tm
    tile = (lambda i: (ntile - 1 - i, 0)) if rev else (lambda i: (i, 0))
    mod = lambda k: pl.BlockSpec((None, SUBLANES, d), lambda i: (0, 0, k))
    full = lambda a: pl.BlockSpec(a.shape, lambda i: (0,) * a.ndim)
    y, sf = pl.pallas_call(
        functools.partial(_s5_scan_kernel, rev=rev, emit=emit),
        grid=(ntile,),
        in_specs=[pl.BlockSpec((tm, d), tile), pl.BlockSpec((1, d), lambda i: (0, 0)), mod(0), mod(1),
                  full(bm), full(cm), full(lam), full(s0)],
        out_specs=[pl.BlockSpec((tm, d), tile if emit else (lambda i: (0, 0))), full(s0)],
        out_shape=[jax.ShapeDtypeStruct((n_rows if emit else tm, d), F32), jax.ShapeDtypeStruct(s0.shape, F32)],
        scratch_shapes=[pltpu.VMEM(s0.shape, F32), pltpu.VMEM((tm, bm.shape[-1]), F32)],
        compiler_params=_cparams("arbitrary"),
        name="s5_scan_bwd" if rev else "s5_scan_fwd",
    )(x, norm_g.reshape(1, d), modt, modt, bm, cm, lam, s0)
    return (y if emit else None), sf


def _s5_out_prologue(x_ref, yf_ref, yb_ref, g_ref, sh_ref, sc_ref, dk_ref):
    u = _norm_mod(x_ref[...], g_ref[...], sc_ref[...], sh_ref[...])
    return _gelu(yf_ref[...] + yb_ref[...] + dk_ref[...] * u)


def _s5_mixer(x, x_ctx, modt, modt_ctx, norm_g, prm, *, ctx_out):
    d = x.shape[-1]
    w = [_s5_weights(prm['lam_re'][dr], prm['lam_im'][dr], prm['log_step'][dr], prm['b_re'][dr], prm['b_im'][dr],
                     prm['c_re'][dr], prm['c_im'][dr]) for dr in range(2)]
    zero = jnp.zeros(w[0][2].shape, F32)
    ys, ys_ctx = [], []
    for dr in range(2):
        bm, cm, lam = w[dr]
        yc, sc = _s5_scan(x_ctx, modt_ctx, norm_g, bm, cm, lam, zero, rev=dr == 1, emit=ctx_out)
        yl, _ = _s5_scan(x, modt, norm_g, bm, cm, lam, sc, rev=dr == 1)
        ys.append(yl)
        ys_ctx.append(yc)
    w_glu = prm['w_glu'].astype(BF16)
    b_glu = prm['b_glu'].reshape(1, -1)

    def out(xa, ya, mt):
        tm = _row_tile(xa.shape[0], cap=ROW_TILE_CAP // 2)
        pro_args = [xa, ya[0], ya[1], norm_g.reshape(1, d), mt, mt, prm['d'].reshape(1, d)]
        pro_specs = [_row_spec(tm, d), _row_spec(tm, d), _row_spec(tm, d), _const_spec((1, d)),
                     _mod_spec(0, d, 10 ** 9), _mod_spec(1, d, 10 ** 9), _const_spec((1, d))]
        return _outproj(_s5_out_prologue, pro_args, pro_specs, d, xa, mt, 10 ** 9, w_glu, b_glu, tm=tm, name="s5_out")

    return out(x, ys, modt), (out(x_ctx, ys_ctx, modt_ctx) if ctx_out else None)


def _lru_in_kernel(x_ref, xb_ref, xa_ref, g_ref, sh_ref, sc_ref, wr_ref, wy_ref, cw_ref, cb_ref, xc_ref, yg_ref, h_ref,
                   *, rs, hb, ha):
    i, j = pl.program_id(0), pl.program_id(1)
    tm = x_ref.shape[0]

    @pl.when(j == 0)
    def _():
        _fill_h(h_ref, x_ref, xb_ref, xa_ref, g_ref, sc_ref, sh_ref, i == 0, i == pl.num_programs(0) - 1, hb, ha)

    p = _dot(h_ref[...], wr_ref[...])
    xc_ref[...] = _conv_rows(p, cw_ref[...], tm, rs, hb) + cb_ref[...]
    yg_ref[...] = _gelu(_dot(h_ref[hb:hb + tm, :], wy_ref[...]))


def _lru_in(x, modt, norm_g, w_in, conv_w, conv_b):
    n_rows, d = x.shape
    wd = conv_w.shape[-1]
    rs = SUBLANES
    hb, ha = 2 * rs, rs
    tm = _row_tile(n_rows)
    tn = 4 * LANES
    nj = wd // tn
    specs = _halo_specs(tm, hb, ha, d, n_rows) + [
        _const_spec((1, d)), _mod_spec(0, d, 10 ** 9), _mod_spec(1, d, 10 ** 9),
        pl.BlockSpec((d, tn), lambda i, j: (0, j)), pl.BlockSpec((d, tn), lambda i, j: (0, j + nj)),
        pl.BlockSpec((conv_w.shape[0], tn), lambda i, j: (0, j)), pl.BlockSpec((1, tn), lambda i, j: (0, j))]
    out_spec = pl.BlockSpec((tm, tn), lambda i, j: (i, j))
    return pl.pallas_call(
        functools.partial(_lru_in_kernel, rs=rs, hb=hb, ha=ha),
        grid=(n_rows // tm, nj),
        in_specs=specs,
        out_specs=[out_spec, out_spec],
        out_shape=[jax.ShapeDtypeStruct((n_rows, wd), F32)] * 2,
        scratch_shapes=[pltpu.VMEM((tm + hb + ha, d), BF16)],
        compiler_params=_cparams("parallel", "arbitrary"),
        name="lru_in",
    )(x, x, x, norm_g.reshape(1, d), modt, modt, w_in, w_in, conv_w, conv_b.reshape(1, wd))


def _lru_scan_kernel(xc_ref, wa_ref, wx_ref, ba_ref, bx_ref, nsp_ref, h0_ref, hs_ref, hf_ref, st_ref, a_ref, b_ref,
                     *, rev, emit):
    i = pl.program_id(0)
    tm = xc_ref.shape[0]
    nt = tm // SUBLANES
    nblk, bw, _ = wa_ref.shape

    @pl.when(i == 0)
    def _():
        st_ref[...] = h0_ref[...]
        if not emit:
            hs_ref[...] = jnp.zeros_like(hs_ref)

    for k in range(nblk):
        cols = slice(k * bw, (k + 1) * bw)
        xc = xc_ref[:, cols]
        xcb = xc.astype(BF16)
        r = jax.nn.sigmoid(_dot(xcb, wa_ref[k]) + ba_ref[:, cols])
        gi = jax.nn.sigmoid(_dot(xcb, wx_ref[k]) + bx_ref[:, cols])
        log_a = nsp_ref[:, cols] * r
        a = jnp.exp(log_a)
        a_ref[:, cols] = a
        b_ref[:, cols] = jnp.sqrt(1.0 - a * a) * (gi * xc)

    def step(t, h):
        tt = nt - 1 - t if rev else t
        r0 = pl.multiple_of(tt * SUBLANES, SUBLANES)
        h = a_ref[pl.ds(r0, SUBLANES), :] * h + b_ref[pl.ds(r0, SUBLANES), :]
        if emit:
            hs_ref[pl.ds(r0, SUBLANES), :] = h
        return h

    st_ref[...] = lax.fori_loop(0, nt, step, st_ref[...], unroll=8)

    @pl.when(i == pl.num_programs(0) - 1)
    def _():
        hf_ref[...] = st_ref[...]


def _lru_scan(xc, wa, wx, ba, bx, nsp, h0, *, rev, emit=True):
    n_rows, wd = xc.shape
    tm = _row_tile(n_rows)
    ntile = n_rows // tm
    tile = (lambda i: (ntile - 1 - i, 0)) if rev else (lambda i: (i, 0))
    full = lambda a: pl.BlockSpec(a.shape, lambda i: (0,) * a.ndim)
    hs, hf = pl.pallas_call(
        functools.partial(_lru_scan_kernel, rev=rev, emit=emit),
        grid=(ntile,),
        in_specs=[pl.BlockSpec((tm, wd), tile), full(wa), full(wx), full(ba), full(bx), full(nsp), full(h0)],
        out_specs=[pl.BlockSpec((tm, wd), tile if emit else (lambda i: (0, 0))), full(h0)],
        out_shape=[jax.ShapeDtypeStruct((n_rows if emit else tm, wd), F32), jax.ShapeDtypeStruct(h0.shape, F32)],
        scratch_shapes=[pltpu.VMEM(h0.shape, F32), pltpu.VMEM((tm, wd), F32), pltpu.VMEM((tm, wd), F32)],
        compiler_params=_cparams("arbitrary"),
        name="lru_scan_bwd" if rev else "lru_scan_fwd",
    )(xc, wa, wx, ba, bx, nsp, h0)
    return (hs if emit else None), hf


def _lru_out_prologue(yg_ref, hf_ref, hb_ref):
    return yg_ref[...] * (hf_ref[...] + hb_ref[...])


def _lru_mixer(x, x_ctx, modt, modt_ctx, norm_g, prm, *, ctx_out):
    d = x.shape[-1]
    w_in = prm['w_in'].astype(BF16)
    wd = prm['conv_w'].shape[-1]
    xc_l, yg_l = _lru_in(x, modt, norm_g, w_in, prm['conv_w'], prm['conv_b'])
    xc_c, yg_c = _lru_in(x_ctx, modt_ctx, norm_g, w_in, prm['conv_w'], prm['conv_b'])
    zero = jnp.zeros((SUBLANES, wd), F32)
    hs_l, hs_c = [], []
    for dr in range(2):
        wa, wx = prm['w_a'][dr].astype(BF16), prm['w_x'][dr].astype(BF16)
        ba, bx = prm['b_a'][dr].reshape(1, wd), prm['b_x'][dr].reshape(1, wd)
        nsp = (-LRU_C * jax.nn.softplus(-prm['lam'][dr].astype(F32))).reshape(1, wd)
        hc, hfin = _lru_scan(xc_c, wa, wx, ba, bx, nsp, zero, rev=dr == 1, emit=ctx_out)
        hl, _ = _lru_scan(xc_l, wa, wx, ba, bx, nsp, hfin, rev=dr == 1)
        hs_l.append(hl)
        hs_c.append(hc)
    w_out = prm['w_out'].astype(BF16)

    def out(xa, yg, hs, mt):
        tm = _row_tile(xa.shape[0], cap=ROW_TILE_CAP // 2)
        specs = [_row_spec(tm, wd)] * 3
        return _outproj(_lru_out_prologue, [yg, hs[0], hs[1]], specs, wd, xa, mt, 10 ** 9, w_out, tm=tm, name="lru_out")

    return out(x, yg_l, hs_l, modt), (out(x_ctx, yg_c, hs_c, modt_ctx) if ctx_out else None)


def _rope_tables(length):
    rows = length // GRID_W
    t = jnp.arange(length, dtype=jnp.int32)
    row = (t // GRID_W).astype(F32) - (rows - 1) / 2.0
    col = (t % GRID_W).astype(F32) - (GRID_W - 1) / 2.0
    quarter = RET_DK // 4
    inv_freq = ROPE_BASE ** (-jnp.arange(quarter, dtype=F32) / quarter)
    ar, ac = row[:, None] * inv_freq[None, :], col[:, None] * inv_freq[None, :]
    cos = jnp.concatenate([jnp.cos(ar), jnp.cos(ar), jnp.cos(ac), jnp.cos(ac)], axis=-1)
    sin = jnp.concatenate([-jnp.sin(ar), jnp.sin(ar), -jnp.sin(ac), jnp.sin(ac)], axis=-1)
    return cos, sin


def _ret_in_kernel(x_ref, g_ref, sh_ref, sc_ref, w_ref, cos_ref, sin_ref, p_ref, h_ref, *, rotate, n_k, n_v):
    j = pl.program_id(1)
    tn = w_ref.shape[1]

    @pl.when(j == 0)
    def _():
        h_ref[...] = _norm_mod(x_ref[...], g_ref[...], sc_ref[...], sh_ref[...]).astype(BF16)

    is_k = j < n_k
    is_q = (j >= n_k + n_v) & (j < 2 * n_k + n_v)

    @pl.when(is_k | is_q)
    def _():
        scale = jnp.where(is_q, RET_DK ** -0.5, 1.0)
        for hd in range(tn // RET_DK):
            ph = _dot(h_ref[...], w_ref[:, hd * RET_DK:(hd + 1) * RET_DK])
            for s in range(RET_DK // LANES):
                tcols = slice(s * LANES, (s + 1) * LANES)
                ps = ph[:, tcols]
                if rotate:
                    rot = pltpu.roll(ps, LANES // 2, axis=1)
                    ps = ps * cos_ref[:, tcols] + rot * sin_ref[:, tcols]
                p_ref[:, hd * RET_DK + s * LANES:hd * RET_DK + (s + 1) * LANES] = ps * scale

    @pl.when(jnp.logical_not(is_k | is_q))
    def _():
        p_ref[...] = _dot(h_ref[...], w_ref[...])


def _ret_in(x, modt, tiles_per_mod, norm_g, w_in, seq_rows, *, rotate):
    n_rows, d = x.shape
    n_out = w_in.shape[1]
    tm = _row_tile(seq_rows)
    tn = RET_HEADS * RET_DK
    tiles_per_seq = seq_rows // tm
    cos, sin = _rope_tables(seq_rows)
    hk, hv = RET_HEADS * RET_DK, RET_HEADS * RET_DV
    tab_spec = pl.BlockSpec((tm, RET_DK), lambda i, j: (i % tiles_per_seq, 0))
    return pl.pallas_call(
        functools.partial(_ret_in_kernel, rotate=rotate, n_k=hk // tn, n_v=hv // tn),
        grid=(n_rows // tm, n_out // tn),
        in_specs=[_row_spec(tm, d), _const_spec((1, d)), _mod_spec(0, d, tiles_per_mod), _mod_spec(1, d, tiles_per_mod),
                  pl.BlockSpec((d, tn), lambda i, j: (0, j)), tab_spec, tab_spec],
        out_specs=pl.BlockSpec((tm, tn), lambda i, j: (i, j)),
        out_shape=jax.ShapeDtypeStruct((n_rows, n_out), F32),
        scratch_shapes=[pltpu.VMEM((tm, d), BF16)],
        compiler_params=_cparams("parallel", "arbitrary"),
        name="ret_in",
    )(x, norm_g.reshape(1, d), modt, modt, w_in, cos, sin)


def _ret_tables(c):
    log_g = np.log1p(-np.power(2.0, -5.0 - np.arange(RET_HEADS, dtype=np.float64)))
    idx = np.arange(c, dtype=np.float64)
    diff = idx[:, None] - idx[None, :]
    fwd = np.where(diff >= 0, np.exp(np.where(diff >= 0, diff, 0.0)[None] * log_g[:, None, None]), 0.0)
    bwd = np.where(diff < 0, np.exp(np.where(diff < 0, -diff, 0.0)[None] * log_g[:, None, None]), 0.0)
    xi_f = np.exp((idx + 1.0)[None, :] * log_g[:, None])
    zeta_f = np.exp((c - 1.0 - idx)[None, :] * log_g[:, None])
    xi_b = np.exp((c - idx)[None, :] * log_g[:, None])
    zeta_b = np.exp(idx[None, :] * log_g[:, None])
    dmask = np.stack([fwd, bwd]).astype(np.float32)
    xi = np.stack([xi_f, xi_b])[..., None].astype(np.float32)
    zeta = np.stack([zeta_f, zeta_b])[..., None].astype(np.float32)
    g_blk = [float(v) for v in np.exp(c * log_g).astype(np.float32)]
    return jnp.asarray(dmask), jnp.asarray(xi), jnp.asarray(zeta), g_blk


def _ret_scan_kernel(k_ref, v0_ref, v1_ref, q_ref, dm_ref, xi_ref, zt_ref, r0_ref, o_ref, rf_ref, r_ref, *, g_blk, emit):
    c = pl.program_id(2)

    @pl.when(c == 0)
    def _():
        r_ref[...] = r0_ref[...]
        if not emit:
            o_ref[...] = jnp.zeros_like(o_ref)

    hv_half = v0_ref.shape[1] // RET_DV
    for h in range(RET_HEADS):
        kh = k_ref[:, h * RET_DK:(h + 1) * RET_DK]
        v_ref = v0_ref if h < hv_half else v1_ref
        hh = h % hv_half
        vh = v_ref[:, hh * RET_DV:(hh + 1) * RET_DV].astype(BF16)
        r_old = r_ref[h]
        if emit:
            qh = q_ref[:, h * RET_DK:(h + 1) * RET_DK].astype(BF16)
            s = _dot_nt(qh, kh.astype(BF16)) * dm_ref[h]
            o = _dot(s.astype(BF16), vh) + _dot(qh, r_old.astype(BF16)) * xi_ref[h]
            o_ref[:, h * RET_DV:(h + 1) * RET_DV] = o
        r_ref[h] = g_blk[h] * r_old + _dot_tn((kh * zt_ref[h]).astype(BF16), vh)

    @pl.when(c == pl.num_programs(2) - 1)
    def _():
        rf_ref[...] = r_ref[...]


def _ret_scan(p, r0, batch, seq_rows, *, emit=True):
    c = min(RET_CHUNK, seq_rows)
    nc = seq_rows // c
    hk, hv = RET_HEADS * RET_DK, RET_HEADS * RET_DV
    dmask, xi, zeta, g_blk = _ret_tables(c)
    n_rows = p.shape[0]

    def rows(d, b, cc):
        return b * nc + jnp.where(d == 0, cc, nc - 1 - cc)

    kcol = lambda blk: pl.BlockSpec((c, hk), lambda d, b, cc: (rows(d, b, cc), blk))
    tab = lambda a: pl.BlockSpec((None,) + a.shape[1:], lambda d, b, cc: (d,) + (0,) * (a.ndim - 1))
    st_spec = pl.BlockSpec((None, None, RET_HEADS, RET_DK, RET_DV), lambda d, b, cc: (d, b, 0, 0, 0))
    assert hv == 2 * hk
    o, rf = pl.pallas_call(
        functools.partial(_ret_scan_kernel, g_blk=g_blk, emit=emit),
        grid=(2, batch, nc),
        in_specs=[kcol(0), kcol(1), kcol(2), kcol(3), tab(dmask), tab(xi), tab(zeta), st_spec],
        out_specs=[pl.BlockSpec((None, c, hv), (lambda d, b, cc: (d, rows(d, b, cc), 0)) if emit else
                                (lambda d, b, cc: (0, 0, 0))), st_spec],
        out_shape=[jax.ShapeDtypeStruct((2, n_rows if emit else c, hv), F32), jax.ShapeDtypeStruct(r0.shape, F32)],
        scratch_shapes=[pltpu.VMEM((RET_HEADS, RET_DK, RET_DV), F32)],
        compiler_params=_cparams("arbitrary", "arbitrary", "arbitrary"),
        name="ret_scan",
    )(p, p, p, p, dmask, xi, zeta, r0)
    return o, rf


def _ret_out_prologue(of_ref, ob_ref, gate_ref, ng_ref):
    o = of_ref[...] + ob_ref[...]
    parts = []
    for h in range(RET_HEADS):
        oh = o[:, h * RET_DV:(h + 1) * RET_DV]
        parts.append(oh * lax.rsqrt(jnp.mean(oh * oh, axis=-1, keepdims=True) + NORM_EPS))
    on = (jnp.concatenate(parts, axis=-1) * ng_ref[...])
    return jax.nn.silu(gate_ref[...]) * on


def _ret_mixer(x, x_ctx, modt, modt_ctx, norm_g, prm, batch, seq, seq_ctx, *, ctx_out):
    d = x.shape[-1]
    w_in = prm['w_in'].astype(BF16)
    hk, hv = RET_HEADS * RET_DK, RET_HEADS * RET_DV
    tiles_per_mod = seq // _row_tile(seq)
    p_l = _ret_in(x, modt, tiles_per_mod, norm_g, w_in, seq, rotate=True)
    p_c = _ret_in(x_ctx, modt_ctx, 10 ** 9, norm_g, w_in, seq_ctx, rotate=False)
    r0 = jnp.zeros((2, batch, RET_HEADS, RET_DK, RET_DV), F32)
    o_c, r_c = _ret_scan(p_c, r0, batch, seq_ctx, emit=ctx_out)
    o_l, _ = _ret_scan(p_l, r_c, batch, seq)
    w_out = prm['w_out'].astype(BF16)
    ng = prm['norm_g'].reshape(1, hv)

    def out(xa, pa, oa, mt, tpm, seq_rows):
        tm = _row_tile(seq_rows, cap=ROW_TILE_CAP // 4)
        gcol = (2 * hk + hv) // hv
        specs = [pl.BlockSpec((None, tm, hv), lambda i, j: (0, i, 0)), pl.BlockSpec((None, tm, hv), lambda i, j: (1, i, 0)),
                 pl.BlockSpec((tm, hv), lambda i, j: (i, gcol)), _const_spec((1, hv))]
        return _outproj(_ret_out_prologue, [oa, oa, pa, ng], specs, hv, xa, mt, tpm * (_row_tile(seq_rows) // tm), w_out,
                        tm=tm, name="ret_out")

    y_l = out(x, p_l, o_l, modt, tiles_per_mod, seq)
    y_c = out(x_ctx, p_c, o_c, modt_ctx, 10 ** 9, seq_ctx) if ctx_out else None
    return y_l, y_c


GDN_IN_TILE = 8 * LANES


def _gdn_in_kernel(x_ref, xb_ref, xa_ref, g_ref, sh_ref, sc_ref, w_ref, cw_ref, o_ref, h_ref, *, tiles_per_seq, n_kq, n_conv):
    i, j = pl.program_id(0), pl.program_id(1)
    tm = x_ref.shape[0]
    hb = ha = SUBLANES
    heads = o_ref.shape[0]

    @pl.when(j == 0)
    def _():
        first = i % tiles_per_seq == 0
        last = i % tiles_per_seq == tiles_per_seq - 1
        _fill_h(h_ref, x_ref, xb_ref, xa_ref, g_ref, sc_ref, sh_ref, first, last, hb, ha)

    pair = 2 * LANES

    def conv_silu(s2):
        p = _dot(h_ref[...], w_ref[:, s2 * pair:(s2 + 1) * pair])
        return jax.nn.silu(_conv_rows(p, cw_ref[:, s2 * pair:(s2 + 1) * pair], tm, 1, hb))

    @pl.when(j < n_kq)
    def _():
        scale = jnp.where(j >= n_kq // 2, GDN_DK ** -0.5, 1.0)
        for s2 in range(heads // 2):
            a = conv_silu(s2)
            for e in range(2):
                ah = a[:, e * LANES:(e + 1) * LANES]
                o_ref[2 * s2 + e] = ah * (lax.rsqrt(jnp.sum(ah * ah, axis=-1, keepdims=True) + 1e-6) * scale)

    @pl.when((j >= n_kq) & (j < n_conv))
    def _():
        for s2 in range(heads // 2):
            a = conv_silu(s2)
            for e in range(2):
                o_ref[2 * s2 + e] = a[:, e * LANES:(e + 1) * LANES]

    @pl.when(j >= n_conv)
    def _():
        for s2 in range(heads // 2):
            p = _dot(h_ref[hb:hb + tm, :], w_ref[:, s2 * pair:(s2 + 1) * pair])
            for e in range(2):
                o_ref[2 * s2 + e] = p[:, e * LANES:(e + 1) * LANES]


def _gdn_in(x, modt, tiles_per_mod, norm_g, w_all, cw_all, seq_rows):
    n_rows, d = x.shape
    n_out = w_all.shape[1]
    tm = _row_tile(seq_rows)
    tn = GDN_IN_TILE
    hb = ha = SUBLANES
    heads = tn // LANES
    n_kq = 2 * GDN_K_HEADS * GDN_DK // tn
    n_conv = n_kq + GDN_V_HEADS * GDN_DV // tn
    kern = functools.partial(_gdn_in_kernel, tiles_per_seq=seq_rows // tm, n_kq=n_kq, n_conv=n_conv)
    specs = _halo_specs(tm, hb, ha, d, n_rows) + [
        _const_spec((1, d)), _mod_spec(0, d, tiles_per_mod), _mod_spec(1, d, tiles_per_mod),
        pl.BlockSpec((d, tn), lambda i, j: (0, j)), pl.BlockSpec((cw_all.shape[0], tn), lambda i, j: (0, j))]
    return pl.pallas_call(
        kern,
        grid=(n_rows // tm, n_out // tn),
        in_specs=specs,
        out_specs=pl.BlockSpec((heads, tm, LANES), lambda i, j: (j, i, 0)),
        out_shape=jax.ShapeDtypeStruct((n_out // LANES, n_rows, LANES), F32),
        scratch_shapes=[pltpu.VMEM((tm + hb + ha, d), BF16)],
        compiler_params=_cparams("parallel", "arbitrary"),
        name="gdn_in",
    )(x, x, x, norm_g.reshape(1, d), modt, modt, w_all, cw_all)


def _split3(a):
    hi = a.astype(BF16)
    lo = (a - hi.astype(F32)).astype(BF16)
    return hi, lo


def _dot3s(a, b):
    ah, al = a
    bh, bl = b
    m = ah.shape[0]
    both = _dot(jnp.concatenate([ah, al], axis=0), bh)
    return both[:m] + (both[m:] + _dot(ah, bl))


def _cumsum_rows(x, rev):
    n = x.shape[0]
    row = lax.broadcasted_iota(jnp.int32, x.shape, 0)
    shift = 1
    while shift < n:
        if rev:
            x = x + jnp.where(row < n - shift, pltpu.roll(x, n - shift, axis=0), 0.0)
        else:
            x = x + jnp.where(row >= shift, pltpu.roll(x, shift, axis=0), 0.0)
        shift *= 2
    return x


GDN_GROUP = 4


def _gdn_expanders(dr, c):
    nv, gs = GDN_V_HEADS, GDN_GROUP
    lane = np.arange(LANES)[:, None]

    def build(first_lane, width):
        col_head = np.arange(gs * width)[None, :] // width
        return np.stack([(lane == first_lane + gs * g + col_head) for g in range(nv // gs)]).astype(np.float32)

    ig, ib = dr * 2 * nv + nv, dr * 2 * nv
    return [jnp.asarray(a, dtype=BF16) for a in (build(ig, c), build(ib, c), build(ig, GDN_DV), build(ib, GDN_DV))]


def _gdn_scan_kernel(kf_ref, qf_ref, vf_ref, baf_ref, kb_ref, qb_ref, vb_ref, bab_ref, an_ref, dtb_ref, eg_ref, eb_ref,
                     eg5_ref, eb5_ref, s0_ref, of_ref, ob_ref, sf_ref, s_ref, *, emit):
    cidx = pl.program_id(1)
    c = kf_ref.shape[1]
    rep = GDN_V_HEADS // GDN_K_HEADS
    dir_refs = ((kf_ref, qf_ref, vf_ref, baf_ref, of_ref), (kb_ref, qb_ref, vb_ref, bab_ref, ob_ref))

    @pl.when(cidx == 0)
    def _():
        s_ref[...] = s0_ref[...]
        if not emit:
            of_ref[...] = jnp.zeros_like(of_ref)
            ob_ref[...] = jnp.zeros_like(ob_ref)

    def expand(split, e):
        r = _dot(split, e)
        return r[:c] + r[c:]

    gs = GDN_GROUP
    wide = gs * c
    ii = lax.broadcasted_iota(jnp.int32, (c, wide), 0)
    jj = lax.broadcasted_iota(jnp.int32, (c, wide), 1) % c
    br = lax.broadcasted_iota(jnp.int32, (wide, wide), 0) // c
    bc = lax.broadcasted_iota(jnp.int32, (wide, wide), 1) // c
    blockmask = (br == bc).astype(F32).astype(BF16)
    nk, nv = GDN_K_HEADS, GDN_V_HEADS
    zc = jnp.zeros((c, GDN_DK), BF16)
    zs = jnp.zeros((GDN_DK, GDN_DV), BF16)
    zr = jnp.zeros((c, GDN_DV), BF16)

    gc_split, bt_split, incl, strict, eye, last = [], [], [], [], [], []
    for d in range(2):
        rev = d == 1
        ba = dir_refs[d][3][...]
        z = ba + dtb_ref[d]
        sp = jnp.maximum(z, 0.0) + jnp.log(1.0 + jnp.exp(-jnp.abs(z)))
        gc = _cumsum_rows(an_ref[d] * sp, rev)
        gc_split.append(jnp.concatenate(_split3(gc), axis=0))
        bt_split.append(jnp.concatenate(_split3(jax.nn.sigmoid(ba)), axis=0))
        diff = (jj - ii) if rev else (ii - jj)
        incl.append(diff >= 0)
        strict.append(diff > 0)
        eye.append((diff == 0).astype(F32))
        last.append(0 if rev else c - 1)

    def bdiag(blocks, zero):
        n = len(blocks)
        return jnp.concatenate([jnp.concatenate([blocks[r] if r == q else zero for q in range(n)], axis=1)
                                for r in range(n)], axis=0)

    def bd_tile(a16):
        return jnp.concatenate([a16] * gs, axis=0) * blockmask

    def dot3_bd(lhs, rhs_hi, rhs_lo):
        lh, ll = _split3(lhs)
        m = lhs.shape[0]
        both = _dot(jnp.concatenate([lh, ll], axis=0), rhs_hi)
        return both[:m] + (both[m:] + _dot(lh, rhs_lo))

    pieces = lambda a: [a[:, h * GDN_DV:(h + 1) * GDN_DV] for h in range(gs)]
    chains = [(d, g) for g in range(nv // gs) for d in range(2)]
    ks = [[dir_refs[d][0][h] for h in range(nk)] for d in range(2)]
    qs = [[dir_refs[d][1][h] for h in range(nk)] for d in range(2)] if emit else None
    vs = [[dir_refs[d][2][h] for h in range(nv)] for d in range(2)]
    s_olds = [[s_ref[d, h] for h in range(nv)] for d in range(2)]

    xs, ts, attns = [], [], []
    for d, g in chains:
        k0, k1 = ks[d][2 * g], ks[d][2 * g + 1]
        k016, k116 = k0.astype(BF16), k1.astype(BF16)
        rhs_t = jnp.concatenate([jnp.concatenate([k016, zc], axis=1)] * rep
                                + [jnp.concatenate([zc, k116], axis=1)] * rep, axis=0)
        if emit:
            lhs = jnp.concatenate([jnp.concatenate([k0, qs[d][2 * g]], axis=0),
                                   jnp.concatenate([k1, qs[d][2 * g + 1]], axis=0)], axis=1).astype(BF16)
        else:
            lhs = jnp.concatenate([k016, k116], axis=1)
        gram = _dot_nt(lhs, rhs_t)
        gcm = expand(gc_split[d], eg_ref[d, g])
        grow = jnp.sum(gcm * eye[d], axis=0, keepdims=True)
        dec = jnp.where(incl[d], jnp.exp(jnp.where(incl[d], gcm - grow, 0.0)), 0.0)
        btm = expand(bt_split[d], eb_ref[d, g])
        x = jnp.where(strict[d], -(gram[:c] * btm) * dec, 0.0)
        xs.append(x)
        ts.append(eye[d] + x)
        attns.append(gram[c:] * dec if emit else None)

    splits = [_split3(x) for x in xs]
    xs = [dot3_bd(x, bd_tile(xh), bd_tile(xl)) for x, (xh, xl) in zip(xs, splits)]
    nlev = int(math.log2(c)) - 1
    for lvl in range(1, nlev + 1):
        splits = [_split3(x) for x in xs]
        if lvl < nlev:
            ps = [dot3_bd(jnp.concatenate([t, x], axis=0), bd_tile(xh), bd_tile(xl))
                  for t, x, (xh, xl) in zip(ts, xs, splits)]
            ts = [t + p[:c] for t, p in zip(ts, ps)]
            xs = [p[c:] for p in ps]
        else:
            ts = [t + dot3_bd(t, bd_tile(xh), bd_tile(xl)) for t, (xh, xl) in zip(ts, splits)]

    gc5s, bt5s, kcats, p1s = [], [], [], []
    for d, g in chains:
        k0, k1 = ks[d][2 * g], ks[d][2 * g + 1]
        gc5 = expand(gc_split[d], eg5_ref[d, g])
        bt5 = expand(bt_split[d], eb5_ref[d, g])
        eg5 = jnp.exp(gc5)
        kcat = jnp.concatenate([k0] * rep + [k1] * rep, axis=1)
        kbe = kcat * bt5 * eg5
        s16 = [s_olds[d][gs * g + h].astype(BF16) for h in range(gs)]
        if emit:
            qe = jnp.concatenate([qs[d][2 * g]] * rep + [qs[d][2 * g + 1]] * rep, axis=1) * eg5
            top = jnp.concatenate([kbe, qe], axis=0).astype(BF16)
        else:
            top = kbe.astype(BF16)
        p1s.append(jnp.concatenate([_dot(top[:, 2 * pr * GDN_DK:(2 * pr + 2) * GDN_DK], bdiag(s16[2 * pr:2 * pr + 2], zs))
                                    for pr in range(gs // 2)], axis=1))
        gc5s.append(gc5)
        bt5s.append(bt5)
        kcats.append(kcat)

    bdvs = []
    for n, (d, g) in enumerate(chains):
        vcat = jnp.concatenate(vs[d][gs * g:gs * (g + 1)], axis=1)
        rh, rl = _split3(vcat * bt5s[n] - p1s[n][:c])
        vn16 = dot3_bd(ts[n], bdiag(pieces(rh), zr), bdiag(pieces(rl), zr)).astype(BF16)
        bdvs.append(bdiag(pieces(vn16), zr))

    outs = [[None] * nv for _ in range(2)]
    s_news = [[None] * nv for _ in range(2)]
    for n, (d, g) in enumerate(chains):
        gl5 = gc5s[n][last[d]:last[d] + 1, :]
        if emit:
            o = pieces(p1s[n][c:] + _dot(attns[n].astype(BF16), bdvs[n]))
        kd = (kcats[n] * jnp.exp(gl5 - gc5s[n])).astype(BF16)
        sn = _dot_tn(jnp.concatenate(pieces(kd), axis=0), bdvs[n])
        egl = jnp.exp(gl5)
        for h in range(gs):
            cols = slice(h * GDN_DV, (h + 1) * GDN_DV)
            s_news[d][gs * g + h] = s_olds[d][gs * g + h] * egl[:, cols] + sn[:, cols]
            if emit:
                outs[d][gs * g + h] = o[h]
    for d in range(2):
        for hv in range(nv):
            if emit:
                dir_refs[d][4][hv] = outs[d][hv]
            s_ref[d, hv] = s_news[d][hv]

    @pl.when(cidx == pl.num_programs(1) - 1)
    def _():
        sf_ref[...] = s_ref[...]


def _gdn_scan(kqvz, an, dtb, s0, batch, seq_rows, *, emit=True):
    c = min(GDN_CHUNK, seq_rows)
    nc = seq_rows // c
    n_rows = kqvz.shape[1]
    rows = (lambda b, cc: b * nc + cc, lambda b, cc: b * nc + nc - 1 - cc)
    kh, vh = GDN_K_HEADS, GDN_V_HEADS
    ba_head = 2 * kh + 2 * vh
    st_spec = pl.BlockSpec((2, None, vh, GDN_DK, GDN_DV), lambda b, cc: (0, b, 0, 0, 0))
    expanders = [jnp.stack(pair) for pair in zip(_gdn_expanders(0, c), _gdn_expanders(1, c))]
    full = lambda a: pl.BlockSpec(a.shape, lambda b, cc: (0,) * a.ndim)

    def dir_specs(rw):
        return [pl.BlockSpec((kh, c, LANES), lambda b, cc: (0, rw(b, cc), 0)),
                pl.BlockSpec((kh, c, LANES), lambda b, cc: (1, rw(b, cc), 0)),
                pl.BlockSpec((vh, c, LANES), lambda b, cc: (1, rw(b, cc), 0)),
                pl.BlockSpec((None, c, LANES), lambda b, cc: (ba_head, rw(b, cc), 0))]

    def o_spec(rw):
        return pl.BlockSpec((vh, c, LANES), (lambda b, cc: (0, rw(b, cc), 0)) if emit else (lambda b, cc: (0, 0, 0)))

    o_shape = jax.ShapeDtypeStruct((vh, n_rows if emit else c, LANES), F32)
    of, ob, sf = pl.pallas_call(
        functools.partial(_gdn_scan_kernel, emit=emit),
        grid=(batch, nc),
        in_specs=dir_specs(rows[0]) + dir_specs(rows[1]) + [full(an), full(dtb)] + [full(e) for e in expanders] + [st_spec],
        out_specs=[o_spec(rows[0]), o_spec(rows[1]), st_spec],
        out_shape=[o_shape, o_shape, jax.ShapeDtypeStruct(s0.shape, F32)],
        scratch_shapes=[pltpu.VMEM((2, vh, GDN_DK, GDN_DV), F32)],
        compiler_params=_cparams("arbitrary", "arbitrary"),
        name="gdn_scan",
    )(*([kqvz] * 8), an, dtb, *expanders, s0)
    return of, ob, sf


def _gdn_out_prologue(of_ref, ob_ref, z_ref, ng_ref):
    parts = []
    for h in range(GDN_V_HEADS):
        o = of_ref[h] + ob_ref[h]
        on = o * lax.rsqrt(jnp.mean(o * o, axis=-1, keepdims=True) + NORM_EPS) * ng_ref[...]
        parts.append(on * jax.nn.silu(z_ref[h]))
    return jnp.concatenate(parts, axis=-1)


def _gdn_mixer(x, x_ctx, modt, modt_ctx, norm_g, prm, batch, seq, seq_ctx, *, ctx_out):
    d = x.shape[-1]
    qk, vv = GDN_K_HEADS * GDN_DK, GDN_V_HEADS * GDN_DV
    ng2 = 2 * 2 * GDN_V_HEADS
    w_in, conv_w = prm['w_in'], prm['conv_w']
    pad = GDN_IN_TILE - ng2
    w_all = jnp.concatenate([w_in[:, :qk], w_in[:, qk + vv + ng2:2 * qk + vv + ng2], w_in[:, qk:qk + vv],
                             w_in[:, 2 * qk + vv + ng2:], w_in[:, qk + vv:qk + vv + ng2],
                             jnp.zeros((d, pad), w_in.dtype)], axis=1).astype(BF16)
    cw_all = jnp.concatenate([conv_w[:, :qk], conv_w[:, qk + vv:], conv_w[:, qk:qk + vv],
                              jnp.zeros((conv_w.shape[0], vv + GDN_IN_TILE), conv_w.dtype)], axis=1)
    tiles_per_mod = seq // _row_tile(seq)
    p_l = _gdn_in(x, modt, tiles_per_mod, norm_g, w_all, cw_all, seq)
    p_c = _gdn_in(x_ctx, modt_ctx, 10 ** 9, norm_g, w_all, cw_all, seq_ctx)
    s0 = jnp.zeros((2, batch, GDN_V_HEADS, GDN_DK, GDN_DV), F32)

    def gate_lanes(p):
        rows = [jnp.zeros((LANES,), F32).at[dr * 2 * GDN_V_HEADS + GDN_V_HEADS:(dr + 1) * 2 * GDN_V_HEADS].set(p[dr])
                for dr in range(2)]
        return jnp.stack(rows).reshape(2, 1, LANES)

    an = gate_lanes(-jnp.exp(prm['a_log'].astype(F32)))
    dtb = gate_lanes(prm['dt_bias'].astype(F32))
    ocf, ocb, sc = _gdn_scan(p_c, an, dtb, s0, batch, seq_ctx, emit=ctx_out)
    olf, olb, _ = _gdn_scan(p_l, an, dtb, sc, batch, seq)
    os_l, os_c = [olf, olb], [ocf, ocb]
    w_out = prm['w_out'].astype(BF16)
    ng = prm['norm_g'].reshape(1, GDN_DV)
    vh = GDN_V_HEADS
    z_blk = (2 * GDN_K_HEADS + vh) // vh

    def out(xa, pa, oa, mt, tpm, seq_rows):
        tm = _row_tile(seq_rows, cap=ROW_TILE_CAP // 4)
        hspec = lambda blk: pl.BlockSpec((vh, tm, LANES), lambda i, j: (blk, i, 0))
        specs = [hspec(0), hspec(0), hspec(z_blk), _const_spec((1, GDN_DV))]
        return _outproj(_gdn_out_prologue, [oa[0], oa[1], pa, ng], specs, vv, xa, mt, tpm * (_row_tile(seq_rows) // tm),
                        w_out, tm=tm, name="gdn_out")

    y_l = out(x, p_l, os_l, modt, tiles_per_mod, seq)
    y_c = out(x_ctx, p_c, os_c, modt_ctx, 10 ** 9, seq_ctx) if ctx_out else None
    return y_l, y_c


def kernel(x, c, ctx, c_ctx, norm1_g, norm2_g, mod_w, mod_b, ffn_w_in, ffn_conv_w, ffn_conv_b, ffn_w_out, s5_lam_re, s5_lam_im, s5_log_step, s5_b_re, s5_b_im, s5_c_re, s5_c_im, s5_d, s5_w_glu, s5_b_glu, lru_w_in, lru_conv_w, lru_conv_b, lru_w_a, lru_b_a, lru_w_x, lru_b_x, lru_lam, lru_w_out, ret_w_in, ret_norm_g, ret_w_out, gdn_w_in, gdn_conv_w, gdn_a_log, gdn_dt_bias, gdn_norm_g, gdn_w_out, final_norm_g):
    B, L, D = x.shape
    Lc = ctx.shape[1]
    depth = mod_w.shape[0]
    assert B == SUBLANES and depth == 4, "time-major layers put the batch on the 8 sublanes; four mixer kinds"
    c16 = jnp.concatenate([c, c_ctx[None], jnp.zeros((16 - B - 1, D), F32)], 0)
    mods = _modulation(c16, mod_w, mod_b)

    def ffn(i, xa, mt, tiles_per_mod, rs, seq_rows, final_norm=False):
        return _ffn(xa, mt, tiles_per_mod, norm2_g[i], ffn_w_in[i].astype(BF16), ffn_conv_w[i], ffn_conv_b[i],
                    ffn_w_out[i].astype(BF16), final_norm_g, rs=rs, seq_rows=seq_rows, final_norm=final_norm)

    xt = jnp.swapaxes(x, 0, 1).reshape(L * B, D)
    ct = jnp.swapaxes(ctx, 0, 1).reshape(Lc * B, D)
    s5p = dict(lam_re=s5_lam_re[0], lam_im=s5_lam_im[0], log_step=s5_log_step[0], b_re=s5_b_re[0], b_im=s5_b_im[0],
               c_re=s5_c_re[0], c_im=s5_c_im[0], d=s5_d[0], w_glu=s5_w_glu[0], b_glu=s5_b_glu[0])
    lrup = dict(w_in=lru_w_in[0], conv_w=lru_conv_w[0], conv_b=lru_conv_b[0], w_a=lru_w_a[0], b_a=lru_b_a[0],
                w_x=lru_w_x[0], b_x=lru_b_x[0], lam=lru_lam[0], w_out=lru_w_out[0])
    for i, (mixer, prm) in enumerate(((_s5_mixer, s5p), (_lru_mixer, lrup))):
        mt = mods[i, :B][None]
        mtc = jnp.broadcast_to(mods[i, B][None, None, :], (1, SUBLANES, N_MOD * D))
        xt, ct = mixer(xt, ct, mt, mtc, norm1_g[i], prm, ctx_out=True)
        xt = ffn(i, xt, mt, 10 ** 9, SUBLANES, L * B)
        ct = ffn(i, ct, mtc, 10 ** 9, SUBLANES, Lc * B)

    xb = jnp.swapaxes(xt.reshape(L, B, D), 0, 1).reshape(B * L, D)
    cb = jnp.swapaxes(ct.reshape(Lc, B, D), 0, 1).reshape(B * Lc, D)
    retp = dict(w_in=ret_w_in[0], norm_g=ret_norm_g[0], w_out=ret_w_out[0])
    gdnp = dict(w_in=gdn_w_in[0], conv_w=gdn_conv_w[0], a_log=gdn_a_log[0], dt_bias=gdn_dt_bias[0],
                norm_g=gdn_norm_g[0], w_out=gdn_w_out[0])
    tiles_per_batch = L // _row_tile(L)
    for i, (mixer, prm) in ((2, (_ret_mixer, retp)), (3, (_gdn_mixer, gdnp))):
        ctx_out = i < depth - 1
        mt = jnp.broadcast_to(mods[i, :B][:, None, :], (B, SUBLANES, N_MOD * D))
        mtc = jnp.broadcast_to(mods[i, B][None, None, :], (1, SUBLANES, N_MOD * D))
        xb, cb_new = mixer(xb, cb, mt, mtc, norm1_g[i], prm, B, L, Lc, ctx_out=ctx_out)
        xb = ffn(i, xb, mt, tiles_per_batch, 1, L, final_norm=not ctx_out)
        if ctx_out:
            cb = ffn(i, cb_new, mtc, 10 ** 9, 1, Lc)
    return xb.reshape(B, L, D)
```

```python
import functools
import math

import numpy as np
import jax
import jax.numpy as jnp
from jax import lax
from jax.experimental import pallas as pl
from jax.experimental.pallas import tpu as pltpu

F32 = jnp.float32
BF16 = jnp.bfloat16

NORM_EPS = 1e-6
SUBLANES = 8
LANES = 128
VMEM_LIMIT_BYTES = 56 * 1024 * 1024
ROW_TILE_CAP = 1024

S5_GROUP = 16
S5_STATE = 64
S5_GB = 16
LRU_BLOCKS = 4
LRU_C = 8.0
RET_HEADS = 4
RET_DK = 256
RET_DV = 512
RET_CHUNK = 128
ROPE_BASE = 10000.0
GRID_W = 64
GDN_K_HEADS = 8
GDN_V_HEADS = 16
GDN_DK = 128
GDN_DV = 128
GDN_CHUNK = 64
N_MOD = 6


def _cparams(*sem):
    return pltpu.CompilerParams(dimension_semantics=sem, vmem_limit_bytes=VMEM_LIMIT_BYTES)


def _row_tile(rows, cap=None):
    cap = ROW_TILE_CAP if cap is None else cap
    tm = min(cap, rows)
    assert rows % tm == 0 and tm % SUBLANES == 0, (rows, tm)
    return tm


def _dot(a, b):
    return jnp.dot(a, b, preferred_element_type=F32)


def _dot_tn(a, b):
    return lax.dot_general(a, b, (((0,), (0,)), ((), ())), preferred_element_type=F32)


def _dot_nt(a, b):
    return lax.dot_general(a, b, (((1,), (1,)), ((), ())), preferred_element_type=F32)


def _gelu(x):
    return jax.nn.gelu(x, approximate=True)


def _sigmoid(x):
    return 1.0 / (1.0 + jnp.exp(-x))


def _silu(x):
    return x * _sigmoid(x)


def _tile8(v, rows):
    n = v.shape[-1]
    return jnp.broadcast_to(v[None], (rows // SUBLANES, SUBLANES, n)).reshape(rows, n)


def _norm_mod(x, g, sc, sh):
    ms = jnp.mean(x * x, axis=-1, keepdims=True)
    y = x * lax.rsqrt(ms + NORM_EPS) * g
    rows = x.shape[0]
    return y * (1.0 + _tile8(sc, rows)) + _tile8(sh, rows)


def _shift_rows(p, off, rows):
    if off % SUBLANES == 0:
        return p[off:off + rows]
    n = p.shape[0]
    base = (off // SUBLANES) * SUBLANES
    rolled = pltpu.roll(p, (n - (off - base)) % n, axis=0)
    return rolled[base:base + rows]


def _conv_rows(p_ext, cw, rows, rs, hb):
    k_taps = cw.shape[0]
    acc = None
    for k in range(k_taps):
        term = cw[k:k + 1, :] * _shift_rows(p_ext, hb + (k - k_taps // 2) * rs, rows)
        acc = term if acc is None else acc + term
    return acc


def _fill_h(h_ref, x_ref, xb_ref, xa_ref, g_ref, sc_ref, sh_ref, first, last, hb, ha):
    tm = x_ref.shape[0]
    g, sc, sh = g_ref[...], sc_ref[...], sh_ref[...]
    h_ref[hb:hb + tm, :] = _norm_mod(x_ref[...], g, sc, sh).astype(BF16)
    if hb:
        hbv = _norm_mod(xb_ref[...], g, sc, sh)
        h_ref[0:hb, :] = jnp.where(first, 0.0, hbv).astype(BF16)
    if ha:
        hav = _norm_mod(xa_ref[...], g, sc, sh)
        h_ref[hb + tm:hb + tm + ha, :] = jnp.where(last, 0.0, hav).astype(BF16)


def _halo_specs(tm, hb, ha, d, n_rows):
    specs = [pl.BlockSpec((tm, d), lambda i, j: (i, 0))]
    if hb:
        specs.append(pl.BlockSpec((hb, d), lambda i, j: (jnp.maximum(i * (tm // hb) - 1, 0), 0)))
    if ha:
        nblk = n_rows // ha
        specs.append(pl.BlockSpec((ha, d), lambda i, j: (jnp.minimum((i + 1) * (tm // ha), nblk - 1), 0)))
    return specs


def _mod_spec(chunk, d, tiles_per_mod):
    return pl.BlockSpec((None, SUBLANES, d), lambda i, j: (i // tiles_per_mod, 0, chunk))


def _mod_kernel(c_ref, w_ref, b_ref, o_ref):
    a = _silu(c_ref[...]).astype(BF16)
    o_ref[...] = _dot(a, w_ref[...].astype(BF16)) + b_ref[...]


def _modulation(c16, mod_w, mod_b):
    depth, d, n = mod_w.shape
    tn = n // 4
    return pl.pallas_call(
        _mod_kernel,
        grid=(depth, n // tn),
        in_specs=[pl.BlockSpec((16, d), lambda l, j: (0, 0)),
                  pl.BlockSpec((None, d, tn), lambda l, j: (l, 0, j)),
                  pl.BlockSpec((None, 1, tn), lambda l, j: (l, 0, j))],
        out_specs=pl.BlockSpec((None, 16, tn), lambda l, j: (l, 0, j)),
        out_shape=jax.ShapeDtypeStruct((depth, 16, n), F32),
        compiler_params=_cparams("arbitrary", "arbitrary"),
        name="modulation",
    )(c16, mod_w, mod_b.reshape(depth, 1, n))


FFN_TILE = 2 * LANES


def _ffn_kernel(x_ref, xb_ref, xa_ref, g_ref, sh_ref, sc_ref, gt_ref, win_ref, cw_ref, cb_ref, wo_ref, fg_ref, o_ref, h_ref,
                *, rs, tiles_per_seq, final_norm):
    i = pl.program_id(0)
    tm = x_ref.shape[0]
    hb = ha = SUBLANES
    f = wo_ref.shape[0]
    first = i % tiles_per_seq == 0
    last = i % tiles_per_seq == tiles_per_seq - 1
    _fill_h(h_ref, x_ref, xb_ref, xa_ref, g_ref, sc_ref, sh_ref, first, last, hb, ha)
    acc = None
    for j in range(f // FFN_TILE):
        cols = slice(j * FFN_TILE, (j + 1) * FFN_TILE)
        gate = _dot(h_ref[...], win_ref[:, cols])
        up = _dot(h_ref[hb:hb + tm, :], win_ref[:, f + j * FFN_TILE:f + (j + 1) * FFN_TILE])
        gc = _conv_rows(gate, cw_ref[:, cols], tm, rs, hb) + cb_ref[:, cols]
        part = _dot((_gelu(gc) * up).astype(BF16), wo_ref[cols, :])
        acc = part if acc is None else acc + part
    y = x_ref[...] + _tile8(gt_ref[...], tm) * acc
    if final_norm:
        ms = jnp.mean(y * y, axis=-1, keepdims=True)
        y = y * lax.rsqrt(ms + NORM_EPS) * fg_ref[...]
    o_ref[...] = y


def _ffn(x, modt, tiles_per_mod, norm_g, w_in, conv_w, conv_b, w_out, final_g, *, rs, seq_rows, final_norm=False):
    n_rows, d = x.shape
    f = w_out.shape[0]
    tm = _row_tile(seq_rows)
    assert f % FFN_TILE == 0
    hb = ha = SUBLANES
    kern = functools.partial(_ffn_kernel, rs=rs, tiles_per_seq=seq_rows // tm, final_norm=final_norm)
    resident = lambda shape: pl.BlockSpec(shape, lambda i, j: (0, 0), pipeline_mode=pl.Buffered(1))
    in_specs = _halo_specs(tm, hb, ha, d, n_rows) + [
        pl.BlockSpec((1, d), lambda i, j: (0, 0)),
        _mod_spec(3, d, tiles_per_mod), _mod_spec(4, d, tiles_per_mod), _mod_spec(5, d, tiles_per_mod),
        resident(w_in.shape), resident(conv_w.shape), resident((1, f)), resident(w_out.shape),
        pl.BlockSpec((1, d), lambda i, j: (0, 0)),
    ]
    return pl.pallas_call(
        kern,
        grid=(n_rows // tm, 1),
        in_specs=in_specs,
        out_specs=pl.BlockSpec((tm, d), lambda i, j: (i, 0)),
        out_shape=jax.ShapeDtypeStruct((n_rows, d), F32),
        scratch_shapes=[pltpu.VMEM((tm + hb + ha, d), BF16)],
        compiler_params=_cparams("parallel", "arbitrary"),
        name="conv_ffn",
    )(x, x, x, norm_g.reshape(1, d), modt, modt, modt, w_in, conv_w, conv_b.reshape(1, f), w_out, final_g.reshape(1, d))


def _outproj_kernel(*refs, n_pro, prologue, glu):
    pro_refs = refs[:n_pro]
    if glu:
        x_ref, gt_ref, w_ref, b_ref, o_ref = refs[n_pro:]
    else:
        x_ref, gt_ref, w_ref, o_ref = refs[n_pro:]
    d = o_ref.shape[1]
    a = prologue(*pro_refs).astype(BF16)
    y = _dot(a, w_ref[:, 0:d])
    if glu:
        y = (y + b_ref[:, 0:d]) * jax.nn.sigmoid(_dot(a, w_ref[:, d:2 * d]) + b_ref[:, d:2 * d])
    o_ref[...] = x_ref[...] + _tile8(gt_ref[...], o_ref.shape[0]) * y


def _outproj(prologue, pro_args, pro_specs, k_dim, x, modt, tiles_per_mod, w, bias=None, *, tm, name):
    n_rows, d = x.shape
    glu = bias is not None
    kern = functools.partial(_outproj_kernel, n_pro=len(pro_args), prologue=prologue, glu=glu)
    specs = list(pro_specs) + [pl.BlockSpec((tm, d), lambda i, j: (i, 0)),
                               pl.BlockSpec((None, SUBLANES, d), lambda i, j: (i // tiles_per_mod, 0, 2)),
                               pl.BlockSpec(w.shape, lambda i, j: (0, 0))]
    args = list(pro_args) + [x, modt, w]
    if glu:
        specs += [pl.BlockSpec(bias.shape, lambda i, j: (0, 0))]
        args += [bias]
    return pl.pallas_call(
        kern,
        grid=(n_rows // tm, 1),
        in_specs=specs,
        out_specs=pl.BlockSpec((tm, d), lambda i, j: (i, 0)),
        out_shape=jax.ShapeDtypeStruct((n_rows, d), F32),
        compiler_params=_cparams("parallel", "arbitrary"),
        name=name,
    )(*args)


def _row_spec(tm, n):
    return pl.BlockSpec((tm, n), lambda i, j: (i, 0))


def _const_spec(shape):
    nd = len(shape)
    return pl.BlockSpec(shape, lambda i, j: (0,) * nd)


def _s5_weights(lam_re, lam_im, log_step, b_re, b_im, c_re, c_im):
    g, p = lam_re.shape
    gc = b_re.shape[-1]
    lr = jnp.minimum(lam_re.astype(F32), -1e-4)
    li = lam_im.astype(F32)
    step = jnp.exp(log_step.astype(F32))[:, None]
    mag = jnp.exp(lr * step)
    ar, ai = mag * jnp.cos(li * step), mag * jnp.sin(li * step)
    den = lr * lr + li * li
    kr = ((ar - 1.0) * lr + ai * li) / den
    ki = (ai * lr - (ar - 1.0) * li) / den
    br32, bi32 = b_re.astype(F32), b_im.astype(F32)
    br = kr[..., None] * br32 - ki[..., None] * bi32
    bi = kr[..., None] * bi32 + ki[..., None] * br32
    nb = g // S5_GB
    eye = jnp.eye(S5_GB, dtype=F32)

    def pack_b(b):
        b4 = b.reshape(nb, S5_GB, p, gc)
        return jnp.einsum('blpc,lm->blcmp', b4, eye).reshape(nb, S5_GB * gc, S5_GB * p)

    def pack_c(cm):
        c4 = cm.astype(F32).reshape(nb, S5_GB, gc, p)
        return jnp.einsum('blcp,lm->blpmc', c4, eye).reshape(nb, S5_GB * p, S5_GB * gc)

    bm = jnp.concatenate([pack_b(br), pack_b(bi)], axis=-1).astype(BF16)
    cm = jnp.concatenate([pack_c(c_re), -pack_c(c_im)], axis=1).astype(BF16)
    lam = jnp.stack([ar.reshape(nb, S5_GB * p), ai.reshape(nb, S5_GB * p)], axis=1)
    lam = jnp.broadcast_to(lam[:, :, None, :], (nb, 2, SUBLANES, S5_GB * p))
    return bm, cm, lam


def _s5_scan_kernel(x_ref, g_ref, sh_ref, sc_ref, bm_ref, cm_ref, lam_ref, s0_ref, y_ref, sf_ref, st_ref, bu_ref,
                    *, rev, emit):
    i = pl.program_id(0)
    tm = x_ref.shape[0]
    nt = tm // SUBLANES
    nb, kin, ns2 = bm_ref.shape
    ns = ns2 // 2

    @pl.when(i == 0)
    def _():
        st_ref[...] = s0_ref[...]
        if not emit:
            y_ref[...] = jnp.zeros_like(y_ref)

    h = _norm_mod(x_ref[...], g_ref[...], sc_ref[...], sh_ref[...]).astype(BF16)
    for gb in range(nb):
        bu_ref[...] = _dot(h[:, gb * kin:(gb + 1) * kin], bm_ref[gb])
        ar, ai = lam_ref[gb, 0], lam_ref[gb, 1]

        def step(t, carry):
            sr, si = carry
            tt = nt - 1 - t if rev else t
            r0 = pl.multiple_of(tt * SUBLANES, SUBLANES)
            nr = ar * sr - ai * si + bu_ref[pl.ds(r0, SUBLANES), 0:ns]
            ni = ar * si + ai * sr + bu_ref[pl.ds(r0, SUBLANES), ns:ns2]
            if emit:
                bu_ref[pl.ds(r0, SUBLANES), 0:ns] = nr
                bu_ref[pl.ds(r0, SUBLANES), ns:ns2] = ni
            return nr, ni

        sr, si = lax.fori_loop(0, nt, step, (st_ref[gb, 0], st_ref[gb, 1]), unroll=4)
        st_ref[gb, 0] = sr
        st_ref[gb, 1] = si
        if emit:
            y_ref[:, gb * kin:(gb + 1) * kin] = _dot(bu_ref[...].astype(BF16), cm_ref[gb])

    @pl.when(i == pl.num_programs(0) - 1)
    def _():
        sf_ref[...] = st_ref[...]


def _s5_scan(x, modt, norm_g, bm, cm, lam, s0, *, rev, emit=True):
    n_rows, d = x.shape
    tm = _row_tile(n_rows, cap=ROW_TILE_CAP // 2)
    ntile = n_rows // tm
    tile = (lambda i: (ntile - 1 - i, 0)) if rev else (lambda i: (i, 0))
    mod = lambda k: pl.BlockSpec((None, SUBLANES, d), lambda i: (0, 0, k))
    full = lambda a: pl.BlockSpec(a.shape, lambda i: (0,) * a.ndim)
    y, sf = pl.pallas_call(
        functools.partial(_s5_scan_kernel, rev=rev, emit=emit),
        grid=(ntile,),
        in_specs=[pl.BlockSpec((tm, d), tile), pl.BlockSpec((1, d), lambda i: (0, 0)), mod(0), mod(1),
                  full(bm), full(cm), full(lam), full(s0)],
        out_specs=[pl.BlockSpec((tm, d), tile if emit else (lambda i: (0, 0))), full(s0)],
        out_shape=[jax.ShapeDtypeStruct((n_rows if emit else tm, d), F32), jax.ShapeDtypeStruct(s0.shape, F32)],
        scratch_shapes=[pltpu.VMEM(s0.shape, F32), pltpu.VMEM((tm, bm.shape[-1]), F32)],
        compiler_params=_cparams("arbitrary"),
        name="s5_scan_bwd" if rev else "s5_scan_fwd",
    )(x, norm_g.reshape(1, d), modt, modt, bm, cm, lam, s0)
    return (y if emit else None), sf


def _s5_out_prologue(x_ref, yf_ref, yb_ref, g_ref, sh_ref, sc_ref, dk_ref):
    u = _norm_mod(x_ref[...], g_ref[...], sc_ref[...], sh_ref[...])
    return _gelu(yf_ref[...] + yb_ref[...] + dk_ref[...] * u)


def _s5_mixer(x, x_ctx, modt, modt_ctx, norm_g, prm, *, ctx_out):
    d = x.shape[-1]
    w = [_s5_weights(prm['lam_re'][dr], prm['lam_im'][dr], prm['log_step'][dr], prm['b_re'][dr], prm['b_im'][dr],
                     prm['c_re'][dr], prm['c_im'][dr]) for dr in range(2)]
    zero = jnp.zeros(w[0][2].shape, F32)
    ys, ys_ctx = [], []
    for dr in range(2):
        bm, cm, lam = w[dr]
        yc, sc = _s5_scan(x_ctx, modt_ctx, norm_g, bm, cm, lam, zero, rev=dr == 1, emit=ctx_out)
        yl, _ = _s5_scan(x, modt, norm_g, bm, cm, lam, sc, rev=dr == 1)
        ys.append(yl)
        ys_ctx.append(yc)
    w_glu = prm['w_glu'].astype(BF16)
    b_glu = prm['b_glu'].reshape(1, -1)

    def out(xa, ya, mt):
        tm = _row_tile(xa.shape[0], cap=ROW_TILE_CAP // 2)
        pro_args = [xa, ya[0], ya[1], norm_g.reshape(1, d), mt, mt, prm['d'].reshape(1, d)]
        pro_specs = [_row_spec(tm, d), _row_spec(tm, d), _row_spec(tm, d), _const_spec((1, d)),
                     _mod_spec(0, d, 10 ** 9), _mod_spec(1, d, 10 ** 9), _const_spec((1, d))]
        return _outproj(_s5_out_prologue, pro_args, pro_specs, d, xa, mt, 10 ** 9, w_glu, b_glu, tm=tm, name="s5_out")

    return out(x, ys, modt), (out(x_ctx, ys_ctx, modt_ctx) if ctx_out else None)


def _lru_in_kernel(x_ref, xb_ref, xa_ref, g_ref, sh_ref, sc_ref, wr_ref, wy_ref, cw_ref, cb_ref, xc_ref, yg_ref, h_ref,
                   *, rs, hb, ha):
    i, j = pl.program_id(0), pl.program_id(1)
    tm = x_ref.shape[0]

    @pl.when(j == 0)
    def _():
        _fill_h(h_ref, x_ref, xb_ref, xa_ref, g_ref, sc_ref, sh_ref, i == 0, i == pl.num_programs(0) - 1, hb, ha)

    p = _dot(h_ref[...], wr_ref[...])
    xc_ref[...] = _conv_rows(p, cw_ref[...], tm, rs, hb) + cb_ref[...]
    yg_ref[...] = _gelu(_dot(h_ref[hb:hb + tm, :], wy_ref[...]))


def _lru_in(x, modt, norm_g, w_in, conv_w, conv_b):
    n_rows, d = x.shape
    wd = conv_w.shape[-1]
    rs = SUBLANES
    hb, ha = 2 * rs, rs
    tm = _row_tile(n_rows)
    tn = 4 * LANES
    nj = wd // tn
    specs = _halo_specs(tm, hb, ha, d, n_rows) + [
        _const_spec((1, d)), _mod_spec(0, d, 10 ** 9), _mod_spec(1, d, 10 ** 9),
        pl.BlockSpec((d, tn), lambda i, j: (0, j)), pl.BlockSpec((d, tn), lambda i, j: (0, j + nj)),
        pl.BlockSpec((conv_w.shape[0], tn), lambda i, j: (0, j)), pl.BlockSpec((1, tn), lambda i, j: (0, j))]
    out_spec = pl.BlockSpec((tm, tn), lambda i, j: (i, j))
    return pl.pallas_call(
        functools.partial(_lru_in_kernel, rs=rs, hb=hb, ha=ha),
        grid=(n_rows // tm, nj),
        in_specs=specs,
        out_specs=[out_spec, out_spec],
        out_shape=[jax.ShapeDtypeStruct((n_rows, wd), F32)] * 2,
        scratch_shapes=[pltpu.VMEM((tm + hb + ha, d), BF16)],
        compiler_params=_cparams("parallel", "arbitrary"),
        name="lru_in",
    )(x, x, x, norm_g.reshape(1, d), modt, modt, w_in, w_in, conv_w, conv_b.reshape(1, wd))


def _lru_scan_kernel(xc_ref, wa_ref, wx_ref, ba_ref, bx_ref, nsp_ref, h0_ref, hs_ref, hf_ref, st_ref, a_ref, b_ref,
                     *, rev, emit):
    i = pl.program_id(0)
    tm = xc_ref.shape[0]
    nt = tm // SUBLANES
    nblk, bw, _ = wa_ref.shape

    @pl.when(i == 0)
    def _():
        st_ref[...] = h0_ref[...]
        if not emit:
            hs_ref[...] = jnp.zeros_like(hs_ref)

    for k in range(nblk):
        cols = slice(k * bw, (k + 1) * bw)
        xc = xc_ref[:, cols]
        xcb = xc.astype(BF16)
        r = jax.nn.sigmoid(_dot(xcb, wa_ref[k]) + ba_ref[:, cols])
        gi = jax.nn.sigmoid(_dot(xcb, wx_ref[k]) + bx_ref[:, cols])
        log_a = nsp_ref[:, cols] * r
        a = jnp.exp(log_a)
        a_ref[:, cols] = a
        b_ref[:, cols] = jnp.sqrt(1.0 - a * a) * (gi * xc)

    def step(t, h):
        tt = nt - 1 - t if rev else t
        r0 = pl.multiple_of(tt * SUBLANES, SUBLANES)
        h = a_ref[pl.ds(r0, SUBLANES), :] * h + b_ref[pl.ds(r0, SUBLANES), :]
        if emit:
            hs_ref[pl.ds(r0, SUBLANES), :] = h
        return h

    st_ref[...] = lax.fori_loop(0, nt, step, st_ref[...], unroll=8)

    @pl.when(i == pl.num_programs(0) - 1)
    def _():
        hf_ref[...] = st_ref[...]


def _lru_scan(xc, wa, wx, ba, bx, nsp, h0, *, rev, emit=True):
    n_rows, wd = xc.shape
    tm = _row_tile(n_rows)
    ntile = n_rows // tm
    tile = (lambda i: (ntile - 1 - i, 0)) if rev else (lambda i: (i, 0))
    full = lambda a: pl.BlockSpec(a.shape, lambda i: (0,) * a.ndim)
    hs, hf = pl.pallas_call(
        functools.partial(_lru_scan_kernel, rev=rev, emit=emit),
        grid=(ntile,),
        in_specs=[pl.BlockSpec((tm, wd), tile), full(wa), full(wx), full(ba), full(bx), full(nsp), full(h0)],
        out_specs=[pl.BlockSpec((tm, wd), tile if emit else (lambda i: (0, 0))), full(h0)],
        out_shape=[jax.ShapeDtypeStruct((n_rows if emit else tm, wd), F32), jax.ShapeDtypeStruct(h0.shape, F32)],
        scratch_shapes=[pltpu.VMEM(h0.shape, F32), pltpu.VMEM((tm, wd), F32), pltpu.VMEM((tm, wd), F32)],
        compiler_params=_cparams("arbitrary"),
        name="lru_scan_bwd" if rev else "lru_scan_fwd",
    )(xc, wa, wx, ba, bx, nsp, h0)
    return (hs if emit else None), hf


def _lru_out_prologue(yg_ref, hf_ref, hb_ref):
    return yg_ref[...] * (hf_ref[...] + hb_ref[...])


def _lru_mixer(x, x_ctx, modt, modt_ctx, norm_g, prm, *, ctx_out):
    d = x.shape[-1]
    w_in = prm['w_in'].astype(BF16)
    wd = prm['conv_w'].shape[-1]
    xc_l, yg_l = _lru_in(x, modt, norm_g, w_in, prm['conv_w'], prm['conv_b'])
    xc_c, yg_c = _lru_in(x_ctx, modt_ctx, norm_g, w_in, prm['conv_w'], prm['conv_b'])
    zero = jnp.zeros((SUBLANES, wd), F32)
    hs_l, hs_c = [], []
    for dr in range(2):
        wa, wx = prm['w_a'][dr].astype(BF16), prm['w_x'][dr].astype(BF16)
        ba, bx = prm['b_a'][dr].reshape(1, wd), prm['b_x'][dr].reshape(1, wd)
        nsp = (-LRU_C * jax.nn.softplus(-prm['lam'][dr].astype(F32))).reshape(1, wd)
        hc, hfin = _lru_scan(xc_c, wa, wx, ba, bx, nsp, zero, rev=dr == 1, emit=ctx_out)
        hl, _ = _lru_scan(xc_l, wa, wx, ba, bx, nsp, hfin, rev=dr == 1)
        hs_l.append(hl)
        hs_c.append(hc)
    w_out = prm['w_out'].astype(BF16)

    def out(xa, yg, hs, mt):
        tm = _row_tile(xa.shape[0], cap=ROW_TILE_CAP // 2)
        specs = [_row_spec(tm, wd)] * 3
        return _outproj(_lru_out_prologue, [yg, hs[0], hs[1]], specs, wd, xa, mt, 10 ** 9, w_out, tm=tm, name="lru_out")

    return out(x, yg_l, hs_l, modt), (out(x_ctx, yg_c, hs_c, modt_ctx) if ctx_out else None)


def _rope_tables(length):
    rows = length // GRID_W
    t = jnp.arange(length, dtype=jnp.int32)
    row = (t // GRID_W).astype(F32) - (rows - 1) / 2.0
    col = (t % GRID_W).astype(F32) - (GRID_W - 1) / 2.0
    quarter = RET_DK // 4
    inv_freq = ROPE_BASE ** (-jnp.arange(quarter, dtype=F32) / quarter)
    ar, ac = row[:, None] * inv_freq[None, :], col[:, None] * inv_freq[None, :]
    cos = jnp.concatenate([jnp.cos(ar), jnp.cos(ar), jnp.cos(ac), jnp.cos(ac)], axis=-1)
    sin = jnp.concatenate([-jnp.sin(ar), jnp.sin(ar), -jnp.sin(ac), jnp.sin(ac)], axis=-1)
    return cos, sin


def _ret_in_kernel(x_ref, g_ref, sh_ref, sc_ref, w_ref, cos_ref, sin_ref, p_ref, h_ref, *, rotate, n_k, n_v):
    j = pl.program_id(1)
    tn = w_ref.shape[1]

    @pl.when(j == 0)
    def _():
        h_ref[...] = _norm_mod(x_ref[...], g_ref[...], sc_ref[...], sh_ref[...]).astype(BF16)

    is_k = j < n_k
    is_q = (j >= n_k + n_v) & (j < 2 * n_k + n_v)

    @pl.when(is_k | is_q)
    def _():
        scale = jnp.where(is_q, RET_DK ** -0.5, 1.0)
        for hd in range(tn // RET_DK):
            ph = _dot(h_ref[...], w_ref[:, hd * RET_DK:(hd + 1) * RET_DK])
            for s in range(RET_DK // LANES):
                tcols = slice(s * LANES, (s + 1) * LANES)
                ps = ph[:, tcols]
                if rotate:
                    rot = pltpu.roll(ps, LANES // 2, axis=1)
                    ps = ps * cos_ref[:, tcols] + rot * sin_ref[:, tcols]
                p_ref[:, hd * RET_DK + s * LANES:hd * RET_DK + (s + 1) * LANES] = (ps * scale).astype(p_ref.dtype)

    @pl.when(jnp.logical_not(is_k | is_q))
    def _():
        p_ref[...] = _dot(h_ref[...], w_ref[...]).astype(p_ref.dtype)


def _ret_in(x, modt, tiles_per_mod, norm_g, w_in, seq_rows, *, rotate):
    n_rows, d = x.shape
    n_out = w_in.shape[1]
    tm = _row_tile(seq_rows)
    tn = RET_HEADS * RET_DK
    tiles_per_seq = seq_rows // tm
    cos, sin = _rope_tables(seq_rows)
    hk, hv = RET_HEADS * RET_DK, RET_HEADS * RET_DV
    tab_spec = pl.BlockSpec((tm, RET_DK), lambda i, j: (i % tiles_per_seq, 0))
    return pl.pallas_call(
        functools.partial(_ret_in_kernel, rotate=rotate, n_k=hk // tn, n_v=hv // tn),
        grid=(n_rows // tm, n_out // tn),
        in_specs=[_row_spec(tm, d), _const_spec((1, d)), _mod_spec(0, d, tiles_per_mod), _mod_spec(1, d, tiles_per_mod),
                  pl.BlockSpec((d, tn), lambda i, j: (0, j)), tab_spec, tab_spec],
        out_specs=pl.BlockSpec((tm, tn), lambda i, j: (i, j)),
        out_shape=jax.ShapeDtypeStruct((n_rows, n_out), BF16),
        scratch_shapes=[pltpu.VMEM((tm, d), BF16)],
        compiler_params=_cparams("parallel", "arbitrary"),
        name="ret_in",
    )(x, norm_g.reshape(1, d), modt, modt, w_in, cos, sin)


def _ret_tables(c):
    log_g = np.log1p(-np.power(2.0, -5.0 - np.arange(RET_HEADS, dtype=np.float64)))
    idx = np.arange(c, dtype=np.float64)
    diff = idx[:, None] - idx[None, :]
    fwd = np.where(diff >= 0, np.exp(np.where(diff >= 0, diff, 0.0)[None] * log_g[:, None, None]), 0.0)
    bwd = np.where(diff < 0, np.exp(np.where(diff < 0, -diff, 0.0)[None] * log_g[:, None, None]), 0.0)
    xi_f = np.exp((idx + 1.0)[None, :] * log_g[:, None])
    zeta_f = np.exp((c - 1.0 - idx)[None, :] * log_g[:, None])
    xi_b = np.exp((c - idx)[None, :] * log_g[:, None])
    zeta_b = np.exp(idx[None, :] * log_g[:, None])
    dmask = np.stack([fwd, bwd]).astype(np.float32)
    xi = np.stack([xi_f, xi_b])[..., None].astype(np.float32)
    zeta = np.stack([zeta_f, zeta_b])[..., None].astype(np.float32)
    g_blk = [float(v) for v in np.exp(c * log_g).astype(np.float32)]
    return jnp.asarray(dmask), jnp.asarray(xi), jnp.asarray(zeta), g_blk


def _ret_scan_kernel(kf_ref, v0f_ref, v1f_ref, qf_ref, kb_ref, v0b_ref, v1b_ref, qb_ref, dm_ref, xi_ref, zt_ref, r0_ref,
                     of_ref, ob_ref, rf_ref, r_ref, *, g_blk, emit):
    c = pl.program_id(1)
    dir_refs = ((kf_ref, v0f_ref, v1f_ref, qf_ref, of_ref), (kb_ref, v0b_ref, v1b_ref, qb_ref, ob_ref))

    @pl.when(c == 0)
    def _():
        r_ref[...] = r0_ref[...]
        if not emit:
            of_ref[...] = jnp.zeros_like(of_ref)
            ob_ref[...] = jnp.zeros_like(ob_ref)

    hv_half = v0f_ref.shape[1] // RET_DV
    r_olds = [[r_ref[d, h] for h in range(RET_HEADS)] for d in range(2)]
    r_news = [[None] * RET_HEADS for _ in range(2)]
    for h in range(RET_HEADS):
        for d in range(2):
            k_ref, v0_ref, v1_ref, q_ref, o_ref = dir_refs[d]
            kh = k_ref[:, h * RET_DK:(h + 1) * RET_DK]
            v_ref = v0_ref if h < hv_half else v1_ref
            hh = h % hv_half
            vh = v_ref[:, hh * RET_DV:(hh + 1) * RET_DV]
            r_old = r_olds[d][h]
            if emit:
                qh = q_ref[:, h * RET_DK:(h + 1) * RET_DK]
                s = _dot_nt(qh, kh) * dm_ref[d, h]
                o = _dot(s.astype(BF16), vh) + _dot(qh, r_old.astype(BF16)) * xi_ref[d, h]
                o_ref[:, h * RET_DV:(h + 1) * RET_DV] = o.astype(o_ref.dtype)
            r_news[d][h] = g_blk[h] * r_old + _dot_tn((kh * zt_ref[d, h]).astype(BF16), vh)
    for d in range(2):
        for h in range(RET_HEADS):
            r_ref[d, h] = r_news[d][h]

    @pl.when(c == pl.num_programs(1) - 1)
    def _():
        rf_ref[...] = r_ref[...]


def _ret_scan(p, r0, batch, seq_rows, *, emit=True):
    c = min(RET_CHUNK, seq_rows)
    nc = seq_rows // c
    hk, hv = RET_HEADS * RET_DK, RET_HEADS * RET_DV
    dmask, xi, zeta, g_blk = _ret_tables(c)
    n_rows = p.shape[0]
    rows = (lambda b, cc: b * nc + cc, lambda b, cc: b * nc + nc - 1 - cc)
    kcols = lambda rw: [pl.BlockSpec((c, hk), lambda b, cc, blk=blk: (rw(b, cc), blk)) for blk in range(4)]
    full = lambda a: pl.BlockSpec(a.shape, lambda b, cc: (0,) * a.ndim)
    st_spec = pl.BlockSpec((2, None, RET_HEADS, RET_DK, RET_DV), lambda b, cc: (0, b, 0, 0, 0))
    o_spec = lambda rw: pl.BlockSpec((c, hv), (lambda b, cc: (rw(b, cc), 0)) if emit else (lambda b, cc: (0, 0)))
    o_shape = jax.ShapeDtypeStruct((n_rows if emit else c, hv), BF16)
    assert hv == 2 * hk
    of, ob, rf = pl.pallas_call(
        functools.partial(_ret_scan_kernel, g_blk=g_blk, emit=emit),
        grid=(batch, nc),
        in_specs=kcols(rows[0]) + kcols(rows[1]) + [full(dmask), full(xi), full(zeta), st_spec],
        out_specs=[o_spec(rows[0]), o_spec(rows[1]), st_spec],
        out_shape=[o_shape, o_shape, jax.ShapeDtypeStruct(r0.shape, F32)],
        scratch_shapes=[pltpu.VMEM((2, RET_HEADS, RET_DK, RET_DV), F32)],
        compiler_params=_cparams("arbitrary", "arbitrary"),
        name="ret_scan",
    )(*([p] * 8), dmask, xi, zeta, r0)
    return of, ob, rf


def _ret_out_prologue(of_ref, ob_ref, gate_ref, ng_ref):
    o = of_ref[...].astype(F32) + ob_ref[...].astype(F32)
    parts = []
    for h in range(RET_HEADS):
        oh = o[:, h * RET_DV:(h + 1) * RET_DV]
        parts.append(oh * lax.rsqrt(jnp.mean(oh * oh, axis=-1, keepdims=True) + NORM_EPS))
    on = (jnp.concatenate(parts, axis=-1) * ng_ref[...])
    return jax.nn.silu(gate_ref[...].astype(F32)) * on


def _ret_mixer(x, x_ctx, modt, modt_ctx, norm_g, prm, batch, seq, seq_ctx, *, ctx_out):
    d = x.shape[-1]
    w_in = prm['w_in'].astype(BF16)
    hk, hv = RET_HEADS * RET_DK, RET_HEADS * RET_DV
    tiles_per_mod = seq // _row_tile(seq)
    p_l = _ret_in(x, modt, tiles_per_mod, norm_g, w_in, seq, rotate=True)
    p_c = _ret_in(x_ctx, modt_ctx, 10 ** 9, norm_g, w_in, seq_ctx, rotate=False)
    r0 = jnp.zeros((2, batch, RET_HEADS, RET_DK, RET_DV), F32)
    of_c, ob_c, r_c = _ret_scan(p_c, r0, batch, seq_ctx, emit=ctx_out)
    of_l, ob_l, _ = _ret_scan(p_l, r_c, batch, seq)
    w_out = prm['w_out'].astype(BF16)
    ng = prm['norm_g'].reshape(1, hv)

    def out(xa, pa, of, ob, mt, tpm, seq_rows):
        tm = _row_tile(seq_rows, cap=ROW_TILE_CAP // 2)
        gcol = (2 * hk + hv) // hv
        specs = [_row_spec(tm, hv), _row_spec(tm, hv), pl.BlockSpec((tm, hv), lambda i, j: (i, gcol)), _const_spec((1, hv))]
        return _outproj(_ret_out_prologue, [of, ob, pa, ng], specs, hv, xa, mt, tpm * (_row_tile(seq_rows) // tm), w_out,
                        tm=tm, name="ret_out")

    y_l = out(x, p_l, of_l, ob_l, modt, tiles_per_mod, seq)
    y_c = out(x_ctx, p_c, of_c, ob_c, modt_ctx, 10 ** 9, seq_ctx) if ctx_out else None
    return y_l, y_c


GDN_IN_TILE = 8 * LANES


def _gdn_in_kernel(x_ref, xb_ref, xa_ref, g_ref, sh_ref, sc_ref, w_ref, cw_ref, wba_ref, o_ref, ba_ref, h_ref,
                   *, tiles_per_seq, n_kq, n_conv):
    i, j = pl.program_id(0), pl.program_id(1)
    tm = x_ref.shape[0]
    hb = ha = SUBLANES
    heads = o_ref.shape[0]

    @pl.when(j == 0)
    def _():
        first = i % tiles_per_seq == 0
        last = i % tiles_per_seq == tiles_per_seq - 1
        _fill_h(h_ref, x_ref, xb_ref, xa_ref, g_ref, sc_ref, sh_ref, first, last, hb, ha)

    pair = 2 * LANES

    def conv_silu(s2):
        p = _dot(h_ref[...], w_ref[:, s2 * pair:(s2 + 1) * pair])
        return jax.nn.silu(_conv_rows(p, cw_ref[:, s2 * pair:(s2 + 1) * pair], tm, 1, hb))

    @pl.when(j < n_kq)
    def _():
        scale = jnp.where(j >= n_kq // 2, GDN_DK ** -0.5, 1.0)
        for s2 in range(heads // 2):
            a = conv_silu(s2)
            for e in range(2):
                ah = a[:, e * LANES:(e + 1) * LANES]
                o_ref[2 * s2 + e] = (ah * (lax.rsqrt(jnp.sum(ah * ah, axis=-1, keepdims=True) + 1e-6) * scale)
                                     ).astype(o_ref.dtype)

    @pl.when((j >= n_kq) & (j < n_conv))
    def _():
        for s2 in range(heads // 2):
            a = conv_silu(s2)
            for e in range(2):
                o_ref[2 * s2 + e] = a[:, e * LANES:(e + 1) * LANES].astype(o_ref.dtype)

    @pl.when(j >= n_conv)
    def _():
        for s2 in range(heads // 2):
            p = _dot(h_ref[hb:hb + tm, :], w_ref[:, s2 * pair:(s2 + 1) * pair])
            for e in range(2):
                o_ref[2 * s2 + e] = p[:, e * LANES:(e + 1) * LANES].astype(o_ref.dtype)

    @pl.when(j == pl.num_programs(1) - 1)
    def _():
        ba_ref[...] = _dot(h_ref[hb:hb + tm, :], wba_ref[...])


def _gdn_in(x, modt, tiles_per_mod, norm_g, w_all, cw_all, w_ba, seq_rows):
    n_rows, d = x.shape
    n_out = w_all.shape[1]
    tm = _row_tile(seq_rows)
    tn = GDN_IN_TILE
    hb = ha = SUBLANES
    heads = tn // LANES
    n_kq = 2 * GDN_K_HEADS * GDN_DK // tn
    n_conv = n_kq + GDN_V_HEADS * GDN_DV // tn
    kern = functools.partial(_gdn_in_kernel, tiles_per_seq=seq_rows // tm, n_kq=n_kq, n_conv=n_conv)
    specs = _halo_specs(tm, hb, ha, d, n_rows) + [
        _const_spec((1, d)), _mod_spec(0, d, tiles_per_mod), _mod_spec(1, d, tiles_per_mod),
        pl.BlockSpec((d, tn), lambda i, j: (0, j)), pl.BlockSpec((cw_all.shape[0], tn), lambda i, j: (0, j)),
        _const_spec(w_ba.shape)]
    return pl.pallas_call(
        kern,
        grid=(n_rows // tm, n_out // tn),
        in_specs=specs,
        out_specs=[pl.BlockSpec((heads, tm, LANES), lambda i, j: (j, i, 0)), pl.BlockSpec((tm, LANES), lambda i, j: (i, 0))],
        out_shape=[jax.ShapeDtypeStruct((n_out // LANES, n_rows, LANES), BF16), jax.ShapeDtypeStruct((n_rows, LANES), F32)],
        scratch_shapes=[pltpu.VMEM((tm + hb + ha, d), BF16)],
        compiler_params=_cparams("parallel", "arbitrary"),
        name="gdn_in",
    )(x, x, x, norm_g.reshape(1, d), modt, modt, w_all, cw_all, w_ba)


def _split3(a):
    hi = a.astype(BF16)
    lo = (a - hi.astype(F32)).astype(BF16)
    return hi, lo


def _dot3s(a, b):
    ah, al = a
    bh, bl = b
    m = ah.shape[0]
    both = _dot(jnp.concatenate([ah, al], axis=0), bh)
    return both[:m] + (both[m:] + _dot(ah, bl))


def _cumsum_rows(x, rev):
    n = x.shape[0]
    row = lax.broadcasted_iota(jnp.int32, x.shape, 0)
    shift = 1
    while shift < n:
        if rev:
            x = x + jnp.where(row < n - shift, pltpu.roll(x, n - shift, axis=0), 0.0)
        else:
            x = x + jnp.where(row >= shift, pltpu.roll(x, shift, axis=0), 0.0)
        shift *= 2
    return x


GDN_GROUP = 4


def _gdn_expanders(dr, c):
    nv, gs = GDN_V_HEADS, GDN_GROUP
    lane = np.arange(LANES)[:, None]

    def build(first_lane, width):
        col_head = np.arange(gs * width)[None, :] // width
        return np.stack([(lane == first_lane + gs * g + col_head) for g in range(nv // gs)]).astype(np.float32)

    ig, ib = dr * 2 * nv + nv, dr * 2 * nv
    return [jnp.asarray(a, dtype=BF16) for a in (build(ig, c), build(ib, c), build(ig, GDN_DV), build(ib, GDN_DV))]


def _gdn_scan_kernel(kf_ref, qf_ref, vf_ref, baf_ref, kb_ref, qb_ref, vb_ref, bab_ref, an_ref, dtb_ref, eg_ref, eb_ref,
                     eg5_ref, eb5_ref, s0_ref, of_ref, ob_ref, sf_ref, s_ref, *, emit):
    cidx = pl.program_id(1)
    c = kf_ref.shape[1]
    rep = GDN_V_HEADS // GDN_K_HEADS
    dir_refs = ((kf_ref, qf_ref, vf_ref, baf_ref, of_ref), (kb_ref, qb_ref, vb_ref, bab_ref, ob_ref))

    @pl.when(cidx == 0)
    def _():
        s_ref[...] = s0_ref[...]
        if not emit:
            of_ref[...] = jnp.zeros_like(of_ref)
            ob_ref[...] = jnp.zeros_like(ob_ref)

    def expand(split, e):
        r = _dot(split, e)
        return r[:c] + r[c:]

    gs = GDN_GROUP
    wide = gs * c
    ii = lax.broadcasted_iota(jnp.int32, (c, wide), 0)
    jj = lax.broadcasted_iota(jnp.int32, (c, wide), 1) % c
    br = lax.broadcasted_iota(jnp.int32, (wide, wide), 0) // c
    bc = lax.broadcasted_iota(jnp.int32, (wide, wide), 1) // c
    blockmask = (br == bc).astype(F32).astype(BF16)
    nk, nv = GDN_K_HEADS, GDN_V_HEADS
    zc = jnp.zeros((c, GDN_DK), BF16)
    zs = jnp.zeros((GDN_DK, GDN_DV), BF16)
    zr = jnp.zeros((c, GDN_DV), BF16)

    gc_split, bt_split, incl, strict, eye, last = [], [], [], [], [], []
    for d in range(2):
        rev = d == 1
        ba = dir_refs[d][3][...]
        z = ba + dtb_ref[d]
        sp = jnp.maximum(z, 0.0) + jnp.log(1.0 + jnp.exp(-jnp.abs(z)))
        gc = _cumsum_rows(an_ref[d] * sp, rev)
        gc_split.append(jnp.concatenate(_split3(gc), axis=0))
        bt_split.append(jnp.concatenate(_split3(jax.nn.sigmoid(ba)), axis=0))
        diff = (jj - ii) if rev else (ii - jj)
        incl.append(diff >= 0)
        strict.append(diff > 0)
        eye.append((diff == 0).astype(F32))
        last.append(0 if rev else c - 1)

    def bdiag(blocks, zero):
        n = len(blocks)
        return jnp.concatenate([jnp.concatenate([blocks[r] if r == q else zero for q in range(n)], axis=1)
                                for r in range(n)], axis=0)

    def bd_tile(a16):
        return jnp.concatenate([a16] * gs, axis=0) * blockmask

    def dot3_bd(lhs, rhs_hi, rhs_lo):
        lh, ll = _split3(lhs)
        m = lhs.shape[0]
        both = _dot(jnp.concatenate([lh, ll], axis=0), rhs_hi)
        return both[:m] + (both[m:] + _dot(lh, rhs_lo))

    pieces = lambda a: [a[:, h * GDN_DV:(h + 1) * GDN_DV] for h in range(gs)]
    chains = [(d, g) for g in range(nv // gs) for d in range(2)]
    ks = [[dir_refs[d][0][h].astype(F32) for h in range(nk)] for d in range(2)]
    qs = [[dir_refs[d][1][h].astype(F32) for h in range(nk)] for d in range(2)] if emit else None
    vs = [[dir_refs[d][2][h].astype(F32) for h in range(nv)] for d in range(2)]
    s_olds = [[s_ref[d, h] for h in range(nv)] for d in range(2)]

    xs, ts, attns = [], [], []
    for d, g in chains:
        k0, k1 = ks[d][2 * g], ks[d][2 * g + 1]
        k016, k116 = k0.astype(BF16), k1.astype(BF16)
        rhs_t = jnp.concatenate([jnp.concatenate([k016, zc], axis=1)] * rep
                                + [jnp.concatenate([zc, k116], axis=1)] * rep, axis=0)
        if emit:
            lhs = jnp.concatenate([jnp.concatenate([k0, qs[d][2 * g]], axis=0),
                                   jnp.concatenate([k1, qs[d][2 * g + 1]], axis=0)], axis=1).astype(BF16)
        else:
            lhs = jnp.concatenate([k016, k116], axis=1)
        gram = _dot_nt(lhs, rhs_t)
        gcm = expand(gc_split[d], eg_ref[d, g])
        grow = jnp.sum(gcm * eye[d], axis=0, keepdims=True)
        dec = jnp.where(incl[d], jnp.exp(jnp.where(incl[d], gcm - grow, 0.0)), 0.0)
        btm = expand(bt_split[d], eb_ref[d, g])
        x = jnp.where(strict[d], -(gram[:c] * btm) * dec, 0.0)
        xs.append(x)
        ts.append(eye[d] + x)
        attns.append(gram[c:] * dec if emit else None)

    splits = [_split3(x) for x in xs]
    xs = [dot3_bd(x, bd_tile(xh), bd_tile(xl)) for x, (xh, xl) in zip(xs, splits)]
    nlev = int(math.log2(c)) - 1
    for lvl in range(1, nlev + 1):
        splits = [_split3(x) for x in xs]
        if lvl < nlev:
            ps = [dot3_bd(jnp.concatenate([t, x], axis=0), bd_tile(xh), bd_tile(xl))
                  for t, x, (xh, xl) in zip(ts, xs, splits)]
            ts = [t + p[:c] for t, p in zip(ts, ps)]
            xs = [p[c:] for p in ps]
        else:
            ts = [t + dot3_bd(t, bd_tile(xh), bd_tile(xl)) for t, (xh, xl) in zip(ts, splits)]

    gc5s, bt5s, kcats, p1s = [], [], [], []
    for d, g in chains:
        k0, k1 = ks[d][2 * g], ks[d][2 * g + 1]
        gc5 = expand(gc_split[d], eg5_ref[d, g])
        bt5 = expand(bt_split[d], eb5_ref[d, g])
        eg5 = jnp.exp(gc5)
        kcat = jnp.concatenate([k0] * rep + [k1] * rep, axis=1)
        kbe = kcat * bt5 * eg5
        s16 = [s_olds[d][gs * g + h].astype(BF16) for h in range(gs)]
        if emit:
            qe = jnp.concatenate([qs[d][2 * g]] * rep + [qs[d][2 * g + 1]] * rep, axis=1) * eg5
            top = jnp.concatenate([kbe, qe], axis=0).astype(BF16)
        else:
            top = kbe.astype(BF16)
        p1s.append(jnp.concatenate([_dot(top[:, 2 * pr * GDN_DK:(2 * pr + 2) * GDN_DK], bdiag(s16[2 * pr:2 * pr + 2], zs))
                                    for pr in range(gs // 2)], axis=1))
        gc5s.append(gc5)
        bt5s.append(bt5)
        kcats.append(kcat)

    bdvs = []
    for n, (d, g) in enumerate(chains):
        vcat = jnp.concatenate(vs[d][gs * g:gs * (g + 1)], axis=1)
        rh, rl = _split3(vcat * bt5s[n] - p1s[n][:c])
        vn16 = dot3_bd(ts[n], bdiag(pieces(rh), zr), bdiag(pieces(rl), zr)).astype(BF16)
        bdvs.append(bdiag(pieces(vn16), zr))

    outs = [[None] * nv for _ in range(2)]
    s_news = [[None] * nv for _ in range(2)]
    for n, (d, g) in enumerate(chains):
        gl5 = gc5s[n][last[d]:last[d] + 1, :]
        if emit:
            o = pieces(p1s[n][c:] + _dot(attns[n].astype(BF16), bdvs[n]))
        kd = (kcats[n] * jnp.exp(gl5 - gc5s[n])).astype(BF16)
        sn = _dot_tn(jnp.concatenate(pieces(kd), axis=0), bdvs[n])
        egl = jnp.exp(gl5)
        for h in range(gs):
            cols = slice(h * GDN_DV, (h + 1) * GDN_DV)
            s_news[d][gs * g + h] = s_olds[d][gs * g + h] * egl[:, cols] + sn[:, cols]
            if emit:
                outs[d][gs * g + h] = o[h]
    for d in range(2):
        for hv in range(nv):
            if emit:
                dir_refs[d][4][hv] = outs[d][hv].astype(dir_refs[d][4].dtype)
            s_ref[d, hv] = s_news[d][hv]

    @pl.when(cidx == pl.num_programs(1) - 1)
    def _():
        sf_ref[...] = s_ref[...]


def _gdn_scan(kqvz, ba, an, dtb, s0, batch, seq_rows, *, emit=True):
    c = min(GDN_CHUNK, seq_rows)
    nc = seq_rows // c
    n_rows = kqvz.shape[1]
    rows = (lambda b, cc: b * nc + cc, lambda b, cc: b * nc + nc - 1 - cc)
    kh, vh = GDN_K_HEADS, GDN_V_HEADS
    st_spec = pl.BlockSpec((2, None, vh, GDN_DK, GDN_DV), lambda b, cc: (0, b, 0, 0, 0))
    expanders = [jnp.stack(pair) for pair in zip(_gdn_expanders(0, c), _gdn_expanders(1, c))]
    full = lambda a: pl.BlockSpec(a.shape, lambda b, cc: (0,) * a.ndim)

    def dir_specs(rw):
        return [pl.BlockSpec((kh, c, LANES), lambda b, cc: (0, rw(b, cc), 0)),
                pl.BlockSpec((kh, c, LANES), lambda b, cc: (1, rw(b, cc), 0)),
                pl.BlockSpec((vh, c, LANES), lambda b, cc: (1, rw(b, cc), 0)),
                pl.BlockSpec((c, LANES), lambda b, cc: (rw(b, cc), 0))]

    def o_spec(rw):
        return pl.BlockSpec((vh, c, LANES), (lambda b, cc: (0, rw(b, cc), 0)) if emit else (lambda b, cc: (0, 0, 0)))

    o_shape = jax.ShapeDtypeStruct((vh, n_rows if emit else c, LANES), BF16)
    of, ob, sf = pl.pallas_call(
        functools.partial(_gdn_scan_kernel, emit=emit),
        grid=(batch, nc),
        in_specs=dir_specs(rows[0]) + dir_specs(rows[1]) + [full(an), full(dtb)] + [full(e) for e in expanders] + [st_spec],
        out_specs=[o_spec(rows[0]), o_spec(rows[1]), st_spec],
        out_shape=[o_shape, o_shape, jax.ShapeDtypeStruct(s0.shape, F32)],
        scratch_shapes=[pltpu.VMEM((2, vh, GDN_DK, GDN_DV), F32)],
        compiler_params=_cparams("arbitrary", "arbitrary"),
        name="gdn_scan",
    )(kqvz, kqvz, kqvz, ba, kqvz, kqvz, kqvz, ba, an, dtb, *expanders, s0)
    return of, ob, sf


def _gdn_out_prologue(of_ref, ob_ref, z_ref, ng_ref):
    parts = []
    for h in range(GDN_V_HEADS):
        o = of_ref[h].astype(F32) + ob_ref[h].astype(F32)
        on = o * lax.rsqrt(jnp.mean(o * o, axis=-1, keepdims=True) + NORM_EPS) * ng_ref[...]
        parts.append(on * jax.nn.silu(z_ref[h].astype(F32)))
    return jnp.concatenate(parts, axis=-1)


def _gdn_mixer(x, x_ctx, modt, modt_ctx, norm_g, prm, batch, seq, seq_ctx, *, ctx_out):
    d = x.shape[-1]
    qk, vv = GDN_K_HEADS * GDN_DK, GDN_V_HEADS * GDN_DV
    ng2 = 2 * 2 * GDN_V_HEADS
    w_in, conv_w = prm['w_in'], prm['conv_w']
    w_all = jnp.concatenate([w_in[:, :qk], w_in[:, qk + vv + ng2:2 * qk + vv + ng2], w_in[:, qk:qk + vv],
                             w_in[:, 2 * qk + vv + ng2:]], axis=1).astype(BF16)
    w_ba = jnp.concatenate([w_in[:, qk + vv:qk + vv + ng2], jnp.zeros((d, LANES - ng2), w_in.dtype)], axis=1).astype(BF16)
    cw_all = jnp.concatenate([conv_w[:, :qk], conv_w[:, qk + vv:], conv_w[:, qk:qk + vv],
                              jnp.zeros((conv_w.shape[0], vv), conv_w.dtype)], axis=1)
    tiles_per_mod = seq // _row_tile(seq)
    p_l, ba_l = _gdn_in(x, modt, tiles_per_mod, norm_g, w_all, cw_all, w_ba, seq)
    p_c, ba_c = _gdn_in(x_ctx, modt_ctx, 10 ** 9, norm_g, w_all, cw_all, w_ba, seq_ctx)
    s0 = jnp.zeros((2, batch, GDN_V_HEADS, GDN_DK, GDN_DV), F32)

    def gate_lanes(p):
        rows = [jnp.zeros((LANES,), F32).at[dr * 2 * GDN_V_HEADS + GDN_V_HEADS:(dr + 1) * 2 * GDN_V_HEADS].set(p[dr])
                for dr in range(2)]
        return jnp.stack(rows).reshape(2, 1, LANES)

    an = gate_lanes(-jnp.exp(prm['a_log'].astype(F32)))
    dtb = gate_lanes(prm['dt_bias'].astype(F32))
    ocf, ocb, sc = _gdn_scan(p_c, ba_c, an, dtb, s0, batch, seq_ctx, emit=ctx_out)
    olf, olb, _ = _gdn_scan(p_l, ba_l, an, dtb, sc, batch, seq)
    os_l, os_c = [olf, olb], [ocf, ocb]
    w_out = prm['w_out'].astype(BF16)
    ng = prm['norm_g'].reshape(1, GDN_DV)
    vh = GDN_V_HEADS
    z_blk = (2 * GDN_K_HEADS + vh) // vh

    def out(xa, pa, oa, mt, tpm, seq_rows):
        tm = _row_tile(seq_rows, cap=ROW_TILE_CAP // 2)
        hspec = lambda blk: pl.BlockSpec((vh, tm, LANES), lambda i, j: (blk, i, 0))
        specs = [hspec(0), hspec(0), hspec(z_blk), _const_spec((1, GDN_DV))]
        return _outproj(_gdn_out_prologue, [oa[0], oa[1], pa, ng], specs, vv, xa, mt, tpm * (_row_tile(seq_rows) // tm),
                        w_out, tm=tm, name="gdn_out")

    y_l = out(x, p_l, os_l, modt, tiles_per_mod, seq)
    y_c = out(x_ctx, p_c, os_c, modt_ctx, 10 ** 9, seq_ctx) if ctx_out else None
    return y_l, y_c


def kernel(x, c, ctx, c_ctx, norm1_g, norm2_g, mod_w, mod_b, ffn_w_in, ffn_conv_w, ffn_conv_b, ffn_w_out, s5_lam_re, s5_lam_im, s5_log_step, s5_b_re, s5_b_im, s5_c_re, s5_c_im, s5_d, s5_w_glu, s5_b_glu, lru_w_in, lru_conv_w, lru_conv_b, lru_w_a, lru_b_a, lru_w_x, lru_b_x, lru_lam, lru_w_out, ret_w_in, ret_norm_g, ret_w_out, gdn_w_in, gdn_conv_w, gdn_a_log, gdn_dt_bias, gdn_norm_g, gdn_w_out, final_norm_g):
    B, L, D = x.shape
    Lc = ctx.shape[1]
    depth = mod_w.shape[0]
    assert B == SUBLANES and depth == 4, "time-major layers put the batch on the 8 sublanes; four mixer kinds"
    c16 = jnp.concatenate([c, c_ctx[None], jnp.zeros((16 - B - 1, D), F32)], 0)
    mods = _modulation(c16, mod_w, mod_b)

    def ffn(i, xa, mt, tiles_per_mod, rs, seq_rows, final_norm=False):
        return _ffn(xa, mt, tiles_per_mod, norm2_g[i], ffn_w_in[i].astype(BF16), ffn_conv_w[i], ffn_conv_b[i],
                    ffn_w_out[i].astype(BF16), final_norm_g, rs=rs, seq_rows=seq_rows, final_norm=final_norm)

    xt = jnp.swapaxes(x, 0, 1).reshape(L * B, D)
    ct = jnp.swapaxes(ctx, 0, 1).reshape(Lc * B, D)
    s5p = dict(lam_re=s5_lam_re[0], lam_im=s5_lam_im[0], log_step=s5_log_step[0], b_re=s5_b_re[0], b_im=s5_b_im[0],
               c_re=s5_c_re[0], c_im=s5_c_im[0], d=s5_d[0], w_glu=s5_w_glu[0], b_glu=s5_b_glu[0])
    lrup = dict(w_in=lru_w_in[0], conv_w=lru_conv_w[0], conv_b=lru_conv_b[0], w_a=lru_w_a[0], b_a=lru_b_a[0],
                w_x=lru_w_x[0], b_x=lru_b_x[0], lam=lru_lam[0], w_out=lru_w_out[0])
    for i, (mixer, prm) in enumerate(((_s5_mixer, s5p), (_lru_mixer, lrup))):
        mt = mods[i, :B][None]
        mtc = jnp.broadcast_to(mods[i, B][None, None, :], (1, SUBLANES, N_MOD * D))
        xt, ct = mixer(xt, ct, mt, mtc, norm1_g[i], prm, ctx_out=True)
        xt = ffn(i, xt, mt, 10 ** 9, SUBLANES, L * B)
        ct = ffn(i, ct, mtc, 10 ** 9, SUBLANES, Lc * B)

    xb = jnp.swapaxes(xt.reshape(L, B, D), 0, 1).reshape(B * L, D)
    cb = jnp.swapaxes(ct.reshape(Lc, B, D), 0, 1).reshape(B * Lc, D)
    retp = dict(w_in=ret_w_in[0], norm_g=ret_norm_g[0], w_out=ret_w_out[0])
    gdnp = dict(w_in=gdn_w_in[0], conv_w=gdn_conv_w[0], a_log=gdn_a_log[0], dt_bias=gdn_dt_bias[0],
                norm_g=gdn_norm_g[0], w_out=gdn_w_out[0])
    tiles_per_batch = L // _row_tile(L)
    for i, (mixer, prm) in ((2, (_ret_mixer, retp)), (3, (_gdn_mixer, gdnp))):
        ctx_out = i < depth - 1
        mt = jnp.broadcast_to(mods[i, :B][:, None, :], (B, SUBLANES, N_MOD * D))
        mtc = jnp.broadcast_to(mods[i, B][None, None, :], (1, SUBLANES, N_MOD * D))
        xb, cb_new = mixer(xb, cb, mt, mtc, norm1_g[i], prm, B, L, Lc, ctx_out=ctx_out)
        xb = ffn(i, xb, mt, tiles_per_batch, 1, L, final_norm=not ctx_out)
        if ctx_out:
            cb = ffn(i, cb_new, mtc, 10 ** 9, 1, Lc)
    return xb.reshape(B, L, D)
```

```python
import functools
import math

import numpy as np
import jax
import jax.numpy as jnp
from jax import lax
from jax.experimental import pallas as pl
from jax.experimental.pallas import tpu as pltpu

F32 = jnp.float32
BF16 = jnp.bfloat16

NORM_EPS = 1e-6
SUBLANES = 8
LANES = 128
VMEM_LIMIT_BYTES = 56 * 1024 * 1024
ROW_TILE_CAP = 1024

S5_GROUP = 16
S5_STATE = 64
S5_GB = 16
LRU_BLOCKS = 4
LRU_C = 8.0
RET_HEADS = 4
RET_DK = 256
RET_DV = 512
RET_CHUNK = 128
ROPE_BASE = 10000.0
GRID_W = 64
GDN_K_HEADS = 8
GDN_V_HEADS = 16
GDN_DK = 128
GDN_DV = 128
GDN_CHUNK = 64
N_MOD = 6


def _cparams(*sem):
    return pltpu.CompilerParams(dimension_semantics=sem, vmem_limit_bytes=VMEM_LIMIT_BYTES)


def _row_tile(rows, cap=None):
    cap = ROW_TILE_CAP if cap is None else cap
    tm = min(cap, rows)
    assert rows % tm == 0 and tm % SUBLANES == 0, (rows, tm)
    return tm


def _dot(a, b):
    return jnp.dot(a, b, preferred_element_type=F32)


def _dot_tn(a, b):
    return lax.dot_general(a, b, (((0,), (0,)), ((), ())), preferred_element_type=F32)


def _dot_nt(a, b):
    return lax.dot_general(a, b, (((1,), (1,)), ((), ())), preferred_element_type=F32)


def _gelu(x):
    return jax.nn.gelu(x, approximate=True)


def _sigmoid(x):
    return 1.0 / (1.0 + jnp.exp(-x))


def _silu(x):
    return x * _sigmoid(x)


def _tile8(v, rows):
    n = v.shape[-1]
    return jnp.broadcast_to(v[None], (rows // SUBLANES, SUBLANES, n)).reshape(rows, n)


def _norm_mod(x, g, sc, sh):
    ms = jnp.mean(x * x, axis=-1, keepdims=True)
    y = x * lax.rsqrt(ms + NORM_EPS) * g
    rows = x.shape[0]
    return y * (1.0 + _tile8(sc, rows)) + _tile8(sh, rows)


def _shift_rows(p, off, rows):
    if off % SUBLANES == 0:
        return p[off:off + rows]
    n = p.shape[0]
    base = (off // SUBLANES) * SUBLANES
    rolled = pltpu.roll(p, (n - (off - base)) % n, axis=0)
    return rolled[base:base + rows]


def _conv_rows(p_ext, cw, rows, rs, hb):
    k_taps = cw.shape[0]
    acc = None
    for k in range(k_taps):
        term = cw[k:k + 1, :] * _shift_rows(p_ext, hb + (k - k_taps // 2) * rs, rows)
        acc = term if acc is None else acc + term
    return acc


def _fill_h(h_ref, x_ref, xb_ref, xa_ref, g_ref, sc_ref, sh_ref, first, last, hb, ha):
    tm = x_ref.shape[0]
    g, sc, sh = g_ref[...], sc_ref[...], sh_ref[...]
    h_ref[hb:hb + tm, :] = _norm_mod(x_ref[...], g, sc, sh).astype(BF16)
    if hb:
        hbv = _norm_mod(xb_ref[...], g, sc, sh)
        h_ref[0:hb, :] = jnp.where(first, 0.0, hbv).astype(BF16)
    if ha:
        hav = _norm_mod(xa_ref[...], g, sc, sh)
        h_ref[hb + tm:hb + tm + ha, :] = jnp.where(last, 0.0, hav).astype(BF16)


def _halo_specs(tm, hb, ha, d, n_rows):
    specs = [pl.BlockSpec((tm, d), lambda i, j: (i, 0))]
    if hb:
        specs.append(pl.BlockSpec((hb, d), lambda i, j: (jnp.maximum(i * (tm // hb) - 1, 0), 0)))
    if ha:
        nblk = n_rows // ha
        specs.append(pl.BlockSpec((ha, d), lambda i, j: (jnp.minimum((i + 1) * (tm // ha), nblk - 1), 0)))
    return specs


def _mod_spec(chunk, d, tiles_per_mod):
    return pl.BlockSpec((None, SUBLANES, d), lambda i, j: (i // tiles_per_mod, 0, chunk))


def _mod_kernel(c_ref, w_ref, b_ref, o_ref):
    a = _silu(c_ref[...]).astype(BF16)
    o_ref[...] = _dot(a, w_ref[...].astype(BF16)) + b_ref[...]


def _modulation(c16, mod_w, mod_b):
    depth, d, n = mod_w.shape
    tn = n // 4
    return pl.pallas_call(
        _mod_kernel,
        grid=(depth, n // tn),
        in_specs=[pl.BlockSpec((16, d), lambda l, j: (0, 0)),
                  pl.BlockSpec((None, d, tn), lambda l, j: (l, 0, j)),
                  pl.BlockSpec((None, 1, tn), lambda l, j: (l, 0, j))],
        out_specs=pl.BlockSpec((None, 16, tn), lambda l, j: (l, 0, j)),
        out_shape=jax.ShapeDtypeStruct((depth, 16, n), F32),
        compiler_params=_cparams("arbitrary", "arbitrary"),
        name="modulation",
    )(c16, mod_w, mod_b.reshape(depth, 1, n))


FFN_TILE = 2 * LANES


def _ffn_kernel(x_ref, xb_ref, xa_ref, g_ref, sh_ref, sc_ref, gt_ref, win_ref, cw_ref, cb_ref, wo_ref, fg_ref, o_ref, h_ref,
                *, rs, tiles_per_seq, final_norm):
    i = pl.program_id(0)
    tm = x_ref.shape[0]
    hb = ha = SUBLANES
    f = wo_ref.shape[0]
    first = i % tiles_per_seq == 0
    last = i % tiles_per_seq == tiles_per_seq - 1
    _fill_h(h_ref, x_ref, xb_ref, xa_ref, g_ref, sc_ref, sh_ref, first, last, hb, ha)
    acc = None
    for j in range(f // FFN_TILE):
        cols = slice(j * FFN_TILE, (j + 1) * FFN_TILE)
        gate = _dot(h_ref[...], win_ref[:, cols])
        up = _dot(h_ref[hb:hb + tm, :], win_ref[:, f + j * FFN_TILE:f + (j + 1) * FFN_TILE])
        gc = _conv_rows(gate, cw_ref[:, cols], tm, rs, hb) + cb_ref[:, cols]
        part = _dot((_gelu(gc) * up).astype(BF16), wo_ref[cols, :])
        acc = part if acc is None else acc + part
    y = x_ref[...] + _tile8(gt_ref[...], tm) * acc
    if final_norm:
        ms = jnp.mean(y * y, axis=-1, keepdims=True)
        y = y * lax.rsqrt(ms + NORM_EPS) * fg_ref[...]
    o_ref[...] = y


def _ffn(x, modt, tiles_per_mod, norm_g, w_in, conv_w, conv_b, w_out, final_g, *, rs, seq_rows, final_norm=False):
    n_rows, d = x.shape
    f = w_out.shape[0]
    tm = _row_tile(seq_rows)
    assert f % FFN_TILE == 0
    hb = ha = SUBLANES
    kern = functools.partial(_ffn_kernel, rs=rs, tiles_per_seq=seq_rows // tm, final_norm=final_norm)
    resident = lambda shape: pl.BlockSpec(shape, lambda i, j: (0, 0), pipeline_mode=pl.Buffered(1))
    in_specs = _halo_specs(tm, hb, ha, d, n_rows) + [
        pl.BlockSpec((1, d), lambda i, j: (0, 0)),
        _mod_spec(3, d, tiles_per_mod), _mod_spec(4, d, tiles_per_mod), _mod_spec(5, d, tiles_per_mod),
        resident(w_in.shape), resident(conv_w.shape), resident((1, f)), resident(w_out.shape),
        pl.BlockSpec((1, d), lambda i, j: (0, 0)),
    ]
    return pl.pallas_call(
        kern,
        grid=(n_rows // tm, 1),
        in_specs=in_specs,
        out_specs=pl.BlockSpec((tm, d), lambda i, j: (i, 0)),
        out_shape=jax.ShapeDtypeStruct((n_rows, d), F32),
        scratch_shapes=[pltpu.VMEM((tm + hb + ha, d), BF16)],
        compiler_params=_cparams("parallel", "arbitrary"),
        name="conv_ffn",
    )(x, x, x, norm_g.reshape(1, d), modt, modt, modt, w_in, conv_w, conv_b.reshape(1, f), w_out, final_g.reshape(1, d))


def _outproj_kernel(*refs, n_pro, prologue, glu):
    pro_refs = refs[:n_pro]
    if glu:
        x_ref, gt_ref, w_ref, b_ref, o_ref = refs[n_pro:]
    else:
        x_ref, gt_ref, w_ref, o_ref = refs[n_pro:]
    d = o_ref.shape[1]
    a = prologue(*pro_refs).astype(BF16)
    y = _dot(a, w_ref[:, 0:d])
    if glu:
        y = (y + b_ref[:, 0:d]) * jax.nn.sigmoid(_dot(a, w_ref[:, d:2 * d]) + b_ref[:, d:2 * d])
    o_ref[...] = x_ref[...] + _tile8(gt_ref[...], o_ref.shape[0]) * y


def _outproj(prologue, pro_args, pro_specs, k_dim, x, modt, tiles_per_mod, w, bias=None, *, tm, name):
    n_rows, d = x.shape
    glu = bias is not None
    kern = functools.partial(_outproj_kernel, n_pro=len(pro_args), prologue=prologue, glu=glu)
    specs = list(pro_specs) + [pl.BlockSpec((tm, d), lambda i, j: (i, 0)),
                               pl.BlockSpec((None, SUBLANES, d), lambda i, j: (i // tiles_per_mod, 0, 2)),
                               pl.BlockSpec(w.shape, lambda i, j: (0, 0))]
    args = list(pro_args) + [x, modt, w]
    if glu:
        specs += [pl.BlockSpec(bias.shape, lambda i, j: (0, 0))]
        args += [bias]
    return pl.pallas_call(
        kern,
        grid=(n_rows // tm, 1),
        in_specs=specs,
        out_specs=pl.BlockSpec((tm, d), lambda i, j: (i, 0)),
        out_shape=jax.ShapeDtypeStruct((n_rows, d), F32),
        compiler_params=_cparams("parallel", "arbitrary"),
        name=name,
    )(*args)


def _row_spec(tm, n):
    return pl.BlockSpec((tm, n), lambda i, j: (i, 0))


def _const_spec(shape):
    nd = len(shape)
    return pl.BlockSpec(shape, lambda i, j: (0,) * nd)


def _s5_weights(lam_re, lam_im, log_step, b_re, b_im, c_re, c_im):
    g, p = lam_re.shape
    gc = b_re.shape[-1]
    lr = jnp.minimum(lam_re.astype(F32), -1e-4)
    li = lam_im.astype(F32)
    step = jnp.exp(log_step.astype(F32))[:, None]
    mag = jnp.exp(lr * step)
    ar, ai = mag * jnp.cos(li * step), mag * jnp.sin(li * step)
    den = lr * lr + li * li
    kr = ((ar - 1.0) * lr + ai * li) / den
    ki = (ai * lr - (ar - 1.0) * li) / den
    br32, bi32 = b_re.astype(F32), b_im.astype(F32)
    br = kr[..., None] * br32 - ki[..., None] * bi32
    bi = kr[..., None] * bi32 + ki[..., None] * br32
    nb = g // S5_GB
    eye = jnp.eye(S5_GB, dtype=F32)

    def pack_b(b):
        b4 = b.reshape(nb, S5_GB, p, gc)
        return jnp.einsum('blpc,lm->blcmp', b4, eye).reshape(nb, S5_GB * gc, S5_GB * p)

    def pack_c(cm):
        c4 = cm.astype(F32).reshape(nb, S5_GB, gc, p)
        return jnp.einsum('blcp,lm->blpmc', c4, eye).reshape(nb, S5_GB * p, S5_GB * gc)

    bm = jnp.concatenate([pack_b(br), pack_b(bi)], axis=-1).astype(BF16)
    cm = jnp.concatenate([pack_c(c_re), -pack_c(c_im)], axis=1).astype(BF16)
    lam = jnp.stack([ar.reshape(nb, S5_GB * p), ai.reshape(nb, S5_GB * p)], axis=1)
    lam = jnp.broadcast_to(lam[:, :, None, :], (nb, 2, SUBLANES, S5_GB * p))
    return bm, cm, lam


def _s5_scan_kernel(x_ref, g_ref, sh_ref, sc_ref, bm_ref, cm_ref, lam_ref, s0_ref, y_ref, sf_ref, st_ref, bu_ref,
                    *, rev, emit):
    i = pl.program_id(0)
    tm = x_ref.shape[0]
    nt = tm // SUBLANES
    nb, kin, ns2 = bm_ref.shape
    ns = ns2 // 2

    @pl.when(i == 0)
    def _():
        st_ref[...] = s0_ref[...]
        if not emit:
            y_ref[...] = jnp.zeros_like(y_ref)

    h = _norm_mod(x_ref[...], g_ref[...], sc_ref[...], sh_ref[...]).astype(BF16)
    for gb in range(nb):
        bu_ref[...] = _dot(h[:, gb * kin:(gb + 1) * kin], bm_ref[gb])
        ar, ai = lam_ref[gb, 0], lam_ref[gb, 1]

        def step(t, carry):
            sr, si = carry
            tt = nt - 1 - t if rev else t
            r0 = pl.multiple_of(tt * SUBLANES, SUBLANES)
            nr = ar * sr - ai * si + bu_ref[pl.ds(r0, SUBLANES), 0:ns]
            ni = ar * si + ai * sr + bu_ref[pl.ds(r0, SUBLANES), ns:ns2]
            if emit:
                bu_ref[pl.ds(r0, SUBLANES), 0:ns] = nr
                bu_ref[pl.ds(r0, SUBLANES), ns:ns2] = ni
            return nr, ni

        sr, si = lax.fori_loop(0, nt, step, (st_ref[gb, 0], st_ref[gb, 1]), unroll=4)
        st_ref[gb, 0] = sr
        st_ref[gb, 1] = si
        if emit:
            y_ref[:, gb * kin:(gb + 1) * kin] = _dot(bu_ref[...].astype(BF16), cm_ref[gb])

    @pl.when(i == pl.num_programs(0) - 1)
    def _():
        sf_ref[...] = st_ref[...]


def _s5_scan(x, modt, norm_g, bm, cm, lam, s0, *, rev, emit=True):
    n_rows, d = x.shape
    tm = _row_tile(n_rows, cap=ROW_TILE_CAP // 2)
    ntile = n_rows // tm
    tile = (lambda i: (ntile - 1 - i, 0)) if rev else (lambda i: (i, 0))
    mod = lambda k: pl.BlockSpec((None, SUBLANES, d), lambda i: (0, 0, k))
    full = lambda a: pl.BlockSpec(a.shape, lambda i: (0,) * a.ndim)
    y, sf = pl.pallas_call(
        functools.partial(_s5_scan_kernel, rev=rev, emit=emit),
        grid=(ntile,),
        in_specs=[pl.BlockSpec((tm, d), tile), pl.BlockSpec((1, d), lambda i: (0, 0)), mod(0), mod(1),
                  full(bm), full(cm), full(lam), full(s0)],
        out_specs=[pl.BlockSpec((tm, d), tile if emit else (lambda i: (0, 0))), full(s0)],
        out_shape=[jax.ShapeDtypeStruct((n_rows if emit else tm, d), F32), jax.ShapeDtypeStruct(s0.shape, F32)],
        scratch_shapes=[pltpu.VMEM(s0.shape, F32), pltpu.VMEM((tm, bm.shape[-1]), F32)],
        compiler_params=_cparams("arbitrary"),
        name="s5_scan_bwd" if rev else "s5_scan_fwd",
    )(x, norm_g.reshape(1, d), modt, modt, bm, cm, lam, s0)
    return (y if emit else None), sf


def _s5_out_prologue(x_ref, yf_ref, yb_ref, g_ref, sh_ref, sc_ref, dk_ref):
    u = _norm_mod(x_ref[...], g_ref[...], sc_ref[...], sh_ref[...])
    return _gelu(yf_ref[...] + yb_ref[...] + dk_ref[...] * u)


def _s5_mixer(x, x_ctx, modt, modt_ctx, norm_g, prm, *, ctx_out):
    d = x.shape[-1]
    w = [_s5_weights(prm['lam_re'][dr], prm['lam_im'][dr], prm['log_step'][dr], prm['b_re'][dr], prm['b_im'][dr],
                     prm['c_re'][dr], prm['c_im'][dr]) for dr in range(2)]
    zero = jnp.zeros(w[0][2].shape, F32)
    ys, ys_ctx = [], []
    for dr in range(2):
        bm, cm, lam = w[dr]
        yc, sc = _s5_scan(x_ctx, modt_ctx, norm_g, bm, cm, lam, zero, rev=dr == 1, emit=ctx_out)
        yl, _ = _s5_scan(x, modt, norm_g, bm, cm, lam, sc, rev=dr == 1)
        ys.append(yl)
        ys_ctx.append(yc)
    w_glu = prm['w_glu'].astype(BF16)
    b_glu = prm['b_glu'].reshape(1, -1)

    def out(xa, ya, mt):
        tm = _row_tile(xa.shape[0], cap=ROW_TILE_CAP // 2)
        pro_args = [xa, ya[0], ya[1], norm_g.reshape(1, d), mt, mt, prm['d'].reshape(1, d)]
        pro_specs = [_row_spec(tm, d), _row_spec(tm, d), _row_spec(tm, d), _const_spec((1, d)),
                     _mod_spec(0, d, 10 ** 9), _mod_spec(1, d, 10 ** 9), _const_spec((1, d))]
        return _outproj(_s5_out_prologue, pro_args, pro_specs, d, xa, mt, 10 ** 9, w_glu, b_glu, tm=tm, name="s5_out")

    return out(x, ys, modt), (out(x_ctx, ys_ctx, modt_ctx) if ctx_out else None)


def _lru_in_kernel(x_ref, xb_ref, xa_ref, g_ref, sh_ref, sc_ref, wr_ref, wy_ref, cw_ref, cb_ref, xc_ref, yg_ref, h_ref,
                   *, rs, hb, ha):
    i, j = pl.program_id(0), pl.program_id(1)
    tm = x_ref.shape[0]

    @pl.when(j == 0)
    def _():
        _fill_h(h_ref, x_ref, xb_ref, xa_ref, g_ref, sc_ref, sh_ref, i == 0, i == pl.num_programs(0) - 1, hb, ha)

    p = _dot(h_ref[...], wr_ref[...])
    xc_ref[...] = _conv_rows(p, cw_ref[...], tm, rs, hb) + cb_ref[...]
    yg_ref[...] = _gelu(_dot(h_ref[hb:hb + tm, :], wy_ref[...]))


def _lru_in(x, modt, norm_g, w_in, conv_w, conv_b):
    n_rows, d = x.shape
    wd = conv_w.shape[-1]
    rs = SUBLANES
    hb, ha = 2 * rs, rs
    tm = _row_tile(n_rows)
    tn = 4 * LANES
    nj = wd // tn
    specs = _halo_specs(tm, hb, ha, d, n_rows) + [
        _const_spec((1, d)), _mod_spec(0, d, 10 ** 9), _mod_spec(1, d, 10 ** 9),
        pl.BlockSpec((d, tn), lambda i, j: (0, j)), pl.BlockSpec((d, tn), lambda i, j: (0, j + nj)),
        pl.BlockSpec((conv_w.shape[0], tn), lambda i, j: (0, j)), pl.BlockSpec((1, tn), lambda i, j: (0, j))]
    out_spec = pl.BlockSpec((tm, tn), lambda i, j: (i, j))
    return pl.pallas_call(
        functools.partial(_lru_in_kernel, rs=rs, hb=hb, ha=ha),
        grid=(n_rows // tm, nj),
        in_specs=specs,
        out_specs=[out_spec, out_spec],
        out_shape=[jax.ShapeDtypeStruct((n_rows, wd), F32)] * 2,
        scratch_shapes=[pltpu.VMEM((tm + hb + ha, d), BF16)],
        compiler_params=_cparams("parallel", "arbitrary"),
        name="lru_in",
    )(x, x, x, norm_g.reshape(1, d), modt, modt, w_in, w_in, conv_w, conv_b.reshape(1, wd))


def _lru_scan_kernel(xc_ref, wa_ref, wx_ref, ba_ref, bx_ref, nsp_ref, h0_ref, hs_ref, hf_ref, st_ref, a_ref, b_ref,
                     *, rev, emit):
    i = pl.program_id(0)
    tm = xc_ref.shape[0]
    nt = tm // SUBLANES
    nblk, bw, _ = wa_ref.shape

    @pl.when(i == 0)
    def _():
        st_ref[...] = h0_ref[...]
        if not emit:
            hs_ref[...] = jnp.zeros_like(hs_ref)

    for k in range(nblk):
        cols = slice(k * bw, (k + 1) * bw)
        xc = xc_ref[:, cols]
        xcb = xc.astype(BF16)
        r = jax.nn.sigmoid(_dot(xcb, wa_ref[k]) + ba_ref[:, cols])
        gi = jax.nn.sigmoid(_dot(xcb, wx_ref[k]) + bx_ref[:, cols])
        log_a = nsp_ref[:, cols] * r
        a = jnp.exp(log_a)
        a_ref[:, cols] = a
        b_ref[:, cols] = jnp.sqrt(1.0 - a * a) * (gi * xc)

    def step(t, h):
        tt = nt - 1 - t if rev else t
        r0 = pl.multiple_of(tt * SUBLANES, SUBLANES)
        h = a_ref[pl.ds(r0, SUBLANES), :] * h + b_ref[pl.ds(r0, SUBLANES), :]
        if emit:
            hs_ref[pl.ds(r0, SUBLANES), :] = h
        return h

    st_ref[...] = lax.fori_loop(0, nt, step, st_ref[...], unroll=8)

    @pl.when(i == pl.num_programs(0) - 1)
    def _():
        hf_ref[...] = st_ref[...]


def _lru_scan(xc, wa, wx, ba, bx, nsp, h0, *, rev, emit=True):
    n_rows, wd = xc.shape
    tm = _row_tile(n_rows)
    ntile = n_rows // tm
    tile = (lambda i: (ntile - 1 - i, 0)) if rev else (lambda i: (i, 0))
    full = lambda a: pl.BlockSpec(a.shape, lambda i: (0,) * a.ndim)
    hs, hf = pl.pallas_call(
        functools.partial(_lru_scan_kernel, rev=rev, emit=emit),
        grid=(ntile,),
        in_specs=[pl.BlockSpec((tm, wd), tile), full(wa), full(wx), full(ba), full(bx), full(nsp), full(h0)],
        out_specs=[pl.BlockSpec((tm, wd), tile if emit else (lambda i: (0, 0))), full(h0)],
        out_shape=[jax.ShapeDtypeStruct((n_rows if emit else tm, wd), F32), jax.ShapeDtypeStruct(h0.shape, F32)],
        scratch_shapes=[pltpu.VMEM(h0.shape, F32), pltpu.VMEM((tm, wd), F32), pltpu.VMEM((tm, wd), F32)],
        compiler_params=_cparams("arbitrary"),
        name="lru_scan_bwd" if rev else "lru_scan_fwd",
    )(xc, wa, wx, ba, bx, nsp, h0)
    return (hs if emit else None), hf


def _lru_out_prologue(yg_ref, hf_ref, hb_ref):
    return yg_ref[...] * (hf_ref[...] + hb_ref[...])


def _lru_mixer(x, x_ctx, modt, modt_ctx, norm_g, prm, *, ctx_out):
    d = x.shape[-1]
    w_in = prm['w_in'].astype(BF16)
    wd = prm['conv_w'].shape[-1]
    xc_l, yg_l = _lru_in(x, modt, norm_g, w_in, prm['conv_w'], prm['conv_b'])
    xc_c, yg_c = _lru_in(x_ctx, modt_ctx, norm_g, w_in, prm['conv_w'], prm['conv_b'])
    zero = jnp.zeros((SUBLANES, wd), F32)
    hs_l, hs_c = [], []
    for dr in range(2):
        wa, wx = prm['w_a'][dr].astype(BF16), prm['w_x'][dr].astype(BF16)
        ba, bx = prm['b_a'][dr].reshape(1, wd), prm['b_x'][dr].reshape(1, wd)
        nsp = (-LRU_C * jax.nn.softplus(-prm['lam'][dr].astype(F32))).reshape(1, wd)
        hc, hfin = _lru_scan(xc_c, wa, wx, ba, bx, nsp, zero, rev=dr == 1, emit=ctx_out)
        hl, _ = _lru_scan(xc_l, wa, wx, ba, bx, nsp, hfin, rev=dr == 1)
        hs_l.append(hl)
        hs_c.append(hc)
    w_out = prm['w_out'].astype(BF16)

    def out(xa, yg, hs, mt):
        tm = _row_tile(xa.shape[0], cap=ROW_TILE_CAP // 2)
        specs = [_row_spec(tm, wd)] * 3
        return _outproj(_lru_out_prologue, [yg, hs[0], hs[1]], specs, wd, xa, mt, 10 ** 9, w_out, tm=tm, name="lru_out")

    return out(x, yg_l, hs_l, modt), (out(x_ctx, yg_c, hs_c, modt_ctx) if ctx_out else None)


def _rope_tables(length):
    rows = length // GRID_W
    t = jnp.arange(length, dtype=jnp.int32)
    row = (t // GRID_W).astype(F32) - (rows - 1) / 2.0
    col = (t % GRID_W).astype(F32) - (GRID_W - 1) / 2.0
    quarter = RET_DK // 4
    inv_freq = ROPE_BASE ** (-jnp.arange(quarter, dtype=F32) / quarter)
    ar, ac = row[:, None] * inv_freq[None, :], col[:, None] * inv_freq[None, :]
    cos = jnp.concatenate([jnp.cos(ar), jnp.cos(ar), jnp.cos(ac), jnp.cos(ac)], axis=-1)
    sin = jnp.concatenate([-jnp.sin(ar), jnp.sin(ar), -jnp.sin(ac), jnp.sin(ac)], axis=-1)
    return cos, sin


def _ret_in_kernel(x_ref, g_ref, sh_ref, sc_ref, w_ref, cos_ref, sin_ref, p_ref, h_ref, *, rotate, n_k, n_v):
    j = pl.program_id(1)
    tn = w_ref.shape[1]

    @pl.when(j == 0)
    def _():
        h_ref[...] = _norm_mod(x_ref[...], g_ref[...], sc_ref[...], sh_ref[...]).astype(BF16)

    is_k = j < n_k
    is_q = (j >= n_k + n_v) & (j < 2 * n_k + n_v)

    @pl.when(is_k | is_q)
    def _():
        scale = jnp.where(is_q, RET_DK ** -0.5, 1.0)
        for hd in range(tn // RET_DK):
            ph = _dot(h_ref[...], w_ref[:, hd * RET_DK:(hd + 1) * RET_DK])
            for s in range(RET_DK // LANES):
                tcols = slice(s * LANES, (s + 1) * LANES)
                ps = ph[:, tcols]
                if rotate:
                    rot = pltpu.roll(ps, LANES // 2, axis=1)
                    ps = ps * cos_ref[:, tcols] + rot * sin_ref[:, tcols]
                p_ref[:, hd * RET_DK + s * LANES:hd * RET_DK + (s + 1) * LANES] = (ps * scale).astype(p_ref.dtype)

    @pl.when(jnp.logical_not(is_k | is_q))
    def _():
        p_ref[...] = _dot(h_ref[...], w_ref[...]).astype(p_ref.dtype)


def _ret_in(x, modt, tiles_per_mod, norm_g, w_in, seq_rows, *, rotate):
    n_rows, d = x.shape
    n_out = w_in.shape[1]
    tm = _row_tile(seq_rows)
    tn = RET_HEADS * RET_DK
    tiles_per_seq = seq_rows // tm
    cos, sin = _rope_tables(seq_rows)
    hk, hv = RET_HEADS * RET_DK, RET_HEADS * RET_DV
    tab_spec = pl.BlockSpec((tm, RET_DK), lambda i, j: (i % tiles_per_seq, 0))
    return pl.pallas_call(
        functools.partial(_ret_in_kernel, rotate=rotate, n_k=hk // tn, n_v=hv // tn),
        grid=(n_rows // tm, n_out // tn),
        in_specs=[_row_spec(tm, d), _const_spec((1, d)), _mod_spec(0, d, tiles_per_mod), _mod_spec(1, d, tiles_per_mod),
                  pl.BlockSpec((d, tn), lambda i, j: (0, j)), tab_spec, tab_spec],
        out_specs=pl.BlockSpec((tm, tn), lambda i, j: (i, j)),
        out_shape=jax.ShapeDtypeStruct((n_rows, n_out), BF16),
        scratch_shapes=[pltpu.VMEM((tm, d), BF16)],
        compiler_params=_cparams("parallel", "arbitrary"),
        name="ret_in",
    )(x, norm_g.reshape(1, d), modt, modt, w_in, cos, sin)


def _ret_tables(c):
    log_g = np.log1p(-np.power(2.0, -5.0 - np.arange(RET_HEADS, dtype=np.float64)))
    idx = np.arange(c, dtype=np.float64)
    diff = idx[:, None] - idx[None, :]
    fwd = np.where(diff >= 0, np.exp(np.where(diff >= 0, diff, 0.0)[None] * log_g[:, None, None]), 0.0)
    bwd = np.where(diff < 0, np.exp(np.where(diff < 0, -diff, 0.0)[None] * log_g[:, None, None]), 0.0)
    xi_f = np.exp((idx + 1.0)[None, :] * log_g[:, None])
    zeta_f = np.exp((c - 1.0 - idx)[None, :] * log_g[:, None])
    xi_b = np.exp((c - idx)[None, :] * log_g[:, None])
    zeta_b = np.exp(idx[None, :] * log_g[:, None])
    dmask = np.stack([fwd, bwd]).astype(np.float32)
    xi = np.stack([xi_f, xi_b])[..., None].astype(np.float32)
    zeta = np.stack([zeta_f, zeta_b])[..., None].astype(np.float32)
    g_blk = [float(v) for v in np.exp(c * log_g).astype(np.float32)]
    return jnp.asarray(dmask), jnp.asarray(xi), jnp.asarray(zeta), g_blk


def _ret_scan_kernel(kf_ref, v0f_ref, v1f_ref, qf_ref, kb_ref, v0b_ref, v1b_ref, qb_ref, dm_ref, xi_ref, zt_ref, r0_ref,
                     of_ref, ob_ref, rf_ref, r_ref, *, g_blk, emit):
    c = pl.program_id(1)
    dir_refs = ((kf_ref, v0f_ref, v1f_ref, qf_ref, of_ref), (kb_ref, v0b_ref, v1b_ref, qb_ref, ob_ref))

    @pl.when(c == 0)
    def _():
        r_ref[...] = r0_ref[...]
        if not emit:
            of_ref[...] = jnp.zeros_like(of_ref)
            ob_ref[...] = jnp.zeros_like(ob_ref)

    hv_half = v0f_ref.shape[1] // RET_DV
    r_olds = [[r_ref[d, h] for h in range(RET_HEADS)] for d in range(2)]
    r_news = [[None] * RET_HEADS for _ in range(2)]
    for h in range(RET_HEADS):
        for d in range(2):
            k_ref, v0_ref, v1_ref, q_ref, o_ref = dir_refs[d]
            kh = k_ref[:, h * RET_DK:(h + 1) * RET_DK]
            v_ref = v0_ref if h < hv_half else v1_ref
            hh = h % hv_half
            vh = v_ref[:, hh * RET_DV:(hh + 1) * RET_DV]
            r_old = r_olds[d][h]
            if emit:
                qh = q_ref[:, h * RET_DK:(h + 1) * RET_DK]
                s = _dot_nt(qh, kh) * dm_ref[d, h]
                o = _dot(s.astype(BF16), vh) + _dot(qh, r_old.astype(BF16)) * xi_ref[d, h]
                o_ref[:, h * RET_DV:(h + 1) * RET_DV] = o.astype(o_ref.dtype)
            r_news[d][h] = g_blk[h] * r_old + _dot_tn((kh * zt_ref[d, h]).astype(BF16), vh)
    for d in range(2):
        for h in range(RET_HEADS):
            r_ref[d, h] = r_news[d][h]

    @pl.when(c == pl.num_programs(1) - 1)
    def _():
        rf_ref[...] = r_ref[...]


def _ret_scan(p, r0, batch, seq_rows, *, emit=True):
    c = min(RET_CHUNK, seq_rows)
    nc = seq_rows // c
    hk, hv = RET_HEADS * RET_DK, RET_HEADS * RET_DV
    dmask, xi, zeta, g_blk = _ret_tables(c)
    n_rows = p.shape[0]
    rows = (lambda b, cc: b * nc + cc, lambda b, cc: b * nc + nc - 1 - cc)
    kcols = lambda rw: [pl.BlockSpec((c, hk), lambda b, cc, blk=blk: (rw(b, cc), blk)) for blk in range(4)]
    full = lambda a: pl.BlockSpec(a.shape, lambda b, cc: (0,) * a.ndim)
    st_spec = pl.BlockSpec((2, None, RET_HEADS, RET_DK, RET_DV), lambda b, cc: (0, b, 0, 0, 0))
    o_spec = lambda rw: pl.BlockSpec((c, hv), (lambda b, cc: (rw(b, cc), 0)) if emit else (lambda b, cc: (0, 0)))
    o_shape = jax.ShapeDtypeStruct((n_rows if emit else c, hv), BF16)
    assert hv == 2 * hk
    of, ob, rf = pl.pallas_call(
        functools.partial(_ret_scan_kernel, g_blk=g_blk, emit=emit),
        grid=(batch, nc),
        in_specs=kcols(rows[0]) + kcols(rows[1]) + [full(dmask), full(xi), full(zeta), st_spec],
        out_specs=[o_spec(rows[0]), o_spec(rows[1]), st_spec],
        out_shape=[o_shape, o_shape, jax.ShapeDtypeStruct(r0.shape, F32)],
        scratch_shapes=[pltpu.VMEM((2, RET_HEADS, RET_DK, RET_DV), F32)],
        compiler_params=_cparams("arbitrary", "arbitrary"),
        name="ret_scan",
    )(*([p] * 8), dmask, xi, zeta, r0)
    return of, ob, rf


def _ret_out_prologue(of_ref, ob_ref, gate_ref, ng_ref):
    o = of_ref[...].astype(F32) + ob_ref[...].astype(F32)
    parts = []
    for h in range(RET_HEADS):
        oh = o[:, h * RET_DV:(h + 1) * RET_DV]
        parts.append(oh * lax.rsqrt(jnp.mean(oh * oh, axis=-1, keepdims=True) + NORM_EPS))
    on = (jnp.concatenate(parts, axis=-1) * ng_ref[...])
    return jax.nn.silu(gate_ref[...].astype(F32)) * on


def _ret_mixer(x, x_ctx, modt, modt_ctx, norm_g, prm, batch, seq, seq_ctx, *, ctx_out):
    d = x.shape[-1]
    w_in = prm['w_in'].astype(BF16)
    hk, hv = RET_HEADS * RET_DK, RET_HEADS * RET_DV
    tiles_per_mod = seq // _row_tile(seq)
    p_l = _ret_in(x, modt, tiles_per_mod, norm_g, w_in, seq, rotate=True)
    p_c = _ret_in(x_ctx, modt_ctx, 10 ** 9, norm_g, w_in, seq_ctx, rotate=False)
    r0 = jnp.zeros((2, batch, RET_HEADS, RET_DK, RET_DV), F32)
    of_c, ob_c, r_c = _ret_scan(p_c, r0, batch, seq_ctx, emit=ctx_out)
    of_l, ob_l, _ = _ret_scan(p_l, r_c, batch, seq)
    w_out = prm['w_out'].astype(BF16)
    ng = prm['norm_g'].reshape(1, hv)

    def out(xa, pa, of, ob, mt, tpm, seq_rows):
        tm = _row_tile(seq_rows, cap=ROW_TILE_CAP // 2)
        gcol = (2 * hk + hv) // hv
        specs = [_row_spec(tm, hv), _row_spec(tm, hv), pl.BlockSpec((tm, hv), lambda i, j: (i, gcol)), _const_spec((1, hv))]
        return _outproj(_ret_out_prologue, [of, ob, pa, ng], specs, hv, xa, mt, tpm * (_row_tile(seq_rows) // tm), w_out,
                        tm=tm, name="ret_out")

    y_l = out(x, p_l, of_l, ob_l, modt, tiles_per_mod, seq)
    y_c = out(x_ctx, p_c, of_c, ob_c, modt_ctx, 10 ** 9, seq_ctx) if ctx_out else None
    return y_l, y_c


GDN_IN_TILE = 8 * LANES


def _gdn_in_kernel(x_ref, xb_ref, xa_ref, g_ref, sh_ref, sc_ref, w_ref, cw_ref, wba_ref, o_ref, ba_ref, h_ref,
                   *, tiles_per_seq, n_kq, n_conv):
    i, j = pl.program_id(0), pl.program_id(1)
    tm = x_ref.shape[0]
    hb = ha = SUBLANES
    heads = o_ref.shape[0]

    @pl.when(j == 0)
    def _():
        first = i % tiles_per_seq == 0
        last = i % tiles_per_seq == tiles_per_seq - 1
        _fill_h(h_ref, x_ref, xb_ref, xa_ref, g_ref, sc_ref, sh_ref, first, last, hb, ha)

    pair = 2 * LANES

    def conv_silu(s2):
        p = _dot(h_ref[...], w_ref[:, s2 * pair:(s2 + 1) * pair])
        return jax.nn.silu(_conv_rows(p, cw_ref[:, s2 * pair:(s2 + 1) * pair], tm, 1, hb))

    @pl.when(j < n_kq)
    def _():
        scale = jnp.where(j >= n_kq // 2, GDN_DK ** -0.5, 1.0)
        for s2 in range(heads // 2):
            a = conv_silu(s2)
            for e in range(2):
                ah = a[:, e * LANES:(e + 1) * LANES]
                o_ref[2 * s2 + e] = (ah * (lax.rsqrt(jnp.sum(ah * ah, axis=-1, keepdims=True) + 1e-6) * scale)
                                     ).astype(o_ref.dtype)

    @pl.when((j >= n_kq) & (j < n_conv))
    def _():
        for s2 in range(heads // 2):
            a = conv_silu(s2)
            for e in range(2):
                o_ref[2 * s2 + e] = a[:, e * LANES:(e + 1) * LANES].astype(o_ref.dtype)

    @pl.when(j >= n_conv)
    def _():
        for s2 in range(heads // 2):
            p = _dot(h_ref[hb:hb + tm, :], w_ref[:, s2 * pair:(s2 + 1) * pair])
            for e in range(2):
                o_ref[2 * s2 + e] = p[:, e * LANES:(e + 1) * LANES].astype(o_ref.dtype)

    @pl.when(j == pl.num_programs(1) - 1)
    def _():
        ba_ref[...] = _dot(h_ref[hb:hb + tm, :], wba_ref[...])


def _gdn_in(x, modt, tiles_per_mod, norm_g, w_all, cw_all, w_ba, seq_rows):
    n_rows, d = x.shape
    n_out = w_all.shape[1]
    tm = _row_tile(seq_rows)
    tn = GDN_IN_TILE
    hb = ha = SUBLANES
    heads = tn // LANES
    n_kq = 2 * GDN_K_HEADS * GDN_DK // tn
    n_conv = n_kq + GDN_V_HEADS * GDN_DV // tn
    kern = functools.partial(_gdn_in_kernel, tiles_per_seq=seq_rows // tm, n_kq=n_kq, n_conv=n_conv)
    specs = _halo_specs(tm, hb, ha, d, n_rows) + [
        _const_spec((1, d)), _mod_spec(0, d, tiles_per_mod), _mod_spec(1, d, tiles_per_mod),
        pl.BlockSpec((d, tn), lambda i, j: (0, j)), pl.BlockSpec((cw_all.shape[0], tn), lambda i, j: (0, j)),
        _const_spec(w_ba.shape)]
    return pl.pallas_call(
        kern,
        grid=(n_rows // tm, n_out // tn),
        in_specs=specs,
        out_specs=[pl.BlockSpec((heads, tm, LANES), lambda i, j: (j, i, 0)), pl.BlockSpec((tm, LANES), lambda i, j: (i, 0))],
        out_shape=[jax.ShapeDtypeStruct((n_out // LANES, n_rows, LANES), BF16), jax.ShapeDtypeStruct((n_rows, LANES), F32)],
        scratch_shapes=[pltpu.VMEM((tm + hb + ha, d), BF16)],
        compiler_params=_cparams("parallel", "arbitrary"),
        name="gdn_in",
    )(x, x, x, norm_g.reshape(1, d), modt, modt, w_all, cw_all, w_ba)


def _split3(a):
    hi = a.astype(BF16)
    lo = (a - hi.astype(F32)).astype(BF16)
    return hi, lo


def _dot3s(a, b):
    ah, al = a
    bh, bl = b
    m = ah.shape[0]
    both = _dot(jnp.concatenate([ah, al], axis=0), bh)
    return both[:m] + (both[m:] + _dot(ah, bl))


def _cumsum_rows(x, rev):
    n = x.shape[0]
    row = lax.broadcasted_iota(jnp.int32, x.shape, 0)
    shift = 1
    while shift < n:
        if rev:
            x = x + jnp.where(row < n - shift, pltpu.roll(x, n - shift, axis=0), 0.0)
        else:
            x = x + jnp.where(row >= shift, pltpu.roll(x, shift, axis=0), 0.0)
        shift *= 2
    return x


GDN_GROUP = 4


def _gdn_scan_kernel(kf_ref, qf_ref, vf_ref, baf_ref, kb_ref, qb_ref, vb_ref, bab_ref, an_ref, dtb_ref,
                     s0_ref, of_ref, ob_ref, sf_ref, s_ref, *, emit):
    cidx = pl.program_id(1)
    c = kf_ref.shape[1]
    rep = GDN_V_HEADS // GDN_K_HEADS
    dir_refs = ((kf_ref, qf_ref, vf_ref, baf_ref, of_ref), (kb_ref, qb_ref, vb_ref, bab_ref, ob_ref))

    @pl.when(cidx == 0)
    def _():
        s_ref[...] = s0_ref[...]
        if not emit:
            of_ref[...] = jnp.zeros_like(of_ref)
            ob_ref[...] = jnp.zeros_like(ob_ref)

    def expand(src, first_lane, g):
        cols = [jnp.broadcast_to(src[:, first_lane + gs * g + h:first_lane + gs * g + h + 1], (c, GDN_DV)) for h in range(gs)]
        per = LANES // c
        narrow = []
        for p in range(gs // per):
            blk = cols[per * p + per - 1]
            for h in range(per - 2, -1, -1):
                blk = jnp.where(lane_in_vreg < (h + 1) * c, cols[per * p + h], blk)
            narrow.append(blk)
        return jnp.concatenate(cols, axis=1), jnp.concatenate(narrow, axis=1)

    gs = GDN_GROUP
    lane_in_vreg = lax.broadcasted_iota(jnp.int32, (c, LANES), 1)
    wide = gs * c
    ii = lax.broadcasted_iota(jnp.int32, (c, wide), 0)
    jj = lax.broadcasted_iota(jnp.int32, (c, wide), 1) % c
    br = lax.broadcasted_iota(jnp.int32, (wide, wide), 0) // c
    bc = lax.broadcasted_iota(jnp.int32, (wide, wide), 1) // c
    blockmask = (br == bc).astype(F32).astype(BF16)
    nk, nv = GDN_K_HEADS, GDN_V_HEADS
    zc = jnp.zeros((c, GDN_DK), BF16)
    zs = jnp.zeros((GDN_DK, GDN_DV), BF16)
    zr = jnp.zeros((c, GDN_DV), BF16)

    gcs, bts, incl, strict, eye, last = [], [], [], [], [], []
    for d in range(2):
        rev = d == 1
        ba = dir_refs[d][3][...]
        z = ba + dtb_ref[d]
        sp = jnp.maximum(z, 0.0) + jnp.log(1.0 + jnp.exp(-jnp.abs(z)))
        gcs.append(_cumsum_rows(an_ref[d] * sp, rev))
        bts.append(jax.nn.sigmoid(ba))
        diff = (jj - ii) if rev else (ii - jj)
        incl.append(diff >= 0)
        strict.append(diff > 0)
        eye.append((diff == 0).astype(F32))
        last.append(0 if rev else c - 1)

    def bdiag(blocks, zero):
        n = len(blocks)
        return jnp.concatenate([jnp.concatenate([blocks[r] if r == q else zero for q in range(n)], axis=1)
                                for r in range(n)], axis=0)

    def bd_tile(a16):
        return jnp.concatenate([a16] * gs, axis=0) * blockmask

    def dot3_bd(lhs, rhs_hi, rhs_lo):
        lh, ll = _split3(lhs)
        m = lhs.shape[0]
        both = _dot(jnp.concatenate([lh, ll], axis=0), rhs_hi)
        return both[:m] + (both[m:] + _dot(lh, rhs_lo))

    pieces = lambda a: [a[:, h * GDN_DV:(h + 1) * GDN_DV] for h in range(gs)]
    chains = [(d, g) for g in range(nv // gs) for d in range(2)]
    ks = [[dir_refs[d][0][h].astype(F32) for h in range(nk)] for d in range(2)]
    qs = [[dir_refs[d][1][h].astype(F32) for h in range(nk)] for d in range(2)] if emit else None
    vs = [[dir_refs[d][2][h].astype(F32) for h in range(nv)] for d in range(2)]
    s_olds = [[s_ref[d, h] for h in range(nv)] for d in range(2)]

    xs, ts, attns, gc5s, bt5s = [], [], [], [], []
    for d, g in chains:
        k0, k1 = ks[d][2 * g], ks[d][2 * g + 1]
        k016, k116 = k0.astype(BF16), k1.astype(BF16)
        rhs_t = jnp.concatenate([jnp.concatenate([k016, zc], axis=1)] * rep
                                + [jnp.concatenate([zc, k116], axis=1)] * rep, axis=0)
        if emit:
            lhs = jnp.concatenate([jnp.concatenate([k0, qs[d][2 * g]], axis=0),
                                   jnp.concatenate([k1, qs[d][2 * g + 1]], axis=0)], axis=1).astype(BF16)
        else:
            lhs = jnp.concatenate([k016, k116], axis=1)
        gram = _dot_nt(lhs, rhs_t)
        gc5, gcm = expand(gcs[d], d * 2 * nv + nv, g)
        bt5, btm = expand(bts[d], d * 2 * nv, g)
        gc5s.append(gc5)
        bt5s.append(bt5)
        grow = jnp.sum(gcm * eye[d], axis=0, keepdims=True)
        dec = jnp.where(incl[d], jnp.exp(jnp.where(incl[d], gcm - grow, 0.0)), 0.0)
        x = jnp.where(strict[d], -(gram[:c] * btm) * dec, 0.0)
        xs.append(x)
        ts.append(eye[d] + x)
        attns.append(gram[c:] * dec if emit else None)

    splits = [_split3(x) for x in xs]
    xs = [dot3_bd(x, bd_tile(xh), bd_tile(xl)) for x, (xh, xl) in zip(xs, splits)]
    nlev = int(math.log2(c)) - 1
    for lvl in range(1, nlev + 1):
        splits = [_split3(x) for x in xs]
        if lvl < nlev:
            ps = [dot3_bd(jnp.concatenate([t, x], axis=0), bd_tile(xh), bd_tile(xl))
                  for t, x, (xh, xl) in zip(ts, xs, splits)]
            ts = [t + p[:c] for t, p in zip(ts, ps)]
            xs = [p[c:] for p in ps]
        else:
            ts = [t + dot3_bd(t, bd_tile(xh), bd_tile(xl)) for t, (xh, xl) in zip(ts, splits)]

    kcats, p1s = [], []
    for n, (d, g) in enumerate(chains):
        k0, k1 = ks[d][2 * g], ks[d][2 * g + 1]
        gc5, bt5 = gc5s[n], bt5s[n]
        eg5 = jnp.exp(gc5)
        kcat = jnp.concatenate([k0] * rep + [k1] * rep, axis=1)
        kbe = kcat * bt5 * eg5
        s16 = [s_olds[d][gs * g + h].astype(BF16) for h in range(gs)]
        if emit:
            qe = jnp.concatenate([qs[d][2 * g]] * rep + [qs[d][2 * g + 1]] * rep, axis=1) * eg5
            top = jnp.concatenate([kbe, qe], axis=0).astype(BF16)
        else:
            top = kbe.astype(BF16)
        p1s.append(jnp.concatenate([_dot(top[:, 2 * pr * GDN_DK:(2 * pr + 2) * GDN_DK], bdiag(s16[2 * pr:2 * pr + 2], zs))
                                    for pr in range(gs // 2)], axis=1))
        kcats.append(kcat)

    bdvs = []
    for n, (d, g) in enumerate(chains):
        vcat = jnp.concatenate(vs[d][gs * g:gs * (g + 1)], axis=1)
        rh, rl = _split3(vcat * bt5s[n] - p1s[n][:c])
        vn16 = dot3_bd(ts[n], bdiag(pieces(rh), zr), bdiag(pieces(rl), zr)).astype(BF16)
        bdvs.append(bdiag(pieces(vn16), zr))

    outs = [[None] * nv for _ in range(2)]
    s_news = [[None] * nv for _ in range(2)]
    for n, (d, g) in enumerate(chains):
        gl5 = gc5s[n][last[d]:last[d] + 1, :]
        if emit:
            o = pieces(p1s[n][c:] + _dot(attns[n].astype(BF16), bdvs[n]))
        kd = (kcats[n] * jnp.exp(gl5 - gc5s[n])).astype(BF16)
        sn = _dot_tn(jnp.concatenate(pieces(kd), axis=0), bdvs[n])
        egl = jnp.exp(gl5)
        for h in range(gs):
            cols = slice(h * GDN_DV, (h + 1) * GDN_DV)
            s_news[d][gs * g + h] = s_olds[d][gs * g + h] * egl[:, cols] + sn[:, cols]
            if emit:
                outs[d][gs * g + h] = o[h]
    for d in range(2):
        for hv in range(nv):
            if emit:
                dir_refs[d][4][hv] = outs[d][hv].astype(dir_refs[d][4].dtype)
            s_ref[d, hv] = s_news[d][hv]

    @pl.when(cidx == pl.num_programs(1) - 1)
    def _():
        sf_ref[...] = s_ref[...]


def _gdn_scan(kqvz, ba, an, dtb, s0, batch, seq_rows, *, emit=True):
    c = min(GDN_CHUNK, seq_rows)
    nc = seq_rows // c
    n_rows = kqvz.shape[1]
    rows = (lambda b, cc: b * nc + cc, lambda b, cc: b * nc + nc - 1 - cc)
    kh, vh = GDN_K_HEADS, GDN_V_HEADS
    st_spec = pl.BlockSpec((2, None, vh, GDN_DK, GDN_DV), lambda b, cc: (0, b, 0, 0, 0))
    full = lambda a: pl.BlockSpec(a.shape, lambda b, cc: (0,) * a.ndim)

    def dir_specs(rw):
        return [pl.BlockSpec((kh, c, LANES), lambda b, cc: (0, rw(b, cc), 0)),
                pl.BlockSpec((kh, c, LANES), lambda b, cc: (1, rw(b, cc), 0)),
                pl.BlockSpec((vh, c, LANES), lambda b, cc: (1, rw(b, cc), 0)),
                pl.BlockSpec((c, LANES), lambda b, cc: (rw(b, cc), 0))]

    def o_spec(rw):
        return pl.BlockSpec((vh, c, LANES), (lambda b, cc: (0, rw(b, cc), 0)) if emit else (lambda b, cc: (0, 0, 0)))

    o_shape = jax.ShapeDtypeStruct((vh, n_rows if emit else c, LANES), BF16)
    of, ob, sf = pl.pallas_call(
        functools.partial(_gdn_scan_kernel, emit=emit),
        grid=(batch, nc),
        in_specs=dir_specs(rows[0]) + dir_specs(rows[1]) + [full(an), full(dtb), st_spec],
        out_specs=[o_spec(rows[0]), o_spec(rows[1]), st_spec],
        out_shape=[o_shape, o_shape, jax.ShapeDtypeStruct(s0.shape, F32)],
        scratch_shapes=[pltpu.VMEM((2, vh, GDN_DK, GDN_DV), F32)],
        compiler_params=_cparams("arbitrary", "arbitrary"),
        name="gdn_scan",
    )(kqvz, kqvz, kqvz, ba, kqvz, kqvz, kqvz, ba, an, dtb, s0)
    return of, ob, sf


def _gdn_out_prologue(of_ref, ob_ref, z_ref, ng_ref):
    parts = []
    for h in range(GDN_V_HEADS):
        o = of_ref[h].astype(F32) + ob_ref[h].astype(F32)
        on = o * lax.rsqrt(jnp.mean(o * o, axis=-1, keepdims=True) + NORM_EPS) * ng_ref[...]
        parts.append(on * jax.nn.silu(z_ref[h].astype(F32)))
    return jnp.concatenate(parts, axis=-1)


def _gdn_mixer(x, x_ctx, modt, modt_ctx, norm_g, prm, batch, seq, seq_ctx, *, ctx_out):
    d = x.shape[-1]
    qk, vv = GDN_K_HEADS * GDN_DK, GDN_V_HEADS * GDN_DV
    ng2 = 2 * 2 * GDN_V_HEADS
    w_in, conv_w = prm['w_in'], prm['conv_w']
    w_all = jnp.concatenate([w_in[:, :qk], w_in[:, qk + vv + ng2:2 * qk + vv + ng2], w_in[:, qk:qk + vv],
                             w_in[:, 2 * qk + vv + ng2:]], axis=1).astype(BF16)
    w_ba = jnp.concatenate([w_in[:, qk + vv:qk + vv + ng2], jnp.zeros((d, LANES - ng2), w_in.dtype)], axis=1).astype(BF16)
    cw_all = jnp.concatenate([conv_w[:, :qk], conv_w[:, qk + vv:], conv_w[:, qk:qk + vv],
                              jnp.zeros((conv_w.shape[0], vv), conv_w.dtype)], axis=1)
    tiles_per_mod = seq // _row_tile(seq)
    p_l, ba_l = _gdn_in(x, modt, tiles_per_mod, norm_g, w_all, cw_all, w_ba, seq)
    p_c, ba_c = _gdn_in(x_ctx, modt_ctx, 10 ** 9, norm_g, w_all, cw_all, w_ba, seq_ctx)
    s0 = jnp.zeros((2, batch, GDN_V_HEADS, GDN_DK, GDN_DV), F32)

    def gate_lanes(p):
        rows = [jnp.zeros((LANES,), F32).at[dr * 2 * GDN_V_HEADS + GDN_V_HEADS:(dr + 1) * 2 * GDN_V_HEADS].set(p[dr])
                for dr in range(2)]
        return jnp.stack(rows).reshape(2, 1, LANES)

    an = gate_lanes(-jnp.exp(prm['a_log'].astype(F32)))
    dtb = gate_lanes(prm['dt_bias'].astype(F32))
    ocf, ocb, sc = _gdn_scan(p_c, ba_c, an, dtb, s0, batch, seq_ctx, emit=ctx_out)
    olf, olb, _ = _gdn_scan(p_l, ba_l, an, dtb, sc, batch, seq)
    os_l, os_c = [olf, olb], [ocf, ocb]
    w_out = prm['w_out'].astype(BF16)
    ng = prm['norm_g'].reshape(1, GDN_DV)
    vh = GDN_V_HEADS
    z_blk = (2 * GDN_K_HEADS + vh) // vh

    def out(xa, pa, oa, mt, tpm, seq_rows):
        tm = _row_tile(seq_rows, cap=ROW_TILE_CAP // 2)
        hspec = lambda blk: pl.BlockSpec((vh, tm, LANES), lambda i, j: (blk, i, 0))
        specs = [hspec(0), hspec(0), hspec(z_blk), _const_spec((1, GDN_DV))]
        return _outproj(_gdn_out_prologue, [oa[0], oa[1], pa, ng], specs, vv, xa, mt, tpm * (_row_tile(seq_rows) // tm),
                        w_out, tm=tm, name="gdn_out")

    y_l = out(x, p_l, os_l, modt, tiles_per_mod, seq)
    y_c = out(x_ctx, p_c, os_c, modt_ctx, 10 ** 9, seq_ctx) if ctx_out else None
    return y_l, y_c


def kernel(x, c, ctx, c_ctx, norm1_g, norm2_g, mod_w, mod_b, ffn_w_in, ffn_conv_w, ffn_conv_b, ffn_w_out, s5_lam_re, s5_lam_im, s5_log_step, s5_b_re, s5_b_im, s5_c_re, s5_c_im, s5_d, s5_w_glu, s5_b_glu, lru_w_in, lru_conv_w, lru_conv_b, lru_w_a, lru_b_a, lru_w_x, lru_b_x, lru_lam, lru_w_out, ret_w_in, ret_norm_g, ret_w_out, gdn_w_in, gdn_conv_w, gdn_a_log, gdn_dt_bias, gdn_norm_g, gdn_w_out, final_norm_g):
    B, L, D = x.shape
    Lc = ctx.shape[1]
    depth = mod_w.shape[0]
    assert B == SUBLANES and depth == 4, "time-major layers put the batch on the 8 sublanes; four mixer kinds"
    c16 = jnp.concatenate([c, c_ctx[None], jnp.zeros((16 - B - 1, D), F32)], 0)
    mods = _modulation(c16, mod_w, mod_b)

    def ffn(i, xa, mt, tiles_per_mod, rs, seq_rows, final_norm=False):
        return _ffn(xa, mt, tiles_per_mod, norm2_g[i], ffn_w_in[i].astype(BF16), ffn_conv_w[i], ffn_conv_b[i],
                    ffn_w_out[i].astype(BF16), final_norm_g, rs=rs, seq_rows=seq_rows, final_norm=final_norm)

    xt = jnp.swapaxes(x, 0, 1).reshape(L * B, D)
    ct = jnp.swapaxes(ctx, 0, 1).reshape(Lc * B, D)
    s5p = dict(lam_re=s5_lam_re[0], lam_im=s5_lam_im[0], log_step=s5_log_step[0], b_re=s5_b_re[0], b_im=s5_b_im[0],
               c_re=s5_c_re[0], c_im=s5_c_im[0], d=s5_d[0], w_glu=s5_w_glu[0], b_glu=s5_b_glu[0])
    lrup = dict(w_in=lru_w_in[0], conv_w=lru_conv_w[0], conv_b=lru_conv_b[0], w_a=lru_w_a[0], b_a=lru_b_a[0],
                w_x=lru_w_x[0], b_x=lru_b_x[0], lam=lru_lam[0], w_out=lru_w_out[0])
    for i, (mixer, prm) in enumerate(((_s5_mixer, s5p), (_lru_mixer, lrup))):
        mt = mods[i, :B][None]
        mtc = jnp.broadcast_to(mods[i, B][None, None, :], (1, SUBLANES, N_MOD * D))
        xt, ct = mixer(xt, ct, mt, mtc, norm1_g[i], prm, ctx_out=True)
        xt = ffn(i, xt, mt, 10 ** 9, SUBLANES, L * B)
        ct = ffn(i, ct, mtc, 10 ** 9, SUBLANES, Lc * B)

    xb = jnp.swapaxes(xt.reshape(L, B, D), 0, 1).reshape(B * L, D)
    cb = jnp.swapaxes(ct.reshape(Lc, B, D), 0, 1).reshape(B * Lc, D)
    retp = dict(w_in=ret_w_in[0], norm_g=ret_norm_g[0], w_out=ret_w_out[0])
    gdnp = dict(w_in=gdn_w_in[0], conv_w=gdn_conv_w[0], a_log=gdn_a_log[0], dt_bias=gdn_dt_bias[0],
                norm_g=gdn_norm_g[0], w_out=gdn_w_out[0])
    tiles_per_batch = L // _row_tile(L)
    for i, (mixer, prm) in ((2, (_ret_mixer, retp)), (3, (_gdn_mixer, gdnp))):
        ctx_out = i < depth - 1
        mt = jnp.broadcast_to(mods[i, :B][:, None, :], (B, SUBLANES, N_MOD * D))
        mtc = jnp.broadcast_to(mods[i, B][None, None, :], (1, SUBLANES, N_MOD * D))
        xb, cb_new = mixer(xb, cb, mt, mtc, norm1_g[i], prm, B, L, Lc, ctx_out=ctx_out)
        xb = ffn(i, xb, mt, tiles_per_batch, 1, L, final_norm=not ctx_out)
        if ctx_out:
            cb = ffn(i, cb_new, mtc, 10 ** 9, 1, Lc)
    return xb.reshape(B, L, D)
```

```python
import functools
import math

import numpy as np
import jax
import jax.numpy as jnp
from jax import lax
from jax.experimental import pallas as pl
from jax.experimental.pallas import tpu as pltpu

F32 = jnp.float32
BF16 = jnp.bfloat16

NORM_EPS = 1e-6
SUBLANES = 8
LANES = 128
VMEM_LIMIT_BYTES = 56 * 1024 * 1024
ROW_TILE_CAP = 1024

S5_GROUP = 16
S5_STATE = 64
S5_GB = 16
LRU_BLOCKS = 4
LRU_C = 8.0
RET_HEADS = 4
RET_DK = 256
RET_DV = 512
RET_CHUNK = 128
ROPE_BASE = 10000.0
GRID_W = 64
GDN_K_HEADS = 8
GDN_V_HEADS = 16
GDN_DK = 128
GDN_DV = 128
GDN_CHUNK = 64
N_MOD = 6


def _cparams(*sem):
    return pltpu.CompilerParams(dimension_semantics=sem, vmem_limit_bytes=VMEM_LIMIT_BYTES)


def _row_tile(rows, cap=None):
    cap = ROW_TILE_CAP if cap is None else cap
    tm = min(cap, rows)
    assert rows % tm == 0 and tm % SUBLANES == 0, (rows, tm)
    return tm


def _dot(a, b):
    return jnp.dot(a, b, preferred_element_type=F32)


def _dot_tn(a, b):
    return lax.dot_general(a, b, (((0,), (0,)), ((), ())), preferred_element_type=F32)


def _dot_nt(a, b):
    return lax.dot_general(a, b, (((1,), (1,)), ((), ())), preferred_element_type=F32)


def _gelu(x):
    return jax.nn.gelu(x, approximate=True)


def _sigmoid(x):
    return 1.0 / (1.0 + jnp.exp(-x))


def _silu(x):
    return x * _sigmoid(x)


def _tile8(v, rows):
    n = v.shape[-1]
    return jnp.broadcast_to(v[None], (rows // SUBLANES, SUBLANES, n)).reshape(rows, n)


def _norm_mod(x, g, sc, sh):
    ms = jnp.mean(x * x, axis=-1, keepdims=True)
    y = x * lax.rsqrt(ms + NORM_EPS) * g
    rows = x.shape[0]
    return y * (1.0 + _tile8(sc, rows)) + _tile8(sh, rows)


def _shift_rows(p, off, rows):
    if off % SUBLANES == 0:
        return p[off:off + rows]
    n = p.shape[0]
    base = (off // SUBLANES) * SUBLANES
    rolled = pltpu.roll(p, (n - (off - base)) % n, axis=0)
    return rolled[base:base + rows]


def _conv_rows(p_ext, cw, rows, rs, hb):
    k_taps = cw.shape[0]
    acc = None
    for k in range(k_taps):
        term = cw[k:k + 1, :] * _shift_rows(p_ext, hb + (k - k_taps // 2) * rs, rows)
        acc = term if acc is None else acc + term
    return acc


def _fill_h(h_ref, x_ref, xb_ref, xa_ref, g_ref, sc_ref, sh_ref, first, last, hb, ha):
    tm = x_ref.shape[0]
    g, sc, sh = g_ref[...], sc_ref[...], sh_ref[...]
    h_ref[hb:hb + tm, :] = _norm_mod(x_ref[...], g, sc, sh).astype(BF16)
    if hb:
        hbv = _norm_mod(xb_ref[...], g, sc, sh)
        h_ref[0:hb, :] = jnp.where(first, 0.0, hbv).astype(BF16)
    if ha:
        hav = _norm_mod(xa_ref[...], g, sc, sh)
        h_ref[hb + tm:hb + tm + ha, :] = jnp.where(last, 0.0, hav).astype(BF16)


def _halo_specs(tm, hb, ha, d, n_rows):
    specs = [pl.BlockSpec((tm, d), lambda i, j: (i, 0))]
    if hb:
        specs.append(pl.BlockSpec((hb, d), lambda i, j: (jnp.maximum(i * (tm // hb) - 1, 0), 0)))
    if ha:
        nblk = n_rows // ha
        specs.append(pl.BlockSpec((ha, d), lambda i, j: (jnp.minimum((i + 1) * (tm // ha), nblk - 1), 0)))
    return specs


def _mod_spec(chunk, d, tiles_per_mod):
    return pl.BlockSpec((None, SUBLANES, d), lambda i, j: (i // tiles_per_mod, 0, chunk))


def _mod_kernel(c_ref, w_ref, b_ref, o_ref):
    a = _silu(c_ref[...]).astype(BF16)
    o_ref[...] = _dot(a, w_ref[...].astype(BF16)) + b_ref[...]


def _modulation(c16, mod_w, mod_b):
    depth, d, n = mod_w.shape
    tn = n // 4
    return pl.pallas_call(
        _mod_kernel,
        grid=(depth, n // tn),
        in_specs=[pl.BlockSpec((16, d), lambda l, j: (0, 0)),
                  pl.BlockSpec((None, d, tn), lambda l, j: (l, 0, j)),
                  pl.BlockSpec((None, 1, tn), lambda l, j: (l, 0, j))],
        out_specs=pl.BlockSpec((None, 16, tn), lambda l, j: (l, 0, j)),
        out_shape=jax.ShapeDtypeStruct((depth, 16, n), F32),
        compiler_params=_cparams("arbitrary", "arbitrary"),
        name="modulation",
    )(c16, mod_w, mod_b.reshape(depth, 1, n))


FFN_TILE = 2 * LANES


def _ffn_kernel(x_ref, xb_ref, xa_ref, g_ref, sh_ref, sc_ref, gt_ref, win_ref, cw_ref, cb_ref, wo_ref, fg_ref, o_ref, h_ref,
                *, rs, tiles_per_seq, final_norm):
    i = pl.program_id(0)
    tm = x_ref.shape[0]
    hb = ha = SUBLANES
    f = wo_ref.shape[0]
    first = i % tiles_per_seq == 0
    last = i % tiles_per_seq == tiles_per_seq - 1
    _fill_h(h_ref, x_ref, xb_ref, xa_ref, g_ref, sc_ref, sh_ref, first, last, hb, ha)
    acc = None
    for j in range(f // FFN_TILE):
        cols = slice(j * FFN_TILE, (j + 1) * FFN_TILE)
        gate = _dot(h_ref[...], win_ref[:, cols])
        up = _dot(h_ref[hb:hb + tm, :], win_ref[:, f + j * FFN_TILE:f + (j + 1) * FFN_TILE])
        gc = _conv_rows(gate, cw_ref[:, cols], tm, rs, hb) + cb_ref[:, cols]
        part = _dot((_gelu(gc) * up).astype(BF16), wo_ref[cols, :])
        acc = part if acc is None else acc + part
    y = x_ref[...] + _tile8(gt_ref[...], tm) * acc
    if final_norm:
        ms = jnp.mean(y * y, axis=-1, keepdims=True)
        y = y * lax.rsqrt(ms + NORM_EPS) * fg_ref[...]
    o_ref[...] = y


def _ffn(x, modt, tiles_per_mod, norm_g, w_in, conv_w, conv_b, w_out, final_g, *, rs, seq_rows, final_norm=False):
    n_rows, d = x.shape
    f = w_out.shape[0]
    tm = _row_tile(seq_rows)
    assert f % FFN_TILE == 0
    hb = ha = SUBLANES
    kern = functools.partial(_ffn_kernel, rs=rs, tiles_per_seq=seq_rows // tm, final_norm=final_norm)
    resident = lambda shape: pl.BlockSpec(shape, lambda i, j: (0, 0), pipeline_mode=pl.Buffered(1))
    in_specs = _halo_specs(tm, hb, ha, d, n_rows) + [
        pl.BlockSpec((1, d), lambda i, j: (0, 0)),
        _mod_spec(3, d, tiles_per_mod), _mod_spec(4, d, tiles_per_mod), _mod_spec(5, d, tiles_per_mod),
        resident(w_in.shape), resident(conv_w.shape), resident((1, f)), resident(w_out.shape),
        pl.BlockSpec((1, d), lambda i, j: (0, 0)),
    ]
    return pl.pallas_call(
        kern,
        grid=(n_rows // tm, 1),
        in_specs=in_specs,
        out_specs=pl.BlockSpec((tm, d), lambda i, j: (i, 0)),
        out_shape=jax.ShapeDtypeStruct((n_rows, d), F32),
        scratch_shapes=[pltpu.VMEM((tm + hb + ha, d), BF16)],
        compiler_params=_cparams("parallel", "arbitrary"),
        name="conv_ffn",
    )(x, x, x, norm_g.reshape(1, d), modt, modt, modt, w_in, conv_w, conv_b.reshape(1, f), w_out, final_g.reshape(1, d))


def _outproj_kernel(*refs, n_pro, prologue, glu):
    pro_refs = refs[:n_pro]
    if glu:
        x_ref, gt_ref, w_ref, b_ref, o_ref = refs[n_pro:]
    else:
        x_ref, gt_ref, w_ref, o_ref = refs[n_pro:]
    d = o_ref.shape[1]
    a = prologue(*pro_refs).astype(BF16)
    y = _dot(a, w_ref[:, 0:d])
    if glu:
        y = (y + b_ref[:, 0:d]) * jax.nn.sigmoid(_dot(a, w_ref[:, d:2 * d]) + b_ref[:, d:2 * d])
    o_ref[...] = x_ref[...] + _tile8(gt_ref[...], o_ref.shape[0]) * y


def _outproj(prologue, pro_args, pro_specs, k_dim, x, modt, tiles_per_mod, w, bias=None, *, tm, name):
    n_rows, d = x.shape
    glu = bias is not None
    kern = functools.partial(_outproj_kernel, n_pro=len(pro_args), prologue=prologue, glu=glu)
    specs = list(pro_specs) + [pl.BlockSpec((tm, d), lambda i, j: (i, 0)),
                               pl.BlockSpec((None, SUBLANES, d), lambda i, j: (i // tiles_per_mod, 0, 2)),
                               pl.BlockSpec(w.shape, lambda i, j: (0, 0))]
    args = list(pro_args) + [x, modt, w]
    if glu:
        specs += [pl.BlockSpec(bias.shape, lambda i, j: (0, 0))]
        args += [bias]
    return pl.pallas_call(
        kern,
        grid=(n_rows // tm, 1),
        in_specs=specs,
        out_specs=pl.BlockSpec((tm, d), lambda i, j: (i, 0)),
        out_shape=jax.ShapeDtypeStruct((n_rows, d), F32),
        compiler_params=_cparams("parallel", "arbitrary"),
        name=name,
    )(*args)


def _row_spec(tm, n):
    return pl.BlockSpec((tm, n), lambda i, j: (i, 0))


def _const_spec(shape):
    nd = len(shape)
    return pl.BlockSpec(shape, lambda i, j: (0,) * nd)


def _s5_weights(lam_re, lam_im, log_step, b_re, b_im, c_re, c_im):
    g, p = lam_re.shape
    gc = b_re.shape[-1]
    lr = jnp.minimum(lam_re.astype(F32), -1e-4)
    li = lam_im.astype(F32)
    step = jnp.exp(log_step.astype(F32))[:, None]
    mag = jnp.exp(lr * step)
    ar, ai = mag * jnp.cos(li * step), mag * jnp.sin(li * step)
    den = lr * lr + li * li
    kr = ((ar - 1.0) * lr + ai * li) / den
    ki = (ai * lr - (ar - 1.0) * li) / den
    br32, bi32 = b_re.astype(F32), b_im.astype(F32)
    br = kr[..., None] * br32 - ki[..., None] * bi32
    bi = kr[..., None] * bi32 + ki[..., None] * br32
    nb = g // S5_GB
    eye = jnp.eye(S5_GB, dtype=F32)

    def pack_b(b):
        b4 = b.reshape(nb, S5_GB, p, gc)
        return jnp.einsum('blpc,lm->blcmp', b4, eye).reshape(nb, S5_GB * gc, S5_GB * p)

    def pack_c(cm):
        c4 = cm.astype(F32).reshape(nb, S5_GB, gc, p)
        return jnp.einsum('blcp,lm->blpmc', c4, eye).reshape(nb, S5_GB * p, S5_GB * gc)

    bm = jnp.concatenate([pack_b(br), pack_b(bi)], axis=-1).astype(BF16)
    cm = jnp.concatenate([pack_c(c_re), -pack_c(c_im)], axis=1).astype(BF16)
    lam = jnp.stack([ar.reshape(nb, S5_GB * p), ai.reshape(nb, S5_GB * p)], axis=1)
    lam = jnp.broadcast_to(lam[:, :, None, :], (nb, 2, SUBLANES, S5_GB * p))
    return bm, cm, lam


def _s5_scan_kernel(x_ref, g_ref, sh_ref, sc_ref, bm_ref, cm_ref, lam_ref, s0_ref, y_ref, sf_ref, st_ref, *bu_refs,
                    rev, emit):
    i = pl.program_id(0)
    tm = x_ref.shape[0]
    nt = tm // SUBLANES
    nb, kin, ns2 = bm_ref.shape
    ns = ns2 // 2

    @pl.when(i == 0)
    def _():
        st_ref[...] = s0_ref[...]
        if not emit:
            y_ref[...] = jnp.zeros_like(y_ref)

    h = _norm_mod(x_ref[...], g_ref[...], sc_ref[...], sh_ref[...]).astype(BF16)
    for gb in range(nb):
        bu_ref = bu_refs[gb]
        bu_ref[...] = _dot(h[:, gb * kin:(gb + 1) * kin], bm_ref[gb])
        ar, ai = lam_ref[gb, 0], lam_ref[gb, 1]
        sr, si = st_ref[gb, 0], st_ref[gb, 1]
        for t in (range(nt - 1, -1, -1) if rev else range(nt)):
            rows = slice(t * SUBLANES, (t + 1) * SUBLANES)
            sr, si = (ar * sr - ai * si + bu_ref[rows, 0:ns], ar * si + ai * sr + bu_ref[rows, ns:ns2])
            if emit:
                bu_ref[rows, 0:ns] = sr
                bu_ref[rows, ns:ns2] = si
        st_ref[gb, 0] = sr
        st_ref[gb, 1] = si
        if emit:
            y_ref[:, gb * kin:(gb + 1) * kin] = _dot(bu_ref[...].astype(BF16), cm_ref[gb])

    @pl.when(i == pl.num_programs(0) - 1)
    def _():
        sf_ref[...] = st_ref[...]


def _s5_scan(x, modt, norm_g, bm, cm, lam, s0, *, rev, emit=True):
    n_rows, d = x.shape
    tm = _row_tile(n_rows, cap=ROW_TILE_CAP // 2)
    ntile = n_rows // tm
    tile = (lambda i: (ntile - 1 - i, 0)) if rev else (lambda i: (i, 0))
    mod = lambda k: pl.BlockSpec((None, SUBLANES, d), lambda i: (0, 0, k))
    full = lambda a: pl.BlockSpec(a.shape, lambda i: (0,) * a.ndim)
    y, sf = pl.pallas_call(
        functools.partial(_s5_scan_kernel, rev=rev, emit=emit),
        grid=(ntile,),
        in_specs=[pl.BlockSpec((tm, d), tile), pl.BlockSpec((1, d), lambda i: (0, 0)), mod(0), mod(1),
                  full(bm), full(cm), full(lam), full(s0)],
        out_specs=[pl.BlockSpec((tm, d), tile if emit else (lambda i: (0, 0))), full(s0)],
        out_shape=[jax.ShapeDtypeStruct((n_rows if emit else tm, d), F32), jax.ShapeDtypeStruct(s0.shape, F32)],
        scratch_shapes=[pltpu.VMEM(s0.shape, F32)] + [pltpu.VMEM((tm, bm.shape[-1]), F32)] * bm.shape[0],
        compiler_params=_cparams("arbitrary"),
        name="s5_scan_bwd" if rev else "s5_scan_fwd",
    )(x, norm_g.reshape(1, d), modt, modt, bm, cm, lam, s0)
    return (y if emit else None), sf


def _s5_out_prologue(x_ref, yf_ref, yb_ref, g_ref, sh_ref, sc_ref, dk_ref):
    u = _norm_mod(x_ref[...], g_ref[...], sc_ref[...], sh_ref[...])
    return _gelu(yf_ref[...] + yb_ref[...] + dk_ref[...] * u)


def _s5_mixer(x, x_ctx, modt, modt_ctx, norm_g, prm, *, ctx_out):
    d = x.shape[-1]
    w = [_s5_weights(prm['lam_re'][dr], prm['lam_im'][dr], prm['log_step'][dr], prm['b_re'][dr], prm['b_im'][dr],
                     prm['c_re'][dr], prm['c_im'][dr]) for dr in range(2)]
    zero = jnp.zeros(w[0][2].shape, F32)
    ys, ys_ctx = [], []
    for dr in range(2):
        bm, cm, lam = w[dr]
        yc, sc = _s5_scan(x_ctx, modt_ctx, norm_g, bm, cm, lam, zero, rev=dr == 1, emit=ctx_out)
        yl, _ = _s5_scan(x, modt, norm_g, bm, cm, lam, sc, rev=dr == 1)
        ys.append(yl)
        ys_ctx.append(yc)
    w_glu = prm['w_glu'].astype(BF16)
    b_glu = prm['b_glu'].reshape(1, -1)

    def out(xa, ya, mt):
        tm = _row_tile(xa.shape[0], cap=ROW_TILE_CAP // 2)
        pro_args = [xa, ya[0], ya[1], norm_g.reshape(1, d), mt, mt, prm['d'].reshape(1, d)]
        pro_specs = [_row_spec(tm, d), _row_spec(tm, d), _row_spec(tm, d), _const_spec((1, d)),
                     _mod_spec(0, d, 10 ** 9), _mod_spec(1, d, 10 ** 9), _const_spec((1, d))]
        return _outproj(_s5_out_prologue, pro_args, pro_specs, d, xa, mt, 10 ** 9, w_glu, b_glu, tm=tm, name="s5_out")

    return out(x, ys, modt), (out(x_ctx, ys_ctx, modt_ctx) if ctx_out else None)


def _lru_in_kernel(x_ref, xb_ref, xa_ref, g_ref, sh_ref, sc_ref, wr_ref, wy_ref, cw_ref, cb_ref, xc_ref, yg_ref, h_ref,
                   *, rs, hb, ha):
    i, j = pl.program_id(0), pl.program_id(1)
    tm = x_ref.shape[0]

    @pl.when(j == 0)
    def _():
        _fill_h(h_ref, x_ref, xb_ref, xa_ref, g_ref, sc_ref, sh_ref, i == 0, i == pl.num_programs(0) - 1, hb, ha)

    p = _dot(h_ref[...], wr_ref[...])
    xc_ref[...] = _conv_rows(p, cw_ref[...], tm, rs, hb) + cb_ref[...]
    yg_ref[...] = _gelu(_dot(h_ref[hb:hb + tm, :], wy_ref[...]))


def _lru_in(x, modt, norm_g, w_in, conv_w, conv_b):
    n_rows, d = x.shape
    wd = conv_w.shape[-1]
    rs = SUBLANES
    hb, ha = 2 * rs, rs
    tm = _row_tile(n_rows)
    tn = 4 * LANES
    nj = wd // tn
    specs = _halo_specs(tm, hb, ha, d, n_rows) + [
        _const_spec((1, d)), _mod_spec(0, d, 10 ** 9), _mod_spec(1, d, 10 ** 9),
        pl.BlockSpec((d, tn), lambda i, j: (0, j)), pl.BlockSpec((d, tn), lambda i, j: (0, j + nj)),
        pl.BlockSpec((conv_w.shape[0], tn), lambda i, j: (0, j)), pl.BlockSpec((1, tn), lambda i, j: (0, j))]
    out_spec = pl.BlockSpec((tm, tn), lambda i, j: (i, j))
    return pl.pallas_call(
        functools.partial(_lru_in_kernel, rs=rs, hb=hb, ha=ha),
        grid=(n_rows // tm, nj),
        in_specs=specs,
        out_specs=[out_spec, out_spec],
        out_shape=[jax.ShapeDtypeStruct((n_rows, wd), F32)] * 2,
        scratch_shapes=[pltpu.VMEM((tm + hb + ha, d), BF16)],
        compiler_params=_cparams("parallel", "arbitrary"),
        name="lru_in",
    )(x, x, x, norm_g.reshape(1, d), modt, modt, w_in, w_in, conv_w, conv_b.reshape(1, wd))


def _lru_scan_kernel(xc_ref, wa_ref, wx_ref, ba_ref, bx_ref, nsp_ref, h0_ref, hs_ref, hf_ref, st_ref, a_ref, b_ref,
                     *, rev, emit):
    i = pl.program_id(0)
    tm = xc_ref.shape[0]
    nt = tm // SUBLANES
    nblk, bw, _ = wa_ref.shape

    @pl.when(i == 0)
    def _():
        st_ref[...] = h0_ref[...]
        if not emit:
            hs_ref[...] = jnp.zeros_like(hs_ref)

    for k in range(nblk):
        cols = slice(k * bw, (k + 1) * bw)
        xc = xc_ref[:, cols]
        xcb = xc.astype(BF16)
        r = jax.nn.sigmoid(_dot(xcb, wa_ref[k]) + ba_ref[:, cols])
        gi = jax.nn.sigmoid(_dot(xcb, wx_ref[k]) + bx_ref[:, cols])
        log_a = nsp_ref[:, cols] * r
        a = jnp.exp(log_a)
        a_ref[:, cols] = a
        b_ref[:, cols] = jnp.sqrt(1.0 - a * a) * (gi * xc)

    def step(t, h):
        tt = nt - 1 - t if rev else t
        r0 = pl.multiple_of(tt * SUBLANES, SUBLANES)
        h = a_ref[pl.ds(r0, SUBLANES), :] * h + b_ref[pl.ds(r0, SUBLANES), :]
        if emit:
            hs_ref[pl.ds(r0, SUBLANES), :] = h
        return h

    st_ref[...] = lax.fori_loop(0, nt, step, st_ref[...], unroll=8)

    @pl.when(i == pl.num_programs(0) - 1)
    def _():
        hf_ref[...] = st_ref[...]


def _lru_scan(xc, wa, wx, ba, bx, nsp, h0, *, rev, emit=True):
    n_rows, wd = xc.shape
    tm = _row_tile(n_rows)
    ntile = n_rows // tm
    tile = (lambda i: (ntile - 1 - i, 0)) if rev else (lambda i: (i, 0))
    full = lambda a: pl.BlockSpec(a.shape, lambda i: (0,) * a.ndim)
    hs, hf = pl.pallas_call(
        functools.partial(_lru_scan_kernel, rev=rev, emit=emit),
        grid=(ntile,),
        in_specs=[pl.BlockSpec((tm, wd), tile), full(wa), full(wx), full(ba), full(bx), full(nsp), full(h0)],
        out_specs=[pl.BlockSpec((tm, wd), tile if emit else (lambda i: (0, 0))), full(h0)],
        out_shape=[jax.ShapeDtypeStruct((n_rows if emit else tm, wd), F32), jax.ShapeDtypeStruct(h0.shape, F32)],
        scratch_shapes=[pltpu.VMEM(h0.shape, F32), pltpu.VMEM((tm, wd), F32), pltpu.VMEM((tm, wd), F32)],
        compiler_params=_cparams("arbitrary"),
        name="lru_scan_bwd" if rev else "lru_scan_fwd",
    )(xc, wa, wx, ba, bx, nsp, h0)
    return (hs if emit else None), hf


def _lru_out_prologue(yg_ref, hf_ref, hb_ref):
    return yg_ref[...] * (hf_ref[...] + hb_ref[...])


def _lru_mixer(x, x_ctx, modt, modt_ctx, norm_g, prm, *, ctx_out):
    d = x.shape[-1]
    w_in = prm['w_in'].astype(BF16)
    wd = prm['conv_w'].shape[-1]
    xc_l, yg_l = _lru_in(x, modt, norm_g, w_in, prm['conv_w'], prm['conv_b'])
    xc_c, yg_c = _lru_in(x_ctx, modt_ctx, norm_g, w_in, prm['conv_w'], prm['conv_b'])
    zero = jnp.zeros((SUBLANES, wd), F32)
    hs_l, hs_c = [], []
    for dr in range(2):
        wa, wx = prm['w_a'][dr].astype(BF16), prm['w_x'][dr].astype(BF16)
        ba, bx = prm['b_a'][dr].reshape(1, wd), prm['b_x'][dr].reshape(1, wd)
        nsp = (-LRU_C * jax.nn.softplus(-prm['lam'][dr].astype(F32))).reshape(1, wd)
        hc, hfin = _lru_scan(xc_c, wa, wx, ba, bx, nsp, zero, rev=dr == 1, emit=ctx_out)
        hl, _ = _lru_scan(xc_l, wa, wx, ba, bx, nsp, hfin, rev=dr == 1)
        hs_l.append(hl)
        hs_c.append(hc)
    w_out = prm['w_out'].astype(BF16)

    def out(xa, yg, hs, mt):
        tm = _row_tile(xa.shape[0], cap=ROW_TILE_CAP // 2)
        specs = [_row_spec(tm, wd)] * 3
        return _outproj(_lru_out_prologue, [yg, hs[0], hs[1]], specs, wd, xa, mt, 10 ** 9, w_out, tm=tm, name="lru_out")

    return out(x, yg_l, hs_l, modt), (out(x_ctx, yg_c, hs_c, modt_ctx) if ctx_out else None)


def _rope_tables(length):
    rows = length // GRID_W
    t = jnp.arange(length, dtype=jnp.int32)
    row = (t // GRID_W).astype(F32) - (rows - 1) / 2.0
    col = (t % GRID_W).astype(F32) - (GRID_W - 1) / 2.0
    quarter = RET_DK // 4
    inv_freq = ROPE_BASE ** (-jnp.arange(quarter, dtype=F32) / quarter)
    ar, ac = row[:, None] * inv_freq[None, :], col[:, None] * inv_freq[None, :]
    cos = jnp.concatenate([jnp.cos(ar), jnp.cos(ar), jnp.cos(ac), jnp.cos(ac)], axis=-1)
    sin = jnp.concatenate([-jnp.sin(ar), jnp.sin(ar), -jnp.sin(ac), jnp.sin(ac)], axis=-1)
    return cos, sin


def _ret_in_kernel(x_ref, g_ref, sh_ref, sc_ref, w_ref, cos_ref, sin_ref, p_ref, h_ref, *, rotate, n_k, n_v):
    j = pl.program_id(1)
    tn = w_ref.shape[1]

    @pl.when(j == 0)
    def _():
        h_ref[...] = _norm_mod(x_ref[...], g_ref[...], sc_ref[...], sh_ref[...]).astype(BF16)

    is_k = j < n_k
    is_q = (j >= n_k + n_v) & (j < 2 * n_k + n_v)

    @pl.when(is_k | is_q)
    def _():
        scale = jnp.where(is_q, RET_DK ** -0.5, 1.0)
        for hd in range(tn // RET_DK):
            ph = _dot(h_ref[...], w_ref[:, hd * RET_DK:(hd + 1) * RET_DK])
            for s in range(RET_DK // LANES):
                tcols = slice(s * LANES, (s + 1) * LANES)
                ps = ph[:, tcols]
                if rotate:
                    rot = pltpu.roll(ps, LANES // 2, axis=1)
                    ps = ps * cos_ref[:, tcols] + rot * sin_ref[:, tcols]
                p_ref[:, hd * RET_DK + s * LANES:hd * RET_DK + (s + 1) * LANES] = (ps * scale).astype(p_ref.dtype)

    @pl.when(jnp.logical_not(is_k | is_q))
    def _():
        p_ref[...] = _dot(h_ref[...], w_ref[...]).astype(p_ref.dtype)


def _ret_in(x, modt, tiles_per_mod, norm_g, w_in, seq_rows, *, rotate):
    n_rows, d = x.shape
    n_out = w_in.shape[1]
    tm = _row_tile(seq_rows)
    tn = RET_HEADS * RET_DK
    tiles_per_seq = seq_rows // tm
    cos, sin = _rope_tables(seq_rows)
    hk, hv = RET_HEADS * RET_DK, RET_HEADS * RET_DV
    tab_spec = pl.BlockSpec((tm, RET_DK), lambda i, j: (i % tiles_per_seq, 0))
    return pl.pallas_call(
        functools.partial(_ret_in_kernel, rotate=rotate, n_k=hk // tn, n_v=hv // tn),
        grid=(n_rows // tm, n_out // tn),
        in_specs=[_row_spec(tm, d), _const_spec((1, d)), _mod_spec(0, d, tiles_per_mod), _mod_spec(1, d, tiles_per_mod),
                  pl.BlockSpec((d, tn), lambda i, j: (0, j)), tab_spec, tab_spec],
        out_specs=pl.BlockSpec((tm, tn), lambda i, j: (i, j)),
        out_shape=jax.ShapeDtypeStruct((n_rows, n_out), BF16),
        scratch_shapes=[pltpu.VMEM((tm, d), BF16)],
        compiler_params=_cparams("parallel", "arbitrary"),
        name="ret_in",
    )(x, norm_g.reshape(1, d), modt, modt, w_in, cos, sin)


def _ret_tables(c):
    log_g = np.log1p(-np.power(2.0, -5.0 - np.arange(RET_HEADS, dtype=np.float64)))
    idx = np.arange(c, dtype=np.float64)
    diff = idx[:, None] - idx[None, :]
    fwd = np.where(diff >= 0, np.exp(np.where(diff >= 0, diff, 0.0)[None] * log_g[:, None, None]), 0.0)
    bwd = np.where(diff < 0, np.exp(np.where(diff < 0, -diff, 0.0)[None] * log_g[:, None, None]), 0.0)
    xi_f = np.exp((idx + 1.0)[None, :] * log_g[:, None])
    zeta_f = np.exp((c - 1.0 - idx)[None, :] * log_g[:, None])
    xi_b = np.exp((c - idx)[None, :] * log_g[:, None])
    zeta_b = np.exp(idx[None, :] * log_g[:, None])
    dmask = np.stack([fwd, bwd]).astype(np.float32)
    xi = np.stack([xi_f, xi_b])[..., None].astype(np.float32)
    zeta = np.stack([zeta_f, zeta_b])[..., None].astype(np.float32)
    g_blk = [float(v) for v in np.exp(c * log_g).astype(np.float32)]
    return jnp.asarray(dmask), jnp.asarray(xi), jnp.asarray(zeta), g_blk


def _ret_scan_kernel(kf_ref, v0f_ref, v1f_ref, qf_ref, kb_ref, v0b_ref, v1b_ref, qb_ref, dm_ref, xi_ref, zt_ref, r0_ref,
                     of_ref, ob_ref, rf_ref, r_ref, *, g_blk, emit):
    c = pl.program_id(1)
    dir_refs = ((kf_ref, v0f_ref, v1f_ref, qf_ref, of_ref), (kb_ref, v0b_ref, v1b_ref, qb_ref, ob_ref))

    @pl.when(c == 0)
    def _():
        r_ref[...] = r0_ref[...]
        if not emit:
            of_ref[...] = jnp.zeros_like(of_ref)
            ob_ref[...] = jnp.zeros_like(ob_ref)

    hv_half = v0f_ref.shape[1] // RET_DV
    r_olds = [[r_ref[d, h] for h in range(RET_HEADS)] for d in range(2)]
    r_news = [[None] * RET_HEADS for _ in range(2)]
    for h in range(RET_HEADS):
        for d in range(2):
            k_ref, v0_ref, v1_ref, q_ref, o_ref = dir_refs[d]
            kh = k_ref[:, h * RET_DK:(h + 1) * RET_DK]
            v_ref = v0_ref if h < hv_half else v1_ref
            hh = h % hv_half
            vh = v_ref[:, hh * RET_DV:(hh + 1) * RET_DV]
            r_old = r_olds[d][h]
            if emit:
                qh = q_ref[:, h * RET_DK:(h + 1) * RET_DK]
                s = _dot_nt(qh, kh) * dm_ref[d, h]
                o = _dot(s.astype(BF16), vh) + _dot(qh, r_old.astype(BF16)) * xi_ref[d, h]
                o_ref[:, h * RET_DV:(h + 1) * RET_DV] = o.astype(o_ref.dtype)
            r_news[d][h] = g_blk[h] * r_old + _dot_tn((kh * zt_ref[d, h]).astype(BF16), vh)
    for d in range(2):
        for h in range(RET_HEADS):
            r_ref[d, h] = r_news[d][h]

    @pl.when(c == pl.num_programs(1) - 1)
    def _():
        rf_ref[...] = r_ref[...]


def _ret_scan(p, r0, batch, seq_rows, *, emit=True):
    c = min(RET_CHUNK, seq_rows)
    nc = seq_rows // c
    hk, hv = RET_HEADS * RET_DK, RET_HEADS * RET_DV
    dmask, xi, zeta, g_blk = _ret_tables(c)
    n_rows = p.shape[0]
    rows = (lambda b, cc: b * nc + cc, lambda b, cc: b * nc + nc - 1 - cc)
    kcols = lambda rw: [pl.BlockSpec((c, hk), lambda b, cc, blk=blk: (rw(b, cc), blk)) for blk in range(4)]
    full = lambda a: pl.BlockSpec(a.shape, lambda b, cc: (0,) * a.ndim)
    st_spec = pl.BlockSpec((2, None, RET_HEADS, RET_DK, RET_DV), lambda b, cc: (0, b, 0, 0, 0))
    o_spec = lambda rw: pl.BlockSpec((c, hv), (lambda b, cc: (rw(b, cc), 0)) if emit else (lambda b, cc: (0, 0)))
    o_shape = jax.ShapeDtypeStruct((n_rows if emit else c, hv), BF16)
    assert hv == 2 * hk
    of, ob, rf = pl.pallas_call(
        functools.partial(_ret_scan_kernel, g_blk=g_blk, emit=emit),
        grid=(batch, nc),
        in_specs=kcols(rows[0]) + kcols(rows[1]) + [full(dmask), full(xi), full(zeta), st_spec],
        out_specs=[o_spec(rows[0]), o_spec(rows[1]), st_spec],
        out_shape=[o_shape, o_shape, jax.ShapeDtypeStruct(r0.shape, F32)],
        scratch_shapes=[pltpu.VMEM((2, RET_HEADS, RET_DK, RET_DV), F32)],
        compiler_params=_cparams("arbitrary", "arbitrary"),
        name="ret_scan",
    )(*([p] * 8), dmask, xi, zeta, r0)
    return of, ob, rf


def _ret_out_prologue(of_ref, ob_ref, gate_ref, ng_ref):
    o = of_ref[...].astype(F32) + ob_ref[...].astype(F32)
    parts = []
    for h in range(RET_HEADS):
        oh = o[:, h * RET_DV:(h + 1) * RET_DV]
        parts.append(oh * lax.rsqrt(jnp.mean(oh * oh, axis=-1, keepdims=True) + NORM_EPS))
    on = (jnp.concatenate(parts, axis=-1) * ng_ref[...])
    return jax.nn.silu(gate_ref[...].astype(F32)) * on


def _ret_mixer(x, x_ctx, modt, modt_ctx, norm_g, prm, batch, seq, seq_ctx, *, ctx_out):
    d = x.shape[-1]
    w_in = prm['w_in'].astype(BF16)
    hk, hv = RET_HEADS * RET_DK, RET_HEADS * RET_DV
    tiles_per_mod = seq // _row_tile(seq)
    p_l = _ret_in(x, modt, tiles_per_mod, norm_g, w_in, seq, rotate=True)
    p_c = _ret_in(x_ctx, modt_ctx, 10 ** 9, norm_g, w_in, seq_ctx, rotate=False)
    r0 = jnp.zeros((2, batch, RET_HEADS, RET_DK, RET_DV), F32)
    of_c, ob_c, r_c = _ret_scan(p_c, r0, batch, seq_ctx, emit=ctx_out)
    of_l, ob_l, _ = _ret_scan(p_l, r_c, batch, seq)
    w_out = prm['w_out'].astype(BF16)
    ng = prm['norm_g'].reshape(1, hv)

    def out(xa, pa, of, ob, mt, tpm, seq_rows):
        tm = _row_tile(seq_rows, cap=ROW_TILE_CAP // 2)
        gcol = (2 * hk + hv) // hv
        specs = [_row_spec(tm, hv), _row_spec(tm, hv), pl.BlockSpec((tm, hv), lambda i, j: (i, gcol)), _const_spec((1, hv))]
        return _outproj(_ret_out_prologue, [of, ob, pa, ng], specs, hv, xa, mt, tpm * (_row_tile(seq_rows) // tm), w_out,
                        tm=tm, name="ret_out")

    y_l = out(x, p_l, of_l, ob_l, modt, tiles_per_mod, seq)
    y_c = out(x_ctx, p_c, of_c, ob_c, modt_ctx, 10 ** 9, seq_ctx) if ctx_out else None
    return y_l, y_c


GDN_IN_TILE = 8 * LANES


def _gdn_in_kernel(x_ref, xb_ref, xa_ref, g_ref, sh_ref, sc_ref, w_ref, cw_ref, wba_ref, o_ref, ba_ref, h_ref,
                   *, tiles_per_seq, n_kq, n_conv):
    i, j = pl.program_id(0), pl.program_id(1)
    tm = x_ref.shape[0]
    hb = ha = SUBLANES
    heads = o_ref.shape[0]

    @pl.when(j == 0)
    def _():
        first = i % tiles_per_seq == 0
        last = i % tiles_per_seq == tiles_per_seq - 1
        _fill_h(h_ref, x_ref, xb_ref, xa_ref, g_ref, sc_ref, sh_ref, first, last, hb, ha)

    pair = 2 * LANES

    def conv_silu(s2):
        p = _dot(h_ref[...], w_ref[:, s2 * pair:(s2 + 1) * pair])
        return jax.nn.silu(_conv_rows(p, cw_ref[:, s2 * pair:(s2 + 1) * pair], tm, 1, hb))

    @pl.when(j < n_kq)
    def _():
        scale = jnp.where(j >= n_kq // 2, GDN_DK ** -0.5, 1.0)
        for s2 in range(heads // 2):
            a = conv_silu(s2)
            for e in range(2):
                ah = a[:, e * LANES:(e + 1) * LANES]
                o_ref[2 * s2 + e] = (ah * (lax.rsqrt(jnp.sum(ah * ah, axis=-1, keepdims=True) + 1e-6) * scale)
                                     ).astype(o_ref.dtype)

    @pl.when((j >= n_kq) & (j < n_conv))
    def _():
        for s2 in range(heads // 2):
            a = conv_silu(s2)
            for e in range(2):
                o_ref[2 * s2 + e] = a[:, e * LANES:(e + 1) * LANES].astype(o_ref.dtype)

    @pl.when(j >= n_conv)
    def _():
        for s2 in range(heads // 2):
            p = _dot(h_ref[hb:hb + tm, :], w_ref[:, s2 * pair:(s2 + 1) * pair])
            for e in range(2):
                o_ref[2 * s2 + e] = p[:, e * LANES:(e + 1) * LANES].astype(o_ref.dtype)

    @pl.when(j == pl.num_programs(1) - 1)
    def _():
        ba_ref[...] = _dot(h_ref[hb:hb + tm, :], wba_ref[...])


def _gdn_in(x, modt, tiles_per_mod, norm_g, w_all, cw_all, w_ba, seq_rows):
    n_rows, d = x.shape
    n_out = w_all.shape[1]
    tm = _row_tile(seq_rows)
    tn = GDN_IN_TILE
    hb = ha = SUBLANES
    heads = tn // LANES
    n_kq = 2 * GDN_K_HEADS * GDN_DK // tn
    n_conv = n_kq + GDN_V_HEADS * GDN_DV // tn
    kern = functools.partial(_gdn_in_kernel, tiles_per_seq=seq_rows // tm, n_kq=n_kq, n_conv=n_conv)
    specs = _halo_specs(tm, hb, ha, d, n_rows) + [
        _const_spec((1, d)), _mod_spec(0, d, tiles_per_mod), _mod_spec(1, d, tiles_per_mod),
        pl.BlockSpec((d, tn), lambda i, j: (0, j)), pl.BlockSpec((cw_all.shape[0], tn), lambda i, j: (0, j)),
        _const_spec(w_ba.shape)]
    return pl.pallas_call(
        kern,
        grid=(n_rows // tm, n_out // tn),
        in_specs=specs,
        out_specs=[pl.BlockSpec((heads, tm, LANES), lambda i, j: (j, i, 0)), pl.BlockSpec((tm, LANES), lambda i, j: (i, 0))],
        out_shape=[jax.ShapeDtypeStruct((n_out // LANES, n_rows, LANES), BF16), jax.ShapeDtypeStruct((n_rows, LANES), F32)],
        scratch_shapes=[pltpu.VMEM((tm + hb + ha, d), BF16)],
        compiler_params=_cparams("parallel", "arbitrary"),
        name="gdn_in",
    )(x, x, x, norm_g.reshape(1, d), modt, modt, w_all, cw_all, w_ba)


def _split3(a):
    hi = a.astype(BF16)
    lo = (a - hi.astype(F32)).astype(BF16)
    return hi, lo


def _dot3s(a, b):
    ah, al = a
    bh, bl = b
    m = ah.shape[0]
    both = _dot(jnp.concatenate([ah, al], axis=0), bh)
    return both[:m] + (both[m:] + _dot(ah, bl))


def _cumsum_rows(x, rev):
    n = x.shape[0]
    row = lax.broadcasted_iota(jnp.int32, x.shape, 0)
    shift = 1
    while shift < n:
        if rev:
            x = x + jnp.where(row < n - shift, pltpu.roll(x, n - shift, axis=0), 0.0)
        else:
            x = x + jnp.where(row >= shift, pltpu.roll(x, shift, axis=0), 0.0)
        shift *= 2
    return x


GDN_GROUP = 4


def _gdn_scan_kernel(kf_ref, qf_ref, vf_ref, baf_ref, kb_ref, qb_ref, vb_ref, bab_ref, an_ref, dtb_ref,
                     s0_ref, of_ref, ob_ref, sf_ref, s_ref, *, emit):
    cidx = pl.program_id(1)
    c = kf_ref.shape[1]
    rep = GDN_V_HEADS // GDN_K_HEADS
    dir_refs = ((kf_ref, qf_ref, vf_ref, baf_ref, of_ref), (kb_ref, qb_ref, vb_ref, bab_ref, ob_ref))

    @pl.when(cidx == 0)
    def _():
        s_ref[...] = s0_ref[...]
        if not emit:
            of_ref[...] = jnp.zeros_like(of_ref)
            ob_ref[...] = jnp.zeros_like(ob_ref)

    def expand(src, first_lane, g):
        cols = [jnp.broadcast_to(src[:, first_lane + gs * g + h:first_lane + gs * g + h + 1], (c, GDN_DV)) for h in range(gs)]
        per = LANES // c
        narrow = []
        for p in range(gs // per):
            blk = cols[per * p + per - 1]
            for h in range(per - 2, -1, -1):
                blk = jnp.where(lane_in_vreg < (h + 1) * c, cols[per * p + h], blk)
            narrow.append(blk)
        return jnp.concatenate(cols, axis=1), jnp.concatenate(narrow, axis=1)

    gs = GDN_GROUP
    lane_in_vreg = lax.broadcasted_iota(jnp.int32, (c, LANES), 1)
    wide = gs * c
    ii = lax.broadcasted_iota(jnp.int32, (c, wide), 0)
    jj = lax.broadcasted_iota(jnp.int32, (c, wide), 1) % c
    br = lax.broadcasted_iota(jnp.int32, (wide, wide), 0) // c
    bc = lax.broadcasted_iota(jnp.int32, (wide, wide), 1) // c
    blockmask = (br == bc).astype(F32).astype(BF16)
    nk, nv = GDN_K_HEADS, GDN_V_HEADS
    zc = jnp.zeros((c, GDN_DK), BF16)
    zs = jnp.zeros((GDN_DK, GDN_DV), BF16)
    zr = jnp.zeros((c, GDN_DV), BF16)

    gcs, bts, incl, strict, eye, last = [], [], [], [], [], []
    for d in range(2):
        rev = d == 1
        ba = dir_refs[d][3][...]
        z = ba + dtb_ref[d]
        sp = jnp.maximum(z, 0.0) + jnp.log(1.0 + jnp.exp(-jnp.abs(z)))
        gcs.append(_cumsum_rows(an_ref[d] * sp, rev))
        bts.append(jax.nn.sigmoid(ba))
        diff = (jj - ii) if rev else (ii - jj)
        incl.append(diff >= 0)
        strict.append(diff > 0)
        eye.append((diff == 0).astype(F32))
        last.append(0 if rev else c - 1)

    def bdiag(blocks, zero):
        n = len(blocks)
        return jnp.concatenate([jnp.concatenate([blocks[r] if r == q else zero for q in range(n)], axis=1)
                                for r in range(n)], axis=0)

    def bd_tile(a16):
        return jnp.concatenate([a16] * gs, axis=0) * blockmask

    def dot3_bd(lhs, rhs_hi, rhs_lo):
        lh, ll = _split3(lhs)
        m = lhs.shape[0]
        both = _dot(jnp.concatenate([lh, ll], axis=0), rhs_hi)
        return both[:m] + (both[m:] + _dot(lh, rhs_lo))

    pieces = lambda a: [a[:, h * GDN_DV:(h + 1) * GDN_DV] for h in range(gs)]
    chains = [(d, g) for g in range(nv // gs) for d in range(2)]
    ks = [[dir_refs[d][0][h].astype(F32) for h in range(nk)] for d in range(2)]
    qs = [[dir_refs[d][1][h].astype(F32) for h in range(nk)] for d in range(2)] if emit else None
    vs = [[dir_refs[d][2][h].astype(F32) for h in range(nv)] for d in range(2)]
    s_olds = [[s_ref[d, h] for h in range(nv)] for d in range(2)]

    xs, ts, attns, gc5s, bt5s = [], [], [], [], []
    for d, g in chains:
        k0, k1 = ks[d][2 * g], ks[d][2 * g + 1]
        k016, k116 = k0.astype(BF16), k1.astype(BF16)
        rhs_t = jnp.concatenate([jnp.concatenate([k016, zc], axis=1)] * rep
                                + [jnp.concatenate([zc, k116], axis=1)] * rep, axis=0)
        if emit:
            lhs = jnp.concatenate([jnp.concatenate([k0, qs[d][2 * g]], axis=0),
                                   jnp.concatenate([k1, qs[d][2 * g + 1]], axis=0)], axis=1).astype(BF16)
        else:
            lhs = jnp.concatenate([k016, k116], axis=1)
        gram = _dot_nt(lhs, rhs_t)
        gc5, gcm = expand(gcs[d], d * 2 * nv + nv, g)
        bt5, btm = expand(bts[d], d * 2 * nv, g)
        gc5s.append(gc5)
        bt5s.append(bt5)
        grow = jnp.sum(gcm * eye[d], axis=0, keepdims=True)
        dec = jnp.where(incl[d], jnp.exp(jnp.where(incl[d], gcm - grow, 0.0)), 0.0)
        x = jnp.where(strict[d], -(gram[:c] * btm) * dec, 0.0)
        xs.append(x)
        ts.append(eye[d] + x)
        attns.append(gram[c:] * dec if emit else None)

    splits = [_split3(x) for x in xs]
    xs = [dot3_bd(x, bd_tile(xh), bd_tile(xl)) for x, (xh, xl) in zip(xs, splits)]
    nlev = int(math.log2(c)) - 1
    for lvl in range(1, nlev + 1):
        splits = [_split3(x) for x in xs]
        if lvl < nlev:
            ps = [dot3_bd(jnp.concatenate([t, x], axis=0), bd_tile(xh), bd_tile(xl))
                  for t, x, (xh, xl) in zip(ts, xs, splits)]
            ts = [t + p[:c] for t, p in zip(ts, ps)]
            xs = [p[c:] for p in ps]
        else:
            ts = [t + dot3_bd(t, bd_tile(xh), bd_tile(xl)) for t, (xh, xl) in zip(ts, splits)]

    kcats, p1s = [], []
    for n, (d, g) in enumerate(chains):
        k0, k1 = ks[d][2 * g], ks[d][2 * g + 1]
        gc5, bt5 = gc5s[n], bt5s[n]
        eg5 = jnp.exp(gc5)
        kcat = jnp.concatenate([k0] * rep + [k1] * rep, axis=1)
        kbe = kcat * bt5 * eg5
        s16 = [s_olds[d][gs * g + h].astype(BF16) for h in range(gs)]
        if emit:
            qe = jnp.concatenate([qs[d][2 * g]] * rep + [qs[d][2 * g + 1]] * rep, axis=1) * eg5
            top = jnp.concatenate([kbe, qe], axis=0).astype(BF16)
        else:
            top = kbe.astype(BF16)
        p1s.append(jnp.concatenate([_dot(top[:, 2 * pr * GDN_DK:(2 * pr + 2) * GDN_DK], bdiag(s16[2 * pr:2 * pr + 2], zs))
                                    for pr in range(gs // 2)], axis=1))
        kcats.append(kcat)

    bdvs = []
    for n, (d, g) in enumerate(chains):
        vcat = jnp.concatenate(vs[d][gs * g:gs * (g + 1)], axis=1)
        rh, rl = _split3(vcat * bt5s[n] - p1s[n][:c])
        vn16 = dot3_bd(ts[n], bdiag(pieces(rh), zr), bdiag(pieces(rl), zr)).astype(BF16)
        bdvs.append(bdiag(pieces(vn16), zr))

    outs = [[None] * nv for _ in range(2)]
    s_news = [[None] * nv for _ in range(2)]
    for n, (d, g) in enumerate(chains):
        gl5 = gc5s[n][last[d]:last[d] + 1, :]
        if emit:
            o = pieces(p1s[n][c:] + _dot(attns[n].astype(BF16), bdvs[n]))
        kd = (kcats[n] * jnp.exp(gl5 - gc5s[n])).astype(BF16)
        sn = _dot_tn(jnp.concatenate(pieces(kd), axis=0), bdvs[n])
        egl = jnp.exp(gl5)
        for h in range(gs):
            cols = slice(h * GDN_DV, (h + 1) * GDN_DV)
            s_news[d][gs * g + h] = s_olds[d][gs * g + h] * egl[:, cols] + sn[:, cols]
            if emit:
                outs[d][gs * g + h] = o[h]
    for d in range(2):
        for hv in range(nv):
            if emit:
                dir_refs[d][4][hv] = outs[d][hv].astype(dir_refs[d][4].dtype)
            s_ref[d, hv] = s_news[d][hv]

    @pl.when(cidx == pl.num_programs(1) - 1)
    def _():
        sf_ref[...] = s_ref[...]


def _gdn_scan(kqvz, ba, an, dtb, s0, batch, seq_rows, *, emit=True):
    c = min(GDN_CHUNK, seq_rows)
    nc = seq_rows // c
    n_rows = kqvz.shape[1]
    rows = (lambda b, cc: b * nc + cc, lambda b, cc: b * nc + nc - 1 - cc)
    kh, vh = GDN_K_HEADS, GDN_V_HEADS
    st_spec = pl.BlockSpec((2, None, vh, GDN_DK, GDN_DV), lambda b, cc: (0, b, 0, 0, 0))
    full = lambda a: pl.BlockSpec(a.shape, lambda b, cc: (0,) * a.ndim)

    def dir_specs(rw):
        return [pl.BlockSpec((kh, c, LANES), lambda b, cc: (0, rw(b, cc), 0)),
                pl.BlockSpec((kh, c, LANES), lambda b, cc: (1, rw(b, cc), 0)),
                pl.BlockSpec((vh, c, LANES), lambda b, cc: (1, rw(b, cc), 0)),
                pl.BlockSpec((c, LANES), lambda b, cc: (rw(b, cc), 0))]

    def o_spec(rw):
        return pl.BlockSpec((vh, c, LANES), (lambda b, cc: (0, rw(b, cc), 0)) if emit else (lambda b, cc: (0, 0, 0)))

    o_shape = jax.ShapeDtypeStruct((vh, n_rows if emit else c, LANES), BF16)
    of, ob, sf = pl.pallas_call(
        functools.partial(_gdn_scan_kernel, emit=emit),
        grid=(batch, nc),
        in_specs=dir_specs(rows[0]) + dir_specs(rows[1]) + [full(an), full(dtb), st_spec],
        out_specs=[o_spec(rows[0]), o_spec(rows[1]), st_spec],
        out_shape=[o_shape, o_shape, jax.ShapeDtypeStruct(s0.shape, F32)],
        scratch_shapes=[pltpu.VMEM((2, vh, GDN_DK, GDN_DV), F32)],
        compiler_params=_cparams("arbitrary", "arbitrary"),
        name="gdn_scan",
    )(kqvz, kqvz, kqvz, ba, kqvz, kqvz, kqvz, ba, an, dtb, s0)
    return of, ob, sf


def _gdn_out_prologue(of_ref, ob_ref, z_ref, ng_ref):
    parts = []
    for h in range(GDN_V_HEADS):
        o = of_ref[h].astype(F32) + ob_ref[h].astype(F32)
        on = o * lax.rsqrt(jnp.mean(o * o, axis=-1, keepdims=True) + NORM_EPS) * ng_ref[...]
        parts.append(on * jax.nn.silu(z_ref[h].astype(F32)))
    return jnp.concatenate(parts, axis=-1)


def _gdn_mixer(x, x_ctx, modt, modt_ctx, norm_g, prm, batch, seq, seq_ctx, *, ctx_out):
    d = x.shape[-1]
    qk, vv = GDN_K_HEADS * GDN_DK, GDN_V_HEADS * GDN_DV
    ng2 = 2 * 2 * GDN_V_HEADS
    w_in, conv_w = prm['w_in'], prm['conv_w']
    w_all = jnp.concatenate([w_in[:, :qk], w_in[:, qk + vv + ng2:2 * qk + vv + ng2], w_in[:, qk:qk + vv],
                             w_in[:, 2 * qk + vv + ng2:]], axis=1).astype(BF16)
    w_ba = jnp.concatenate([w_in[:, qk + vv:qk + vv + ng2], jnp.zeros((d, LANES - ng2), w_in.dtype)], axis=1).astype(BF16)
    cw_all = jnp.concatenate([conv_w[:, :qk], conv_w[:, qk + vv:], conv_w[:, qk:qk + vv],
                              jnp.zeros((conv_w.shape[0], vv), conv_w.dtype)], axis=1)
    tiles_per_mod = seq // _row_tile(seq)
    p_l, ba_l = _gdn_in(x, modt, tiles_per_mod, norm_g, w_all, cw_all, w_ba, seq)
    p_c, ba_c = _gdn_in(x_ctx, modt_ctx, 10 ** 9, norm_g, w_all, cw_all, w_ba, seq_ctx)
    s0 = jnp.zeros((2, batch, GDN_V_HEADS, GDN_DK, GDN_DV), F32)

    def gate_lanes(p):
        rows = [jnp.zeros((LANES,), F32).at[dr * 2 * GDN_V_HEADS + GDN_V_HEADS:(dr + 1) * 2 * GDN_V_HEADS].set(p[dr])
                for dr in range(2)]
        return jnp.stack(rows).reshape(2, 1, LANES)

    an = gate_lanes(-jnp.exp(prm['a_log'].astype(F32)))
    dtb = gate_lanes(prm['dt_bias'].astype(F32))
    ocf, ocb, sc = _gdn_scan(p_c, ba_c, an, dtb, s0, batch, seq_ctx, emit=ctx_out)
    olf, olb, _ = _gdn_scan(p_l, ba_l, an, dtb, sc, batch, seq)
    os_l, os_c = [olf, olb], [ocf, ocb]
    w_out = prm['w_out'].astype(BF16)
    ng = prm['norm_g'].reshape(1, GDN_DV)
    vh = GDN_V_HEADS
    z_blk = (2 * GDN_K_HEADS + vh) // vh

    def out(xa, pa, oa, mt, tpm, seq_rows):
        tm = _row_tile(seq_rows, cap=ROW_TILE_CAP // 2)
        hspec = lambda blk: pl.BlockSpec((vh, tm, LANES), lambda i, j: (blk, i, 0))
        specs = [hspec(0), hspec(0), hspec(z_blk), _const_spec((1, GDN_DV))]
        return _outproj(_gdn_out_prologue, [oa[0], oa[1], pa, ng], specs, vv, xa, mt, tpm * (_row_tile(seq_rows) // tm),
                        w_out, tm=tm, name="gdn_out")

    y_l = out(x, p_l, os_l, modt, tiles_per_mod, seq)
    y_c = out(x_ctx, p_c, os_c, modt_ctx, 10 ** 9, seq_ctx) if ctx_out else None
    return y_l, y_c


def kernel(x, c, ctx, c_ctx, norm1_g, norm2_g, mod_w, mod_b, ffn_w_in, ffn_conv_w, ffn_conv_b, ffn_w_out, s5_lam_re, s5_lam_im, s5_log_step, s5_b_re, s5_b_im, s5_c_re, s5_c_im, s5_d, s5_w_glu, s5_b_glu, lru_w_in, lru_conv_w, lru_conv_b, lru_w_a, lru_b_a, lru_w_x, lru_b_x, lru_lam, lru_w_out, ret_w_in, ret_norm_g, ret_w_out, gdn_w_in, gdn_conv_w, gdn_a_log, gdn_dt_bias, gdn_norm_g, gdn_w_out, final_norm_g):
    B, L, D = x.shape
    Lc = ctx.shape[1]
    depth = mod_w.shape[0]
    assert B == SUBLANES and depth == 4, "time-major layers put the batch on the 8 sublanes; four mixer kinds"
    c16 = jnp.concatenate([c, c_ctx[None], jnp.zeros((16 - B - 1, D), F32)], 0)
    mods = _modulation(c16, mod_w, mod_b)

    def ffn(i, xa, mt, tiles_per_mod, rs, seq_rows, final_norm=False):
        return _ffn(xa, mt, tiles_per_mod, norm2_g[i], ffn_w_in[i].astype(BF16), ffn_conv_w[i], ffn_conv_b[i],
                    ffn_w_out[i].astype(BF16), final_norm_g, rs=rs, seq_rows=seq_rows, final_norm=final_norm)

    xt = jnp.swapaxes(x, 0, 1).reshape(L * B, D)
    ct = jnp.swapaxes(ctx, 0, 1).reshape(Lc * B, D)
    s5p = dict(lam_re=s5_lam_re[0], lam_im=s5_lam_im[0], log_step=s5_log_step[0], b_re=s5_b_re[0], b_im=s5_b_im[0],
               c_re=s5_c_re[0], c_im=s5_c_im[0], d=s5_d[0], w_glu=s5_w_glu[0], b_glu=s5_b_glu[0])
    lrup = dict(w_in=lru_w_in[0], conv_w=lru_conv_w[0], conv_b=lru_conv_b[0], w_a=lru_w_a[0], b_a=lru_b_a[0],
                w_x=lru_w_x[0], b_x=lru_b_x[0], lam=lru_lam[0], w_out=lru_w_out[0])
    for i, (mixer, prm) in enumerate(((_s5_mixer, s5p), (_lru_mixer, lrup))):
        mt = mods[i, :B][None]
        mtc = jnp.broadcast_to(mods[i, B][None, None, :], (1, SUBLANES, N_MOD * D))
        xt, ct = mixer(xt, ct, mt, mtc, norm1_g[i], prm, ctx_out=True)
        xt = ffn(i, xt, mt, 10 ** 9, SUBLANES, L * B)
        ct = ffn(i, ct, mtc, 10 ** 9, SUBLANES, Lc * B)

    xb = jnp.swapaxes(xt.reshape(L, B, D), 0, 1).reshape(B * L, D)
    cb = jnp.swapaxes(ct.reshape(Lc, B, D), 0, 1).reshape(B * Lc, D)
    retp = dict(w_in=ret_w_in[0], norm_g=ret_norm_g[0], w_out=ret_w_out[0])
    gdnp = dict(w_in=gdn_w_in[0], conv_w=gdn_conv_w[0], a_log=gdn_a_log[0], dt_bias=gdn_dt_bias[0],
                norm_g=gdn_norm_g[0], w_out=gdn_w_out[0])
    tiles_per_batch = L // _row_tile(L)
    for i, (mixer, prm) in ((2, (_ret_mixer, retp)), (3, (_gdn_mixer, gdnp))):
        ctx_out = i < depth - 1
        mt = jnp.broadcast_to(mods[i, :B][:, None, :], (B, SUBLANES, N_MOD * D))
        mtc = jnp.broadcast_to(mods[i, B][None, None, :], (1, SUBLANES, N_MOD * D))
        xb, cb_new = mixer(xb, cb, mt, mtc, norm1_g[i], prm, B, L, Lc, ctx_out=ctx_out)
        xb = ffn(i, xb, mt, tiles_per_batch, 1, L, final_norm=not ctx_out)
        if ctx_out:
            cb = ffn(i, cb_new, mtc, 10 ** 9, 1, Lc)
    return xb.reshape(B, L, D)
```

```python
import functools
import math

import numpy as np
import jax
import jax.numpy as jnp
from jax import lax
from jax.experimental import pallas as pl
from jax.experimental.pallas import tpu as pltpu

F32 = jnp.float32
BF16 = jnp.bfloat16

NORM_EPS = 1e-6
SUBLANES = 8
LANES = 128
VMEM_LIMIT_BYTES = 56 * 1024 * 1024
ROW_TILE_CAP = 1024

S5_GROUP = 16
S5_STATE = 64
S5_GB = 16
LRU_BLOCKS = 4
LRU_C = 8.0
RET_HEADS = 4
RET_DK = 256
RET_DV = 512
RET_CHUNK = 128
ROPE_BASE = 10000.0
GRID_W = 64
GDN_K_HEADS = 8
GDN_V_HEADS = 16
GDN_DK = 128
GDN_DV = 128
GDN_CHUNK = 64
N_MOD = 6


def _cparams(*sem):
    return pltpu.CompilerParams(dimension_semantics=sem, vmem_limit_bytes=VMEM_LIMIT_BYTES)


def _row_tile(rows, cap=None):
    cap = ROW_TILE_CAP if cap is None else cap
    tm = min(cap, rows)
    assert rows % tm == 0 and tm % SUBLANES == 0, (rows, tm)
    return tm


def _dot(a, b):
    return jnp.dot(a, b, preferred_element_type=F32)


def _dot_tn(a, b):
    return lax.dot_general(a, b, (((0,), (0,)), ((), ())), preferred_element_type=F32)


def _dot_nt(a, b):
    return lax.dot_general(a, b, (((1,), (1,)), ((), ())), preferred_element_type=F32)


def _gelu(x):
    return jax.nn.gelu(x, approximate=True)


def _sigmoid(x):
    return 1.0 / (1.0 + jnp.exp(-x))


def _silu(x):
    return x * _sigmoid(x)


def _tile8(v, rows):
    n = v.shape[-1]
    return jnp.broadcast_to(v[None], (rows // SUBLANES, SUBLANES, n)).reshape(rows, n)


def _norm_mod(x, g, sc, sh):
    ms = jnp.mean(x * x, axis=-1, keepdims=True)
    y = x * lax.rsqrt(ms + NORM_EPS) * g
    rows = x.shape[0]
    return y * (1.0 + _tile8(sc, rows)) + _tile8(sh, rows)


def _shift_rows(p, off, rows):
    if off % SUBLANES == 0:
        return p[off:off + rows]
    n = p.shape[0]
    base = (off // SUBLANES) * SUBLANES
    rolled = pltpu.roll(p, (n - (off - base)) % n, axis=0)
    return rolled[base:base + rows]


def _conv_rows(p_ext, cw, rows, rs, hb):
    k_taps = cw.shape[0]
    acc = None
    for k in range(k_taps):
        term = cw[k:k + 1, :] * _shift_rows(p_ext, hb + (k - k_taps // 2) * rs, rows)
        acc = term if acc is None else acc + term
    return acc


def _fill_h(h_ref, x_ref, xb_ref, xa_ref, g_ref, sc_ref, sh_ref, first, last, hb, ha):
    tm = x_ref.shape[0]
    g, sc, sh = g_ref[...], sc_ref[...], sh_ref[...]
    h_ref[hb:hb + tm, :] = _norm_mod(x_ref[...], g, sc, sh).astype(BF16)
    if hb:
        hbv = _norm_mod(xb_ref[...], g, sc, sh)
        h_ref[0:hb, :] = jnp.where(first, 0.0, hbv).astype(BF16)
    if ha:
        hav = _norm_mod(xa_ref[...], g, sc, sh)
        h_ref[hb + tm:hb + tm + ha, :] = jnp.where(last, 0.0, hav).astype(BF16)


def _halo_specs(tm, hb, ha, d, n_rows):
    specs = [pl.BlockSpec((tm, d), lambda i, j: (i, 0))]
    if hb:
        specs.append(pl.BlockSpec((hb, d), lambda i, j: (jnp.maximum(i * (tm // hb) - 1, 0), 0)))
    if ha:
        nblk = n_rows // ha
        specs.append(pl.BlockSpec((ha, d), lambda i, j: (jnp.minimum((i + 1) * (tm // ha), nblk - 1), 0)))
    return specs


def _mod_spec(chunk, d, tiles_per_mod):
    return pl.BlockSpec((None, SUBLANES, d), lambda i, j: (i // tiles_per_mod, 0, chunk))


def _mod_kernel(c_ref, w_ref, b_ref, o_ref):
    a = _silu(c_ref[...]).astype(BF16)
    o_ref[...] = _dot(a, w_ref[...].astype(BF16)) + b_ref[...]


def _modulation(c16, mod_w, mod_b):
    depth, d, n = mod_w.shape
    tn = n // 4
    return pl.pallas_call(
        _mod_kernel,
        grid=(depth, n // tn),
        in_specs=[pl.BlockSpec((16, d), lambda l, j: (0, 0)),
                  pl.BlockSpec((None, d, tn), lambda l, j: (l, 0, j)),
                  pl.BlockSpec((None, 1, tn), lambda l, j: (l, 0, j))],
        out_specs=pl.BlockSpec((None, 16, tn), lambda l, j: (l, 0, j)),
        out_shape=jax.ShapeDtypeStruct((depth, 16, n), F32),
        compiler_params=_cparams("arbitrary", "arbitrary"),
        name="modulation",
    )(c16, mod_w, mod_b.reshape(depth, 1, n))


FFN_TILE = 2 * LANES
FFN_DOWN_GROUP = 4


def _ffn_kernel(x_ref, xb_ref, xa_ref, g_ref, sh_ref, sc_ref, gt_ref, win_ref, cw_ref, cb_ref, wo_ref, fg_ref, o_ref, h_ref,
                *, rs, tiles_per_seq, final_norm):
    i = pl.program_id(0)
    tm = x_ref.shape[0]
    hb = ha = SUBLANES
    f = wo_ref.shape[0]
    first = i % tiles_per_seq == 0
    last = i % tiles_per_seq == tiles_per_seq - 1
    _fill_h(h_ref, x_ref, xb_ref, xa_ref, g_ref, sc_ref, sh_ref, first, last, hb, ha)
    acc = None
    nj = f // FFN_TILE
    pending = []
    for j in range(nj):
        cols = slice(j * FFN_TILE, (j + 1) * FFN_TILE)
        gate = _dot(h_ref[...], win_ref[:, cols])
        up = _dot(h_ref[hb:hb + tm, :], win_ref[:, f + j * FFN_TILE:f + (j + 1) * FFN_TILE])
        gc = _conv_rows(gate, cw_ref[:, cols], tm, rs, hb) + cb_ref[:, cols]
        pending.append((_gelu(gc) * up).astype(BF16))
        if len(pending) == FFN_DOWN_GROUP or j == nj - 1:
            k0 = (j + 1 - len(pending)) * FFN_TILE
            part = _dot(jnp.concatenate(pending, axis=1), wo_ref[k0:(j + 1) * FFN_TILE, :])
            acc = part if acc is None else acc + part
            pending = []
    y = x_ref[...] + _tile8(gt_ref[...], tm) * acc
    if final_norm:
        ms = jnp.mean(y * y, axis=-1, keepdims=True)
        y = y * lax.rsqrt(ms + NORM_EPS) * fg_ref[...]
    o_ref[...] = y


def _ffn(x, modt, tiles_per_mod, norm_g, w_in, conv_w, conv_b, w_out, final_g, *, rs, seq_rows, final_norm=False):
    n_rows, d = x.shape
    f = w_out.shape[0]
    tm = _row_tile(seq_rows)
    assert f % FFN_TILE == 0
    hb = ha = SUBLANES
    kern = functools.partial(_ffn_kernel, rs=rs, tiles_per_seq=seq_rows // tm, final_norm=final_norm)
    resident = lambda shape: pl.BlockSpec(shape, lambda i, j: (0, 0), pipeline_mode=pl.Buffered(1))
    in_specs = _halo_specs(tm, hb, ha, d, n_rows) + [
        pl.BlockSpec((1, d), lambda i, j: (0, 0)),
        _mod_spec(3, d, tiles_per_mod), _mod_spec(4, d, tiles_per_mod), _mod_spec(5, d, tiles_per_mod),
        resident(w_in.shape), resident(conv_w.shape), resident((1, f)), resident(w_out.shape),
        pl.BlockSpec((1, d), lambda i, j: (0, 0)),
    ]
    return pl.pallas_call(
        kern,
        grid=(n_rows // tm, 1),
        in_specs=in_specs,
        out_specs=pl.BlockSpec((tm, d), lambda i, j: (i, 0)),
        out_shape=jax.ShapeDtypeStruct((n_rows, d), F32),
        scratch_shapes=[pltpu.VMEM((tm + hb + ha, d), BF16)],
        compiler_params=_cparams("parallel", "arbitrary"),
        name="conv_ffn",
    )(x, x, x, norm_g.reshape(1, d), modt, modt, modt, w_in, conv_w, conv_b.reshape(1, f), w_out, final_g.reshape(1, d))


def _outproj_kernel(*refs, n_pro, prologue, glu):
    pro_refs = refs[:n_pro]
    if glu:
        x_ref, gt_ref, w_ref, b_ref, o_ref = refs[n_pro:]
    else:
        x_ref, gt_ref, w_ref, o_ref = refs[n_pro:]
    d = o_ref.shape[1]
    a = prologue(*pro_refs).astype(BF16)
    y = _dot(a, w_ref[:, 0:d])
    if glu:
        y = (y + b_ref[:, 0:d]) * jax.nn.sigmoid(_dot(a, w_ref[:, d:2 * d]) + b_ref[:, d:2 * d])
    o_ref[...] = x_ref[...] + _tile8(gt_ref[...], o_ref.shape[0]) * y


def _outproj(prologue, pro_args, pro_specs, k_dim, x, modt, tiles_per_mod, w, bias=None, *, tm, name):
    n_rows, d = x.shape
    glu = bias is not None
    kern = functools.partial(_outproj_kernel, n_pro=len(pro_args), prologue=prologue, glu=glu)
    specs = list(pro_specs) + [pl.BlockSpec((tm, d), lambda i, j: (i, 0)),
                               pl.BlockSpec((None, SUBLANES, d), lambda i, j: (i // tiles_per_mod, 0, 2)),
                               pl.BlockSpec(w.shape, lambda i, j: (0, 0))]
    args = list(pro_args) + [x, modt, w]
    if glu:
        specs += [pl.BlockSpec(bias.shape, lambda i, j: (0, 0))]
        args += [bias]
    return pl.pallas_call(
        kern,
        grid=(n_rows // tm, 1),
        in_specs=specs,
        out_specs=pl.BlockSpec((tm, d), lambda i, j: (i, 0)),
        out_shape=jax.ShapeDtypeStruct((n_rows, d), F32),
        compiler_params=_cparams("parallel", "arbitrary"),
        name=name,
    )(*args)


def _row_spec(tm, n):
    return pl.BlockSpec((tm, n), lambda i, j: (i, 0))


def _const_spec(shape):
    nd = len(shape)
    return pl.BlockSpec(shape, lambda i, j: (0,) * nd)


def _s5_weights(lam_re, lam_im, log_step, b_re, b_im, c_re, c_im):
    g, p = lam_re.shape
    gc = b_re.shape[-1]
    lr = jnp.minimum(lam_re.astype(F32), -1e-4)
    li = lam_im.astype(F32)
    step = jnp.exp(log_step.astype(F32))[:, None]
    mag = jnp.exp(lr * step)
    ar, ai = mag * jnp.cos(li * step), mag * jnp.sin(li * step)
    den = lr * lr + li * li
    kr = ((ar - 1.0) * lr + ai * li) / den
    ki = (ai * lr - (ar - 1.0) * li) / den
    br32, bi32 = b_re.astype(F32), b_im.astype(F32)
    br = kr[..., None] * br32 - ki[..., None] * bi32
    bi = kr[..., None] * bi32 + ki[..., None] * br32
    nb = g // S5_GB
    eye = jnp.eye(S5_GB, dtype=F32)

    def pack_b(b):
        b4 = b.reshape(nb, S5_GB, p, gc)
        return jnp.einsum('blpc,lm->blcmp', b4, eye).reshape(nb, S5_GB * gc, S5_GB * p)

    def pack_c(cm):
        c4 = cm.astype(F32).reshape(nb, S5_GB, gc, p)
        return jnp.einsum('blcp,lm->blpmc', c4, eye).reshape(nb, S5_GB * p, S5_GB * gc)

    bm = jnp.concatenate([pack_b(br), pack_b(bi)], axis=-1).astype(BF16)
    cm = jnp.concatenate([pack_c(c_re), -pack_c(c_im)], axis=1).astype(BF16)
    lam = jnp.stack([ar.reshape(nb, S5_GB * p), ai.reshape(nb, S5_GB * p)], axis=1)
    lam = jnp.broadcast_to(lam[:, :, None, :], (nb, 2, SUBLANES, S5_GB * p))
    return bm, cm, lam


def _s5_scan_kernel(x_ref, g_ref, sh_ref, sc_ref, bm_ref, cm_ref, lam_ref, s0_ref, y_ref, sf_ref, st_ref, *bu_refs,
                    rev, emit):
    i = pl.program_id(0)
    tm = x_ref.shape[0]
    nt = tm // SUBLANES
    nb, kin, ns2 = bm_ref.shape
    ns = ns2 // 2

    @pl.when(i == 0)
    def _():
        st_ref[...] = s0_ref[...]
        if not emit:
            y_ref[...] = jnp.zeros_like(y_ref)

    h = _norm_mod(x_ref[...], g_ref[...], sc_ref[...], sh_ref[...]).astype(BF16)
    for gb in range(nb):
        bu_ref = bu_refs[gb]
        bu_ref[...] = _dot(h[:, gb * kin:(gb + 1) * kin], bm_ref[gb])
        ar, ai = lam_ref[gb, 0], lam_ref[gb, 1]
        sr, si = st_ref[gb, 0], st_ref[gb, 1]
        for t in (range(nt - 1, -1, -1) if rev else range(nt)):
            rows = slice(t * SUBLANES, (t + 1) * SUBLANES)
            sr, si = (ar * sr - ai * si + bu_ref[rows, 0:ns], ar * si + ai * sr + bu_ref[rows, ns:ns2])
            if emit:
                bu_ref[rows, 0:ns] = sr
                bu_ref[rows, ns:ns2] = si
        st_ref[gb, 0] = sr
        st_ref[gb, 1] = si
        if emit:
            y_ref[:, gb * kin:(gb + 1) * kin] = _dot(bu_ref[...].astype(BF16), cm_ref[gb])

    @pl.when(i == pl.num_programs(0) - 1)
    def _():
        sf_ref[...] = st_ref[...]


def _s5_scan(x, modt, norm_g, bm, cm, lam, s0, *, rev, emit=True):
    n_rows, d = x.shape
    tm = _row_tile(n_rows, cap=ROW_TILE_CAP // 2)
    ntile = n_rows // tm
    tile = (lambda i: (ntile - 1 - i, 0)) if rev else (lambda i: (i, 0))
    mod = lambda k: pl.BlockSpec((None, SUBLANES, d), lambda i: (0, 0, k))
    full = lambda a: pl.BlockSpec(a.shape, lambda i: (0,) * a.ndim)
    y, sf = pl.pallas_call(
        functools.partial(_s5_scan_kernel, rev=rev, emit=emit),
        grid=(ntile,),
        in_specs=[pl.BlockSpec((tm, d), tile), pl.BlockSpec((1, d), lambda i: (0, 0)), mod(0), mod(1),
                  full(bm), full(cm), full(lam), full(s0)],
        out_specs=[pl.BlockSpec((tm, d), tile if emit else (lambda i: (0, 0))), full(s0)],
        out_shape=[jax.ShapeDtypeStruct((n_rows if emit else tm, d), F32), jax.ShapeDtypeStruct(s0.shape, F32)],
        scratch_shapes=[pltpu.VMEM(s0.shape, F32)] + [pltpu.VMEM((tm, bm.shape[-1]), F32)] * bm.shape[0],
        compiler_params=_cparams("arbitrary"),
        name="s5_scan_bwd" if rev else "s5_scan_fwd",
    )(x, norm_g.reshape(1, d), modt, modt, bm, cm, lam, s0)
    return (y if emit else None), sf


def _s5_out_prologue(x_ref, yf_ref, yb_ref, g_ref, sh_ref, sc_ref, dk_ref):
    u = _norm_mod(x_ref[...], g_ref[...], sc_ref[...], sh_ref[...])
    return _gelu(yf_ref[...] + yb_ref[...] + dk_ref[...] * u)


def _s5_mixer(x, x_ctx, modt, modt_ctx, norm_g, prm, *, ctx_out):
    d = x.shape[-1]
    w = [_s5_weights(prm['lam_re'][dr], prm['lam_im'][dr], prm['log_step'][dr], prm['b_re'][dr], prm['b_im'][dr],
                     prm['c_re'][dr], prm['c_im'][dr]) for dr in range(2)]
    zero = jnp.zeros(w[0][2].shape, F32)
    ys, ys_ctx = [], []
    for dr in range(2):
        bm, cm, lam = w[dr]
        yc, sc = _s5_scan(x_ctx, modt_ctx, norm_g, bm, cm, lam, zero, rev=dr == 1, emit=ctx_out)
        yl, _ = _s5_scan(x, modt, norm_g, bm, cm, lam, sc, rev=dr == 1)
        ys.append(yl)
        ys_ctx.append(yc)
    w_glu = prm['w_glu'].astype(BF16)
    b_glu = prm['b_glu'].reshape(1, -1)

    def out(xa, ya, mt):
        tm = _row_tile(xa.shape[0], cap=ROW_TILE_CAP // 2)
        pro_args = [xa, ya[0], ya[1], norm_g.reshape(1, d), mt, mt, prm['d'].reshape(1, d)]
        pro_specs = [_row_spec(tm, d), _row_spec(tm, d), _row_spec(tm, d), _const_spec((1, d)),
                     _mod_spec(0, d, 10 ** 9), _mod_spec(1, d, 10 ** 9), _const_spec((1, d))]
        return _outproj(_s5_out_prologue, pro_args, pro_specs, d, xa, mt, 10 ** 9, w_glu, b_glu, tm=tm, name="s5_out")

    return out(x, ys, modt), (out(x_ctx, ys_ctx, modt_ctx) if ctx_out else None)


def _lru_in_kernel(x_ref, xb_ref, xa_ref, g_ref, sh_ref, sc_ref, wr_ref, wy_ref, cw_ref, cb_ref, xc_ref, yg_ref, h_ref,
                   *, rs, hb, ha):
    i, j = pl.program_id(0), pl.program_id(1)
    tm = x_ref.shape[0]

    @pl.when(j == 0)
    def _():
        _fill_h(h_ref, x_ref, xb_ref, xa_ref, g_ref, sc_ref, sh_ref, i == 0, i == pl.num_programs(0) - 1, hb, ha)

    p = _dot(h_ref[...], wr_ref[...])
    xc_ref[...] = _conv_rows(p, cw_ref[...], tm, rs, hb) + cb_ref[...]
    yg_ref[...] = _gelu(_dot(h_ref[hb:hb + tm, :], wy_ref[...]))


def _lru_in(x, modt, norm_g, w_in, conv_w, conv_b):
    n_rows, d = x.shape
    wd = conv_w.shape[-1]
    rs = SUBLANES
    hb, ha = 2 * rs, rs
    tm = _row_tile(n_rows)
    tn = 4 * LANES
    nj = wd // tn
    specs = _halo_specs(tm, hb, ha, d, n_rows) + [
        _const_spec((1, d)), _mod_spec(0, d, 10 ** 9), _mod_spec(1, d, 10 ** 9),
        pl.BlockSpec((d, tn), lambda i, j: (0, j)), pl.BlockSpec((d, tn), lambda i, j: (0, j + nj)),
        pl.BlockSpec((conv_w.shape[0], tn), lambda i, j: (0, j)), pl.BlockSpec((1, tn), lambda i, j: (0, j))]
    out_spec = pl.BlockSpec((tm, tn), lambda i, j: (i, j))
    return pl.pallas_call(
        functools.partial(_lru_in_kernel, rs=rs, hb=hb, ha=ha),
        grid=(n_rows // tm, nj),
        in_specs=specs,
        out_specs=[out_spec, out_spec],
        out_shape=[jax.ShapeDtypeStruct((n_rows, wd), F32)] * 2,
        scratch_shapes=[pltpu.VMEM((tm + hb + ha, d), BF16)],
        compiler_params=_cparams("parallel", "arbitrary"),
        name="lru_in",
    )(x, x, x, norm_g.reshape(1, d), modt, modt, w_in, w_in, conv_w, conv_b.reshape(1, wd))


def _lru_scan_kernel(xc_ref, wa_ref, wx_ref, ba_ref, bx_ref, nsp_ref, h0_ref, hs_ref, hf_ref, st_ref, a_ref, b_ref,
                     *, rev, emit):
    i = pl.program_id(0)
    tm = xc_ref.shape[0]
    nt = tm // SUBLANES
    nblk, bw, _ = wa_ref.shape

    @pl.when(i == 0)
    def _():
        st_ref[...] = h0_ref[...]
        if not emit:
            hs_ref[...] = jnp.zeros_like(hs_ref)

    for k in range(nblk):
        cols = slice(k * bw, (k + 1) * bw)
        xc = xc_ref[:, cols]
        xcb = xc.astype(BF16)
        r = jax.nn.sigmoid(_dot(xcb, wa_ref[k]) + ba_ref[:, cols])
        gi = jax.nn.sigmoid(_dot(xcb, wx_ref[k]) + bx_ref[:, cols])
        log_a = nsp_ref[:, cols] * r
        a = jnp.exp(log_a)
        a_ref[:, cols] = a
        b_ref[:, cols] = jnp.sqrt(1.0 - a * a) * (gi * xc)

    def step(t, h):
        tt = nt - 1 - t if rev else t
        r0 = pl.multiple_of(tt * SUBLANES, SUBLANES)
        h = a_ref[pl.ds(r0, SUBLANES), :] * h + b_ref[pl.ds(r0, SUBLANES), :]
        if emit:
            hs_ref[pl.ds(r0, SUBLANES), :] = h
        return h

    st_ref[...] = lax.fori_loop(0, nt, step, st_ref[...], unroll=8)

    @pl.when(i == pl.num_programs(0) - 1)
    def _():
        hf_ref[...] = st_ref[...]


def _lru_scan(xc, wa, wx, ba, bx, nsp, h0, *, rev, emit=True):
    n_rows, wd = xc.shape
    tm = _row_tile(n_rows)
    ntile = n_rows // tm
    tile = (lambda i: (ntile - 1 - i, 0)) if rev else (lambda i: (i, 0))
    full = lambda a: pl.BlockSpec(a.shape, lambda i: (0,) * a.ndim)
    hs, hf = pl.pallas_call(
        functools.partial(_lru_scan_kernel, rev=rev, emit=emit),
        grid=(ntile,),
        in_specs=[pl.BlockSpec((tm, wd), tile), full(wa), full(wx), full(ba), full(bx), full(nsp), full(h0)],
        out_specs=[pl.BlockSpec((tm, wd), tile if emit else (lambda i: (0, 0))), full(h0)],
        out_shape=[jax.ShapeDtypeStruct((n_rows if emit else tm, wd), F32), jax.ShapeDtypeStruct(h0.shape, F32)],
        scratch_shapes=[pltpu.VMEM(h0.shape, F32), pltpu.VMEM((tm, wd), F32), pltpu.VMEM((tm, wd), F32)],
        compiler_params=_cparams("arbitrary"),
        name="lru_scan_bwd" if rev else "lru_scan_fwd",
    )(xc, wa, wx, ba, bx, nsp, h0)
    return (hs if emit else None), hf


def _lru_out_prologue(yg_ref, hf_ref, hb_ref):
    return yg_ref[...] * (hf_ref[...] + hb_ref[...])


def _lru_mixer(x, x_ctx, modt, modt_ctx, norm_g, prm, *, ctx_out):
    d = x.shape[-1]
    w_in = prm['w_in'].astype(BF16)
    wd = prm['conv_w'].shape[-1]
    xc_l, yg_l = _lru_in(x, modt, norm_g, w_in, prm['conv_w'], prm['conv_b'])
    xc_c, yg_c = _lru_in(x_ctx, modt_ctx, norm_g, w_in, prm['conv_w'], prm['conv_b'])
    zero = jnp.zeros((SUBLANES, wd), F32)
    hs_l, hs_c = [], []
    for dr in range(2):
        wa, wx = prm['w_a'][dr].astype(BF16), prm['w_x'][dr].astype(BF16)
        ba, bx = prm['b_a'][dr].reshape(1, wd), prm['b_x'][dr].reshape(1, wd)
        nsp = (-LRU_C * jax.nn.softplus(-prm['lam'][dr].astype(F32))).reshape(1, wd)
        hc, hfin = _lru_scan(xc_c, wa, wx, ba, bx, nsp, zero, rev=dr == 1, emit=ctx_out)
        hl, _ = _lru_scan(xc_l, wa, wx, ba, bx, nsp, hfin, rev=dr == 1)
        hs_l.append(hl)
        hs_c.append(hc)
    w_out = prm['w_out'].astype(BF16)

    def out(xa, yg, hs, mt):
        tm = _row_tile(xa.shape[0], cap=ROW_TILE_CAP // 2)
        specs = [_row_spec(tm, wd)] * 3
        return _outproj(_lru_out_prologue, [yg, hs[0], hs[1]], specs, wd, xa, mt, 10 ** 9, w_out, tm=tm, name="lru_out")

    return out(x, yg_l, hs_l, modt), (out(x_ctx, yg_c, hs_c, modt_ctx) if ctx_out else None)


def _rope_tables(length):
    rows = length // GRID_W
    t = jnp.arange(length, dtype=jnp.int32)
    row = (t // GRID_W).astype(F32) - (rows - 1) / 2.0
    col = (t % GRID_W).astype(F32) - (GRID_W - 1) / 2.0
    quarter = RET_DK // 4
    inv_freq = ROPE_BASE ** (-jnp.arange(quarter, dtype=F32) / quarter)
    ar, ac = row[:, None] * inv_freq[None, :], col[:, None] * inv_freq[None, :]
    cos = jnp.concatenate([jnp.cos(ar), jnp.cos(ar), jnp.cos(ac), jnp.cos(ac)], axis=-1)
    sin = jnp.concatenate([-jnp.sin(ar), jnp.sin(ar), -jnp.sin(ac), jnp.sin(ac)], axis=-1)
    return cos, sin


def _ret_in_kernel(x_ref, g_ref, sh_ref, sc_ref, w_ref, cos_ref, sin_ref, p_ref, h_ref, *, rotate, n_k, n_v):
    j = pl.program_id(1)
    tn = w_ref.shape[1]

    @pl.when(j == 0)
    def _():
        h_ref[...] = _norm_mod(x_ref[...], g_ref[...], sc_ref[...], sh_ref[...]).astype(BF16)

    is_k = j < n_k
    is_q = (j >= n_k + n_v) & (j < 2 * n_k + n_v)

    @pl.when(is_k | is_q)
    def _():
        scale = jnp.where(is_q, RET_DK ** -0.5, 1.0)
        for hd in range(tn // RET_DK):
            ph = _dot(h_ref[...], w_ref[:, hd * RET_DK:(hd + 1) * RET_DK])
            for s in range(RET_DK // LANES):
                tcols = slice(s * LANES, (s + 1) * LANES)
                ps = ph[:, tcols]
                if rotate:
                    rot = pltpu.roll(ps, LANES // 2, axis=1)
                    ps = ps * cos_ref[:, tcols] + rot * sin_ref[:, tcols]
                p_ref[:, hd * RET_DK + s * LANES:hd * RET_DK + (s + 1) * LANES] = (ps * scale).astype(p_ref.dtype)

    @pl.when(jnp.logical_not(is_k | is_q))
    def _():
        p_ref[...] = _dot(h_ref[...], w_ref[...]).astype(p_ref.dtype)


def _ret_in(x, modt, tiles_per_mod, norm_g, w_in, seq_rows, *, rotate):
    n_rows, d = x.shape
    n_out = w_in.shape[1]
    tm = _row_tile(seq_rows)
    tn = RET_HEADS * RET_DK
    tiles_per_seq = seq_rows // tm
    cos, sin = _rope_tables(seq_rows)
    hk, hv = RET_HEADS * RET_DK, RET_HEADS * RET_DV
    tab_spec = pl.BlockSpec((tm, RET_DK), lambda i, j: (i % tiles_per_seq, 0))
    return pl.pallas_call(
        functools.partial(_ret_in_kernel, rotate=rotate, n_k=hk // tn, n_v=hv // tn),
        grid=(n_rows // tm, n_out // tn),
        in_specs=[_row_spec(tm, d), _const_spec((1, d)), _mod_spec(0, d, tiles_per_mod), _mod_spec(1, d, tiles_per_mod),
                  pl.BlockSpec((d, tn), lambda i, j: (0, j)), tab_spec, tab_spec],
        out_specs=pl.BlockSpec((tm, tn), lambda i, j: (i, j)),
        out_shape=jax.ShapeDtypeStruct((n_rows, n_out), BF16),
        scratch_shapes=[pltpu.VMEM((tm, d), BF16)],
        compiler_params=_cparams("parallel", "arbitrary"),
        name="ret_in",
    )(x, norm_g.reshape(1, d), modt, modt, w_in, cos, sin)


def _ret_tables(c):
    log_g = np.log1p(-np.power(2.0, -5.0 - np.arange(RET_HEADS, dtype=np.float64)))
    idx = np.arange(c, dtype=np.float64)
    diff = idx[:, None] - idx[None, :]
    fwd = np.where(diff >= 0, np.exp(np.where(diff >= 0, diff, 0.0)[None] * log_g[:, None, None]), 0.0)
    bwd = np.where(diff < 0, np.exp(np.where(diff < 0, -diff, 0.0)[None] * log_g[:, None, None]), 0.0)
    xi_f = np.exp((idx + 1.0)[None, :] * log_g[:, None])
    zeta_f = np.exp((c - 1.0 - idx)[None, :] * log_g[:, None])
    xi_b = np.exp((c - idx)[None, :] * log_g[:, None])
    zeta_b = np.exp(idx[None, :] * log_g[:, None])
    dmask = np.stack([fwd, bwd]).astype(np.float32)
    xi = np.stack([xi_f, xi_b])[..., None].astype(np.float32)
    zeta = np.stack([zeta_f, zeta_b])[..., None].astype(np.float32)
    g_blk = [float(v) for v in np.exp(c * log_g).astype(np.float32)]
    return jnp.asarray(dmask), jnp.asarray(xi), jnp.asarray(zeta), g_blk


def _ret_scan_kernel(kf_ref, v0f_ref, v1f_ref, qf_ref, kb_ref, v0b_ref, v1b_ref, qb_ref, dm_ref, xi_ref, zt_ref, r0_ref,
                     of_ref, ob_ref, rf_ref, r_ref, *, g_blk, emit):
    c = pl.program_id(1)
    dir_refs = ((kf_ref, v0f_ref, v1f_ref, qf_ref, of_ref), (kb_ref, v0b_ref, v1b_ref, qb_ref, ob_ref))

    @pl.when(c == 0)
    def _():
        r_ref[...] = r0_ref[...]
        if not emit:
            of_ref[...] = jnp.zeros_like(of_ref)
            ob_ref[...] = jnp.zeros_like(ob_ref)

    hv_half = v0f_ref.shape[1] // RET_DV
    r_olds = [[r_ref[d, h] for h in range(RET_HEADS)] for d in range(2)]
    r_news = [[None] * RET_HEADS for _ in range(2)]
    for h in range(RET_HEADS):
        for d in range(2):
            k_ref, v0_ref, v1_ref, q_ref, o_ref = dir_refs[d]
            kh = k_ref[:, h * RET_DK:(h + 1) * RET_DK]
            v_ref = v0_ref if h < hv_half else v1_ref
            hh = h % hv_half
            vh = v_ref[:, hh * RET_DV:(hh + 1) * RET_DV]
            r_old = r_olds[d][h]
            if emit:
                qh = q_ref[:, h * RET_DK:(h + 1) * RET_DK]
                s = _dot_nt(qh, kh) * dm_ref[d, h]
                o = _dot(s.astype(BF16), vh) + _dot(qh, r_old.astype(BF16)) * xi_ref[d, h]
                o_ref[:, h * RET_DV:(h + 1) * RET_DV] = o.astype(o_ref.dtype)
            r_news[d][h] = g_blk[h] * r_old + _dot_tn((kh * zt_ref[d, h]).astype(BF16), vh)
    for d in range(2):
        for h in range(RET_HEADS):
            r_ref[d, h] = r_news[d][h]

    @pl.when(c == pl.num_programs(1) - 1)
    def _():
        rf_ref[...] = r_ref[...]


def _ret_scan(p, r0, batch, seq_rows, *, emit=True):
    c = min(RET_CHUNK, seq_rows)
    nc = seq_rows // c
    hk, hv = RET_HEADS * RET_DK, RET_HEADS * RET_DV
    dmask, xi, zeta, g_blk = _ret_tables(c)
    n_rows = p.shape[0]
    rows = (lambda b, cc: b * nc + cc, lambda b, cc: b * nc + nc - 1 - cc)
    kcols = lambda rw: [pl.BlockSpec((c, hk), lambda b, cc, blk=blk: (rw(b, cc), blk)) for blk in range(4)]
    full = lambda a: pl.BlockSpec(a.shape, lambda b, cc: (0,) * a.ndim)
    st_spec = pl.BlockSpec((2, None, RET_HEADS, RET_DK, RET_DV), lambda b, cc: (0, b, 0, 0, 0))
    o_spec = lambda rw: pl.BlockSpec((c, hv), (lambda b, cc: (rw(b, cc), 0)) if emit else (lambda b, cc: (0, 0)))
    o_shape = jax.ShapeDtypeStruct((n_rows if emit else c, hv), BF16)
    assert hv == 2 * hk
    of, ob, rf = pl.pallas_call(
        functools.partial(_ret_scan_kernel, g_blk=g_blk, emit=emit),
        grid=(batch, nc),
        in_specs=kcols(rows[0]) + kcols(rows[1]) + [full(dmask), full(xi), full(zeta), st_spec],
        out_specs=[o_spec(rows[0]), o_spec(rows[1]), st_spec],
        out_shape=[o_shape, o_shape, jax.ShapeDtypeStruct(r0.shape, F32)],
        scratch_shapes=[pltpu.VMEM((2, RET_HEADS, RET_DK, RET_DV), F32)],
        compiler_params=_cparams("arbitrary", "arbitrary"),
        name="ret_scan",
    )(*([p] * 8), dmask, xi, zeta, r0)
    return of, ob, rf


def _ret_out_prologue(of_ref, ob_ref, gate_ref, ng_ref):
    o = of_ref[...].astype(F32) + ob_ref[...].astype(F32)
    parts = []
    for h in range(RET_HEADS):
        oh = o[:, h * RET_DV:(h + 1) * RET_DV]
        parts.append(oh * lax.rsqrt(jnp.mean(oh * oh, axis=-1, keepdims=True) + NORM_EPS))
    on = (jnp.concatenate(parts, axis=-1) * ng_ref[...])
    return jax.nn.silu(gate_ref[...].astype(F32)) * on


def _ret_mixer(x, x_ctx, modt, modt_ctx, norm_g, prm, batch, seq, seq_ctx, *, ctx_out):
    d = x.shape[-1]
    w_in = prm['w_in'].astype(BF16)
    hk, hv = RET_HEADS * RET_DK, RET_HEADS * RET_DV
    tiles_per_mod = seq // _row_tile(seq)
    p_l = _ret_in(x, modt, tiles_per_mod, norm_g, w_in, seq, rotate=True)
    p_c = _ret_in(x_ctx, modt_ctx, 10 ** 9, norm_g, w_in, seq_ctx, rotate=False)
    r0 = jnp.zeros((2, batch, RET_HEADS, RET_DK, RET_DV), F32)
    of_c, ob_c, r_c = _ret_scan(p_c, r0, batch, seq_ctx, emit=ctx_out)
    of_l, ob_l, _ = _ret_scan(p_l, r_c, batch, seq)
    w_out = prm['w_out'].astype(BF16)
    ng = prm['norm_g'].reshape(1, hv)

    def out(xa, pa, of, ob, mt, tpm, seq_rows):
        tm = _row_tile(seq_rows, cap=ROW_TILE_CAP // 2)
        gcol = (2 * hk + hv) // hv
        specs = [_row_spec(tm, hv), _row_spec(tm, hv), pl.BlockSpec((tm, hv), lambda i, j: (i, gcol)), _const_spec((1, hv))]
        return _outproj(_ret_out_prologue, [of, ob, pa, ng], specs, hv, xa, mt, tpm * (_row_tile(seq_rows) // tm), w_out,
                        tm=tm, name="ret_out")

    y_l = out(x, p_l, of_l, ob_l, modt, tiles_per_mod, seq)
    y_c = out(x_ctx, p_c, of_c, ob_c, modt_ctx, 10 ** 9, seq_ctx) if ctx_out else None
    return y_l, y_c


GDN_IN_TILE = 8 * LANES


def _gdn_in_kernel(x_ref, xb_ref, xa_ref, g_ref, sh_ref, sc_ref, w_ref, cw_ref, wba_ref, o_ref, ba_ref, h_ref,
                   *, tiles_per_seq, n_kq, n_conv):
    i, j = pl.program_id(0), pl.program_id(1)
    tm = x_ref.shape[0]
    hb = ha = SUBLANES
    heads = o_ref.shape[0]

    @pl.when(j == 0)
    def _():
        first = i % tiles_per_seq == 0
        last = i % tiles_per_seq == tiles_per_seq - 1
        _fill_h(h_ref, x_ref, xb_ref, xa_ref, g_ref, sc_ref, sh_ref, first, last, hb, ha)

    pair = 2 * LANES

    def conv_silu(s2):
        p = _dot(h_ref[...], w_ref[:, s2 * pair:(s2 + 1) * pair])
        return jax.nn.silu(_conv_rows(p, cw_ref[:, s2 * pair:(s2 + 1) * pair], tm, 1, hb))

    @pl.when(j < n_kq)
    def _():
        scale = jnp.where(j >= n_kq // 2, GDN_DK ** -0.5, 1.0)
        for s2 in range(heads // 2):
            a = conv_silu(s2)
            for e in range(2):
                ah = a[:, e * LANES:(e + 1) * LANES]
                o_ref[2 * s2 + e] = (ah * (lax.rsqrt(jnp.sum(ah * ah, axis=-1, keepdims=True) + 1e-6) * scale)
                                     ).astype(o_ref.dtype)

    @pl.when((j >= n_kq) & (j < n_conv))
    def _():
        for s2 in range(heads // 2):
            a = conv_silu(s2)
            for e in range(2):
                o_ref[2 * s2 + e] = a[:, e * LANES:(e + 1) * LANES].astype(o_ref.dtype)

    @pl.when(j >= n_conv)
    def _():
        for s2 in range(heads // 2):
            p = _dot(h_ref[hb:hb + tm, :], w_ref[:, s2 * pair:(s2 + 1) * pair])
            for e in range(2):
                o_ref[2 * s2 + e] = p[:, e * LANES:(e + 1) * LANES].astype(o_ref.dtype)

    @pl.when(j == pl.num_programs(1) - 1)
    def _():
        ba_ref[...] = _dot(h_ref[hb:hb + tm, :], wba_ref[...])


def _gdn_in(x, modt, tiles_per_mod, norm_g, w_all, cw_all, w_ba, seq_rows):
    n_rows, d = x.shape
    n_out = w_all.shape[1]
    tm = _row_tile(seq_rows)
    tn = GDN_IN_TILE
    hb = ha = SUBLANES
    heads = tn // LANES
    n_kq = 2 * GDN_K_HEADS * GDN_DK // tn
    n_conv = n_kq + GDN_V_HEADS * GDN_DV // tn
    kern = functools.partial(_gdn_in_kernel, tiles_per_seq=seq_rows // tm, n_kq=n_kq, n_conv=n_conv)
    specs = _halo_specs(tm, hb, ha, d, n_rows) + [
        _const_spec((1, d)), _mod_spec(0, d, tiles_per_mod), _mod_spec(1, d, tiles_per_mod),
        pl.BlockSpec((d, tn), lambda i, j: (0, j)), pl.BlockSpec((cw_all.shape[0], tn), lambda i, j: (0, j)),
        _const_spec(w_ba.shape)]
    return pl.pallas_call(
        kern,
        grid=(n_rows // tm, n_out // tn),
        in_specs=specs,
        out_specs=[pl.BlockSpec((heads, tm, LANES), lambda i, j: (j, i, 0)), pl.BlockSpec((tm, LANES), lambda i, j: (i, 0))],
        out_shape=[jax.ShapeDtypeStruct((n_out // LANES, n_rows, LANES), BF16), jax.ShapeDtypeStruct((n_rows, LANES), F32)],
        scratch_shapes=[pltpu.VMEM((tm + hb + ha, d), BF16)],
        compiler_params=_cparams("parallel", "arbitrary"),
        name="gdn_in",
    )(x, x, x, norm_g.reshape(1, d), modt, modt, w_all, cw_all, w_ba)


def _split3(a):
    hi = a.astype(BF16)
    lo = (a - hi.astype(F32)).astype(BF16)
    return hi, lo


def _dot3s(a, b):
    ah, al = a
    bh, bl = b
    m = ah.shape[0]
    both = _dot(jnp.concatenate([ah, al], axis=0), bh)
    return both[:m] + (both[m:] + _dot(ah, bl))


def _cumsum_rows(x, rev):
    n = x.shape[0]
    row = lax.broadcasted_iota(jnp.int32, x.shape, 0)
    shift = 1
    while shift < n:
        if rev:
            x = x + jnp.where(row < n - shift, pltpu.roll(x, n - shift, axis=0), 0.0)
        else:
            x = x + jnp.where(row >= shift, pltpu.roll(x, shift, axis=0), 0.0)
        shift *= 2
    return x


GDN_GROUP = 4


def _gdn_scan_kernel(kf_ref, qf_ref, vf_ref, baf_ref, kb_ref, qb_ref, vb_ref, bab_ref, an_ref, dtb_ref,
                     s0_ref, of_ref, ob_ref, sf_ref, s_ref, *, emit):
    cidx = pl.program_id(1)
    c = kf_ref.shape[1]
    rep = GDN_V_HEADS // GDN_K_HEADS
    dir_refs = ((kf_ref, qf_ref, vf_ref, baf_ref, of_ref), (kb_ref, qb_ref, vb_ref, bab_ref, ob_ref))

    @pl.when(cidx == 0)
    def _():
        s_ref[...] = s0_ref[...]
        if not emit:
            of_ref[...] = jnp.zeros_like(of_ref)
            ob_ref[...] = jnp.zeros_like(ob_ref)

    def expand(src, first_lane, g):
        cols = [jnp.broadcast_to(src[:, first_lane + gs * g + h:first_lane + gs * g + h + 1], (c, GDN_DV)) for h in range(gs)]
        per = LANES // c
        narrow = []
        for p in range(gs // per):
            blk = cols[per * p + per - 1]
            for h in range(per - 2, -1, -1):
                blk = jnp.where(lane_in_vreg < (h + 1) * c, cols[per * p + h], blk)
            narrow.append(blk)
        return jnp.concatenate(cols, axis=1), jnp.concatenate(narrow, axis=1)

    gs = GDN_GROUP
    lane_in_vreg = lax.broadcasted_iota(jnp.int32, (c, LANES), 1)
    wide = gs * c
    ii = lax.broadcasted_iota(jnp.int32, (c, wide), 0)
    jj = lax.broadcasted_iota(jnp.int32, (c, wide), 1) % c
    br = lax.broadcasted_iota(jnp.int32, (wide, wide), 0) // c
    bc = lax.broadcasted_iota(jnp.int32, (wide, wide), 1) // c
    blockmask = (br == bc).astype(F32).astype(BF16)
    nk, nv = GDN_K_HEADS, GDN_V_HEADS
    zc = jnp.zeros((c, GDN_DK), BF16)
    zs = jnp.zeros((GDN_DK, GDN_DV), BF16)
    zr = jnp.zeros((c, GDN_DV), BF16)

    gcs, bts, incl, strict, eye, last = [], [], [], [], [], []
    for d in range(2):
        rev = d == 1
        ba = dir_refs[d][3][...]
        z = ba + dtb_ref[d]
        sp = jnp.maximum(z, 0.0) + jnp.log(1.0 + jnp.exp(-jnp.abs(z)))
        gcs.append(_cumsum_rows(an_ref[d] * sp, rev))
        bts.append(jax.nn.sigmoid(ba))
        diff = (jj - ii) if rev else (ii - jj)
        incl.append(diff >= 0)
        strict.append(diff > 0)
        eye.append((diff == 0).astype(F32))
        last.append(0 if rev else c - 1)

    def bdiag(blocks, zero):
        n = len(blocks)
        return jnp.concatenate([jnp.concatenate([blocks[r] if r == q else zero for q in range(n)], axis=1)
                                for r in range(n)], axis=0)

    def bd_tile(a16):
        return jnp.concatenate([a16] * gs, axis=0) * blockmask

    def dot3_bd(lhs, rhs_hi, rhs_lo):
        lh, ll = _split3(lhs)
        m = lhs.shape[0]
        both = _dot(jnp.concatenate([lh, ll], axis=0), rhs_hi)
        return both[:m] + (both[m:] + _dot(lh, rhs_lo))

    pieces = lambda a: [a[:, h * GDN_DV:(h + 1) * GDN_DV] for h in range(gs)]
    chains = [(d, g) for g in range(nv // gs) for d in range(2)]
    ks = [[dir_refs[d][0][h].astype(F32) for h in range(nk)] for d in range(2)]
    qs = [[dir_refs[d][1][h].astype(F32) for h in range(nk)] for d in range(2)] if emit else None
    vs = [[dir_refs[d][2][h].astype(F32) for h in range(nv)] for d in range(2)]
    s_olds = [[s_ref[d, h] for h in range(nv)] for d in range(2)]

    xs, ts, attns, gc5s, bt5s = [], [], [], [], []
    for d, g in chains:
        k0, k1 = ks[d][2 * g], ks[d][2 * g + 1]
        k016, k116 = k0.astype(BF16), k1.astype(BF16)
        rhs_t = jnp.concatenate([jnp.concatenate([k016, zc], axis=1)] * rep
                                + [jnp.concatenate([zc, k116], axis=1)] * rep, axis=0)
        if emit:
            lhs = jnp.concatenate([jnp.concatenate([k0, qs[d][2 * g]], axis=0),
                                   jnp.concatenate([k1, qs[d][2 * g + 1]], axis=0)], axis=1).astype(BF16)
        else:
            lhs = jnp.concatenate([k016, k116], axis=1)
        gram = _dot_nt(lhs, rhs_t)
        gc5, gcm = expand(gcs[d], d * 2 * nv + nv, g)
        bt5, btm = expand(bts[d], d * 2 * nv, g)
        gc5s.append(gc5)
        bt5s.append(bt5)
        grow = jnp.sum(gcm * eye[d], axis=0, keepdims=True)
        dec = jnp.where(incl[d], jnp.exp(jnp.where(incl[d], gcm - grow, 0.0)), 0.0)
        x = jnp.where(strict[d], -(gram[:c] * btm) * dec, 0.0)
        xs.append(x)
        ts.append(eye[d] + x)
        attns.append(gram[c:] * dec if emit else None)

    splits = [_split3(x) for x in xs]
    xs = [dot3_bd(x, bd_tile(xh), bd_tile(xl)) for x, (xh, xl) in zip(xs, splits)]
    nlev = int(math.log2(c)) - 1
    for lvl in range(1, nlev + 1):
        splits = [_split3(x) for x in xs]
        if lvl < nlev:
            ps = [dot3_bd(jnp.concatenate([t, x], axis=0), bd_tile(xh), bd_tile(xl))
                  for t, x, (xh, xl) in zip(ts, xs, splits)]
            ts = [t + p[:c] for t, p in zip(ts, ps)]
            xs = [p[c:] for p in ps]
        else:
            ts = [t + dot3_bd(t, bd_tile(xh), bd_tile(xl)) for t, (xh, xl) in zip(ts, splits)]

    kcats, p1s = [], []
    for n, (d, g) in enumerate(chains):
        k0, k1 = ks[d][2 * g], ks[d][2 * g + 1]
        gc5, bt5 = gc5s[n], bt5s[n]
        eg5 = jnp.exp(gc5)
        kcat = jnp.concatenate([k0] * rep + [k1] * rep, axis=1)
        kbe = kcat * bt5 * eg5
        s16 = [s_olds[d][gs * g + h].astype(BF16) for h in range(gs)]
        if emit:
            qe = jnp.concatenate([qs[d][2 * g]] * rep + [qs[d][2 * g + 1]] * rep, axis=1) * eg5
            top = jnp.concatenate([kbe, qe], axis=0).astype(BF16)
        else:
            top = kbe.astype(BF16)
        p1s.append(jnp.concatenate([_dot(top[:, 2 * pr * GDN_DK:(2 * pr + 2) * GDN_DK], bdiag(s16[2 * pr:2 * pr + 2], zs))
                                    for pr in range(gs // 2)], axis=1))
        kcats.append(kcat)

    bdvs = []
    for n, (d, g) in enumerate(chains):
        vcat = jnp.concatenate(vs[d][gs * g:gs * (g + 1)], axis=1)
        rh, rl = _split3(vcat * bt5s[n] - p1s[n][:c])
        vn16 = dot3_bd(ts[n], bdiag(pieces(rh), zr), bdiag(pieces(rl), zr)).astype(BF16)
        bdvs.append(bdiag(pieces(vn16), zr))

    outs = [[None] * nv for _ in range(2)]
    s_news = [[None] * nv for _ in range(2)]
    for n, (d, g) in enumerate(chains):
        gl5 = gc5s[n][last[d]:last[d] + 1, :]
        if emit:
            o = pieces(p1s[n][c:] + _dot(attns[n].astype(BF16), bdvs[n]))
        kd = (kcats[n] * jnp.exp(gl5 - gc5s[n])).astype(BF16)
        sn = _dot_tn(jnp.concatenate(pieces(kd), axis=0), bdvs[n])
        egl = jnp.exp(gl5)
        for h in range(gs):
            cols = slice(h * GDN_DV, (h + 1) * GDN_DV)
            s_news[d][gs * g + h] = s_olds[d][gs * g + h] * egl[:, cols] + sn[:, cols]
            if emit:
                outs[d][gs * g + h] = o[h]
    for d in range(2):
        for hv in range(nv):
            if emit:
                dir_refs[d][4][hv] = outs[d][hv].astype(dir_refs[d][4].dtype)
            s_ref[d, hv] = s_news[d][hv]

    @pl.when(cidx == pl.num_programs(1) - 1)
    def _():
        sf_ref[...] = s_ref[...]


def _gdn_scan(kqvz, ba, an, dtb, s0, batch, seq_rows, *, emit=True):
    c = min(GDN_CHUNK, seq_rows)
    nc = seq_rows // c
    n_rows = kqvz.shape[1]
    rows = (lambda b, cc: b * nc + cc, lambda b, cc: b * nc + nc - 1 - cc)
    kh, vh = GDN_K_HEADS, GDN_V_HEADS
    st_spec = pl.BlockSpec((2, None, vh, GDN_DK, GDN_DV), lambda b, cc: (0, b, 0, 0, 0))
    full = lambda a: pl.BlockSpec(a.shape, lambda b, cc: (0,) * a.ndim)

    def dir_specs(rw):
        return [pl.BlockSpec((kh, c, LANES), lambda b, cc: (0, rw(b, cc), 0)),
                pl.BlockSpec((kh, c, LANES), lambda b, cc: (1, rw(b, cc), 0)),
                pl.BlockSpec((vh, c, LANES), lambda b, cc: (1, rw(b, cc), 0)),
                pl.BlockSpec((c, LANES), lambda b, cc: (rw(b, cc), 0))]

    def o_spec(rw):
        return pl.BlockSpec((vh, c, LANES), (lambda b, cc: (0, rw(b, cc), 0)) if emit else (lambda b, cc: (0, 0, 0)))

    o_shape = jax.ShapeDtypeStruct((vh, n_rows if emit else c, LANES), BF16)
    of, ob, sf = pl.pallas_call(
        functools.partial(_gdn_scan_kernel, emit=emit),
        grid=(batch, nc),
        in_specs=dir_specs(rows[0]) + dir_specs(rows[1]) + [full(an), full(dtb), st_spec],
        out_specs=[o_spec(rows[0]), o_spec(rows[1]), st_spec],
        out_shape=[o_shape, o_shape, jax.ShapeDtypeStruct(s0.shape, F32)],
        scratch_shapes=[pltpu.VMEM((2, vh, GDN_DK, GDN_DV), F32)],
        compiler_params=_cparams("arbitrary", "arbitrary"),
        name="gdn_scan",
    )(kqvz, kqvz, kqvz, ba, kqvz, kqvz, kqvz, ba, an, dtb, s0)
    return of, ob, sf


def _gdn_out_prologue(of_ref, ob_ref, z_ref, ng_ref):
    parts = []
    for h in range(GDN_V_HEADS):
        o = of_ref[h].astype(F32) + ob_ref[h].astype(F32)
        on = o * lax.rsqrt(jnp.mean(o * o, axis=-1, keepdims=True) + NORM_EPS) * ng_ref[...]
        parts.append(on * jax.nn.silu(z_ref[h].astype(F32)))
    return jnp.concatenate(parts, axis=-1)


def _gdn_mixer(x, x_ctx, modt, modt_ctx, norm_g, prm, batch, seq, seq_ctx, *, ctx_out):
    d = x.shape[-1]
    qk, vv = GDN_K_HEADS * GDN_DK, GDN_V_HEADS * GDN_DV
    ng2 = 2 * 2 * GDN_V_HEADS
    w_in, conv_w = prm['w_in'], prm['conv_w']
    w_all = jnp.concatenate([w_in[:, :qk], w_in[:, qk + vv + ng2:2 * qk + vv + ng2], w_in[:, qk:qk + vv],
                             w_in[:, 2 * qk + vv + ng2:]], axis=1).astype(BF16)
    w_ba = jnp.concatenate([w_in[:, qk + vv:qk + vv + ng2], jnp.zeros((d, LANES - ng2), w_in.dtype)], axis=1).astype(BF16)
    cw_all = jnp.concatenate([conv_w[:, :qk], conv_w[:, qk + vv:], conv_w[:, qk:qk + vv],
                              jnp.zeros((conv_w.shape[0], vv), conv_w.dtype)], axis=1)
    tiles_per_mod = seq // _row_tile(seq)
    p_l, ba_l = _gdn_in(x, modt, tiles_per_mod, norm_g, w_all, cw_all, w_ba, seq)
    p_c, ba_c = _gdn_in(x_ctx, modt_ctx, 10 ** 9, norm_g, w_all, cw_all, w_ba, seq_ctx)
    s0 = jnp.zeros((2, batch, GDN_V_HEADS, GDN_DK, GDN_DV), F32)

    def gate_lanes(p):
        rows = [jnp.zeros((LANES,), F32).at[dr * 2 * GDN_V_HEADS + GDN_V_HEADS:(dr + 1) * 2 * GDN_V_HEADS].set(p[dr])
                for dr in range(2)]
        return jnp.stack(rows).reshape(2, 1, LANES)

    an = gate_lanes(-jnp.exp(prm['a_log'].astype(F32)))
    dtb = gate_lanes(prm['dt_bias'].astype(F32))
    ocf, ocb, sc = _gdn_scan(p_c, ba_c, an, dtb, s0, batch, seq_ctx, emit=ctx_out)
    olf, olb, _ = _gdn_scan(p_l, ba_l, an, dtb, sc, batch, seq)
    os_l, os_c = [olf, olb], [ocf, ocb]
    w_out = prm['w_out'].astype(BF16)
    ng = prm['norm_g'].reshape(1, GDN_DV)
    vh = GDN_V_HEADS
    z_blk = (2 * GDN_K_HEADS + vh) // vh

    def out(xa, pa, oa, mt, tpm, seq_rows):
        tm = _row_tile(seq_rows, cap=ROW_TILE_CAP // 2)
        hspec = lambda blk: pl.BlockSpec((vh, tm, LANES), lambda i, j: (blk, i, 0))
        specs = [hspec(0), hspec(0), hspec(z_blk), _const_spec((1, GDN_DV))]
        return _outproj(_gdn_out_prologue, [oa[0], oa[1], pa, ng], specs, vv, xa, mt, tpm * (_row_tile(seq_rows) // tm),
                        w_out, tm=tm, name="gdn_out")

    y_l = out(x, p_l, os_l, modt, tiles_per_mod, seq)
    y_c = out(x_ctx, p_c, os_c, modt_ctx, 10 ** 9, seq_ctx) if ctx_out else None
    return y_l, y_c


def kernel(x, c, ctx, c_ctx, norm1_g, norm2_g, mod_w, mod_b, ffn_w_in, ffn_conv_w, ffn_conv_b, ffn_w_out, s5_lam_re, s5_lam_im, s5_log_step, s5_b_re, s5_b_im, s5_c_re, s5_c_im, s5_d, s5_w_glu, s5_b_glu, lru_w_in, lru_conv_w, lru_conv_b, lru_w_a, lru_b_a, lru_w_x, lru_b_x, lru_lam, lru_w_out, ret_w_in, ret_norm_g, ret_w_out, gdn_w_in, gdn_conv_w, gdn_a_log, gdn_dt_bias, gdn_norm_g, gdn_w_out, final_norm_g):
    B, L, D = x.shape
    Lc = ctx.shape[1]
    depth = mod_w.shape[0]
    assert B == SUBLANES and depth == 4, "time-major layers put the batch on the 8 sublanes; four mixer kinds"
    c16 = jnp.concatenate([c, c_ctx[None], jnp.zeros((16 - B - 1, D), F32)], 0)
    mods = _modulation(c16, mod_w, mod_b)

    def ffn(i, xa, mt, tiles_per_mod, rs, seq_rows, final_norm=False):
        return _ffn(xa, mt, tiles_per_mod, norm2_g[i], ffn_w_in[i].astype(BF16), ffn_conv_w[i], ffn_conv_b[i],
                    ffn_w_out[i].astype(BF16), final_norm_g, rs=rs, seq_rows=seq_rows, final_norm=final_norm)

    xt = jnp.swapaxes(x, 0, 1).reshape(L * B, D)
    ct = jnp.swapaxes(ctx, 0, 1).reshape(Lc * B, D)
    s5p = dict(lam_re=s5_lam_re[0], lam_im=s5_lam_im[0], log_step=s5_log_step[0], b_re=s5_b_re[0], b_im=s5_b_im[0],
               c_re=s5_c_re[0], c_im=s5_c_im[0], d=s5_d[0], w_glu=s5_w_glu[0], b_glu=s5_b_glu[0])
    lrup = dict(w_in=lru_w_in[0], conv_w=lru_conv_w[0], conv_b=lru_conv_b[0], w_a=lru_w_a[0], b_a=lru_b_a[0],
                w_x=lru_w_x[0], b_x=lru_b_x[0], lam=lru_lam[0], w_out=lru_w_out[0])
    for i, (mixer, prm) in enumerate(((_s5_mixer, s5p), (_lru_mixer, lrup))):
        mt = mods[i, :B][None]
        mtc = jnp.broadcast_to(mods[i, B][None, None, :], (1, SUBLANES, N_MOD * D))
        xt, ct = mixer(xt, ct, mt, mtc, norm1_g[i], prm, ctx_out=True)
        xt = ffn(i, xt, mt, 10 ** 9, SUBLANES, L * B)
        ct = ffn(i, ct, mtc, 10 ** 9, SUBLANES, Lc * B)

    xb = jnp.swapaxes(xt.reshape(L, B, D), 0, 1).reshape(B * L, D)
    cb = jnp.swapaxes(ct.reshape(Lc, B, D), 0, 1).reshape(B * Lc, D)
    retp = dict(w_in=ret_w_in[0], norm_g=ret_norm_g[0], w_out=ret_w_out[0])
    gdnp = dict(w_in=gdn_w_in[0], conv_w=gdn_conv_w[0], a_log=gdn_a_log[0], dt_bias=gdn_dt_bias[0],
                norm_g=gdn_norm_g[0], w_out=gdn_w_out[0])
    tiles_per_batch = L // _row_tile(L)
    for i, (mixer, prm) in ((2, (_ret_mixer, retp)), (3, (_gdn_mixer, gdnp))):
        ctx_out = i < depth - 1
        mt = jnp.broadcast_to(mods[i, :B][:, None, :], (B, SUBLANES, N_MOD * D))
        mtc = jnp.broadcast_to(mods[i, B][None, None, :], (1, SUBLANES, N_MOD * D))
        xb, cb_new = mixer(xb, cb, mt, mtc, norm1_g[i], prm, B, L, Lc, ctx_out=ctx_out)
        xb = ffn(i, xb, mt, tiles_per_batch, 1, L, final_norm=not ctx_out)
        if ctx_out:
            cb = ffn(i, cb_new, mtc, 10 ** 9, 1, Lc)
    return xb.reshape(B, L, D)
```

```python
import functools
import math

import numpy as np
import jax
import jax.numpy as jnp
from jax import lax
from jax.experimental import pallas as pl
from jax.experimental.pallas import tpu as pltpu

F32 = jnp.float32
BF16 = jnp.bfloat16

NORM_EPS = 1e-6
SUBLANES = 8
LANES = 128
VMEM_LIMIT_BYTES = 56 * 1024 * 1024
ROW_TILE_CAP = 1024

S5_GROUP = 16
S5_STATE = 64
S5_GB = 16
LRU_BLOCKS = 4
LRU_C = 8.0
RET_HEADS = 4
RET_DK = 256
RET_DV = 512
RET_CHUNK = 128
ROPE_BASE = 10000.0
GRID_W = 64
GDN_K_HEADS = 8
GDN_V_HEADS = 16
GDN_DK = 128
GDN_DV = 128
GDN_CHUNK = 64
N_MOD = 6


def _cparams(*sem):
    return pltpu.CompilerParams(dimension_semantics=sem, vmem_limit_bytes=VMEM_LIMIT_BYTES)


def _row_tile(rows, cap=None):
    cap = ROW_TILE_CAP if cap is None else cap
    tm = min(cap, rows)
    assert rows % tm == 0 and tm % SUBLANES == 0, (rows, tm)
    return tm


def _dot(a, b):
    return jnp.dot(a, b, preferred_element_type=F32)


def _dot_tn(a, b):
    return lax.dot_general(a, b, (((0,), (0,)), ((), ())), preferred_element_type=F32)


def _dot_nt(a, b):
    return lax.dot_general(a, b, (((1,), (1,)), ((), ())), preferred_element_type=F32)


def _gelu(x):
    return jax.nn.gelu(x, approximate=True)


def _sigmoid(x):
    return 1.0 / (1.0 + jnp.exp(-x))


def _silu(x):
    return x * _sigmoid(x)


def _tile8(v, rows):
    n = v.shape[-1]
    return jnp.broadcast_to(v[None], (rows // SUBLANES, SUBLANES, n)).reshape(rows, n)


def _norm_mod(x, g, sc, sh):
    ms = jnp.mean(x * x, axis=-1, keepdims=True)
    y = x * lax.rsqrt(ms + NORM_EPS) * g
    rows = x.shape[0]
    return y * (1.0 + _tile8(sc, rows)) + _tile8(sh, rows)


def _shift_rows(p, off, rows):
    if off % SUBLANES == 0:
        return p[off:off + rows]
    n = p.shape[0]
    base = (off // SUBLANES) * SUBLANES
    rolled = pltpu.roll(p, (n - (off - base)) % n, axis=0)
    return rolled[base:base + rows]


def _conv_rows(p_ext, cw, rows, rs, hb):
    k_taps = cw.shape[0]
    acc = None
    for k in range(k_taps):
        term = cw[k:k + 1, :] * _shift_rows(p_ext, hb + (k - k_taps // 2) * rs, rows)
        acc = term if acc is None else acc + term
    return acc


def _fill_h(h_ref, x_ref, xb_ref, xa_ref, g_ref, sc_ref, sh_ref, first, last, hb, ha):
    tm = x_ref.shape[0]
    g, sc, sh = g_ref[...], sc_ref[...], sh_ref[...]
    h_ref[hb:hb + tm, :] = _norm_mod(x_ref[...], g, sc, sh).astype(BF16)
    if hb:
        hbv = _norm_mod(xb_ref[...], g, sc, sh)
        h_ref[0:hb, :] = jnp.where(first, 0.0, hbv).astype(BF16)
    if ha:
        hav = _norm_mod(xa_ref[...], g, sc, sh)
        h_ref[hb + tm:hb + tm + ha, :] = jnp.where(last, 0.0, hav).astype(BF16)


def _halo_specs(tm, hb, ha, d, n_rows):
    specs = [pl.BlockSpec((tm, d), lambda i, j: (i, 0))]
    if hb:
        specs.append(pl.BlockSpec((hb, d), lambda i, j: (jnp.maximum(i * (tm // hb) - 1, 0), 0)))
    if ha:
        nblk = n_rows // ha
        specs.append(pl.BlockSpec((ha, d), lambda i, j: (jnp.minimum((i + 1) * (tm // ha), nblk - 1), 0)))
    return specs


def _mod_spec(chunk, d, tiles_per_mod):
    return pl.BlockSpec((None, SUBLANES, d), lambda i, j: (i // tiles_per_mod, 0, chunk))


def _mod_kernel(c_ref, w_ref, b_ref, o_ref):
    a = _silu(c_ref[...]).astype(BF16)
    o_ref[...] = _dot(a, w_ref[...].astype(BF16)) + b_ref[...]


def _modulation(c16, mod_w, mod_b):
    depth, d, n = mod_w.shape
    tn = n // 4
    return pl.pallas_call(
        _mod_kernel,
        grid=(depth, n // tn),
        in_specs=[pl.BlockSpec((16, d), lambda l, j: (0, 0)),
                  pl.BlockSpec((None, d, tn), lambda l, j: (l, 0, j)),
                  pl.BlockSpec((None, 1, tn), lambda l, j: (l, 0, j))],
        out_specs=pl.BlockSpec((None, 16, tn), lambda l, j: (l, 0, j)),
        out_shape=jax.ShapeDtypeStruct((depth, 16, n), F32),
        compiler_params=_cparams("arbitrary", "arbitrary"),
        name="modulation",
    )(c16, mod_w, mod_b.reshape(depth, 1, n))


FFN_TILE = 2 * LANES
FFN_DOWN_GROUP = 11


def _ffn_kernel(x_ref, xb_ref, xa_ref, g_ref, sh_ref, sc_ref, gt_ref, win_ref, cw_ref, cb_ref, wo_ref, fg_ref, o_ref, h_ref,
                *, rs, tiles_per_seq, final_norm):
    i = pl.program_id(0)
    tm = x_ref.shape[0]
    hb = ha = SUBLANES
    f = wo_ref.shape[0]
    first = i % tiles_per_seq == 0
    last = i % tiles_per_seq == tiles_per_seq - 1
    _fill_h(h_ref, x_ref, xb_ref, xa_ref, g_ref, sc_ref, sh_ref, first, last, hb, ha)
    acc = None
    nj = f // FFN_TILE
    pending = []
    for j in range(nj):
        cols = slice(j * FFN_TILE, (j + 1) * FFN_TILE)
        gate = _dot(h_ref[...], win_ref[:, cols])
        up = _dot(h_ref[hb:hb + tm, :], win_ref[:, f + j * FFN_TILE:f + (j + 1) * FFN_TILE])
        gc = _conv_rows(gate, cw_ref[:, cols], tm, rs, hb) + cb_ref[:, cols]
        pending.append((_gelu(gc) * up).astype(BF16))
        if len(pending) == FFN_DOWN_GROUP or j == nj - 1:
            k0 = (j + 1 - len(pending)) * FFN_TILE
            part = _dot(jnp.concatenate(pending, axis=1), wo_ref[k0:(j + 1) * FFN_TILE, :])
            acc = part if acc is None else acc + part
            pending = []
    y = x_ref[...] + _tile8(gt_ref[...], tm) * acc
    if final_norm:
        ms = jnp.mean(y * y, axis=-1, keepdims=True)
        y = y * lax.rsqrt(ms + NORM_EPS) * fg_ref[...]
    o_ref[...] = y


def _ffn(x, modt, tiles_per_mod, norm_g, w_in, conv_w, conv_b, w_out, final_g, *, rs, seq_rows, final_norm=False):
    n_rows, d = x.shape
    f = w_out.shape[0]
    tm = _row_tile(seq_rows)
    assert f % FFN_TILE == 0
    hb = ha = SUBLANES
    kern = functools.partial(_ffn_kernel, rs=rs, tiles_per_seq=seq_rows // tm, final_norm=final_norm)
    resident = lambda shape: pl.BlockSpec(shape, lambda i, j: (0, 0), pipeline_mode=pl.Buffered(1))
    in_specs = _halo_specs(tm, hb, ha, d, n_rows) + [
        pl.BlockSpec((1, d), lambda i, j: (0, 0)),
        _mod_spec(3, d, tiles_per_mod), _mod_spec(4, d, tiles_per_mod), _mod_spec(5, d, tiles_per_mod),
        resident(w_in.shape), resident(conv_w.shape), resident((1, f)), resident(w_out.shape),
        pl.BlockSpec((1, d), lambda i, j: (0, 0)),
    ]
    return pl.pallas_call(
        kern,
        grid=(n_rows // tm, 1),
        in_specs=in_specs,
        out_specs=pl.BlockSpec((tm, d), lambda i, j: (i, 0)),
        out_shape=jax.ShapeDtypeStruct((n_rows, d), F32),
        scratch_shapes=[pltpu.VMEM((tm + hb + ha, d), BF16)],
        compiler_params=_cparams("parallel", "arbitrary"),
        name="conv_ffn",
    )(x, x, x, norm_g.reshape(1, d), modt, modt, modt, w_in, conv_w, conv_b.reshape(1, f), w_out, final_g.reshape(1, d))


def _outproj_kernel(*refs, n_pro, prologue, glu):
    pro_refs = refs[:n_pro]
    if glu:
        x_ref, gt_ref, w_ref, b_ref, o_ref = refs[n_pro:]
    else:
        x_ref, gt_ref, w_ref, o_ref = refs[n_pro:]
    d = o_ref.shape[1]
    a = prologue(*pro_refs).astype(BF16)
    y = _dot(a, w_ref[:, 0:d])
    if glu:
        y = (y + b_ref[:, 0:d]) * jax.nn.sigmoid(_dot(a, w_ref[:, d:2 * d]) + b_ref[:, d:2 * d])
    o_ref[...] = x_ref[...] + _tile8(gt_ref[...], o_ref.shape[0]) * y


def _outproj(prologue, pro_args, pro_specs, k_dim, x, modt, tiles_per_mod, w, bias=None, *, tm, name):
    n_rows, d = x.shape
    glu = bias is not None
    kern = functools.partial(_outproj_kernel, n_pro=len(pro_args), prologue=prologue, glu=glu)
    specs = list(pro_specs) + [pl.BlockSpec((tm, d), lambda i, j: (i, 0)),
                               pl.BlockSpec((None, SUBLANES, d), lambda i, j: (i // tiles_per_mod, 0, 2)),
                               pl.BlockSpec(w.shape, lambda i, j: (0, 0))]
    args = list(pro_args) + [x, modt, w]
    if glu:
        specs += [pl.BlockSpec(bias.shape, lambda i, j: (0, 0))]
        args += [bias]
    return pl.pallas_call(
        kern,
        grid=(n_rows // tm, 1),
        in_specs=specs,
        out_specs=pl.BlockSpec((tm, d), lambda i, j: (i, 0)),
        out_shape=jax.ShapeDtypeStruct((n_rows, d), F32),
        compiler_params=_cparams("parallel", "arbitrary"),
        name=name,
    )(*args)


def _row_spec(tm, n):
    return pl.BlockSpec((tm, n), lambda i, j: (i, 0))


def _const_spec(shape):
    nd = len(shape)
    return pl.BlockSpec(shape, lambda i, j: (0,) * nd)


def _s5_weights(lam_re, lam_im, log_step, b_re, b_im, c_re, c_im):
    g, p = lam_re.shape
    gc = b_re.shape[-1]
    lr = jnp.minimum(lam_re.astype(F32), -1e-4)
    li = lam_im.astype(F32)
    step = jnp.exp(log_step.astype(F32))[:, None]
    mag = jnp.exp(lr * step)
    ar, ai = mag * jnp.cos(li * step), mag * jnp.sin(li * step)
    den = lr * lr + li * li
    kr = ((ar - 1.0) * lr + ai * li) / den
    ki = (ai * lr - (ar - 1.0) * li) / den
    br32, bi32 = b_re.astype(F32), b_im.astype(F32)
    br = kr[..., None] * br32 - ki[..., None] * bi32
    bi = kr[..., None] * bi32 + ki[..., None] * br32
    nb = g // S5_GB
    eye = jnp.eye(S5_GB, dtype=F32)

    def pack_b(b):
        b4 = b.reshape(nb, S5_GB, p, gc)
        return jnp.einsum('blpc,lm->blcmp', b4, eye).reshape(nb, S5_GB * gc, S5_GB * p)

    def pack_c(cm):
        c4 = cm.astype(F32).reshape(nb, S5_GB, gc, p)
        return jnp.einsum('blcp,lm->blpmc', c4, eye).reshape(nb, S5_GB * p, S5_GB * gc)

    bm = jnp.concatenate([pack_b(br), pack_b(bi)], axis=-1).astype(BF16)
    cm = jnp.concatenate([pack_c(c_re), -pack_c(c_im)], axis=1).astype(BF16)
    lam = jnp.stack([ar.reshape(nb, S5_GB * p), ai.reshape(nb, S5_GB * p)], axis=1)
    lam = jnp.broadcast_to(lam[:, :, None, :], (nb, 2, SUBLANES, S5_GB * p))
    return bm, cm, lam


def _s5_scan_kernel(x_ref, g_ref, sh_ref, sc_ref, bm_ref, cm_ref, lam_ref, s0_ref, y_ref, sf_ref, st_ref, *bu_refs,
                    rev, emit):
    i = pl.program_id(0)
    tm = x_ref.shape[0]
    nt = tm // SUBLANES
    nb, kin, ns2 = bm_ref.shape
    ns = ns2 // 2

    @pl.when(i == 0)
    def _():
        st_ref[...] = s0_ref[...]
        if not emit:
            y_ref[...] = jnp.zeros_like(y_ref)

    h = _norm_mod(x_ref[...], g_ref[...], sc_ref[...], sh_ref[...]).astype(BF16)
    for gb in range(nb):
        bu_ref = bu_refs[gb]
        bu_ref[...] = _dot(h[:, gb * kin:(gb + 1) * kin], bm_ref[gb])
        ar, ai = lam_ref[gb, 0], lam_ref[gb, 1]
        sr, si = st_ref[gb, 0], st_ref[gb, 1]
        for t in (range(nt - 1, -1, -1) if rev else range(nt)):
            rows = slice(t * SUBLANES, (t + 1) * SUBLANES)
            sr, si = (ar * sr - ai * si + bu_ref[rows, 0:ns], ar * si + ai * sr + bu_ref[rows, ns:ns2])
            if emit:
                bu_ref[rows, 0:ns] = sr
                bu_ref[rows, ns:ns2] = si
        st_ref[gb, 0] = sr
        st_ref[gb, 1] = si
        if emit:
            y_ref[:, gb * kin:(gb + 1) * kin] = _dot(bu_ref[...].astype(BF16), cm_ref[gb])

    @pl.when(i == pl.num_programs(0) - 1)
    def _():
        sf_ref[...] = st_ref[...]


def _s5_scan(x, modt, norm_g, bm, cm, lam, s0, *, rev, emit=True):
    n_rows, d = x.shape
    tm = _row_tile(n_rows, cap=ROW_TILE_CAP // 2)
    ntile = n_rows // tm
    tile = (lambda i: (ntile - 1 - i, 0)) if rev else (lambda i: (i, 0))
    mod = lambda k: pl.BlockSpec((None, SUBLANES, d), lambda i: (0, 0, k))
    full = lambda a: pl.BlockSpec(a.shape, lambda i: (0,) * a.ndim)
    y, sf = pl.pallas_call(
        functools.partial(_s5_scan_kernel, rev=rev, emit=emit),
        grid=(ntile,),
        in_specs=[pl.BlockSpec((tm, d), tile), pl.BlockSpec((1, d), lambda i: (0, 0)), mod(0), mod(1),
                  full(bm), full(cm), full(lam), full(s0)],
        out_specs=[pl.BlockSpec((tm, d), tile if emit else (lambda i: (0, 0))), full(s0)],
        out_shape=[jax.ShapeDtypeStruct((n_rows if emit else tm, d), F32), jax.ShapeDtypeStruct(s0.shape, F32)],
        scratch_shapes=[pltpu.VMEM(s0.shape, F32)] + [pltpu.VMEM((tm, bm.shape[-1]), F32)] * bm.shape[0],
        compiler_params=_cparams("arbitrary"),
        name="s5_scan_bwd" if rev else "s5_scan_fwd",
    )(x, norm_g.reshape(1, d), modt, modt, bm, cm, lam, s0)
    return (y if emit else None), sf


def _s5_out_prologue(x_ref, yf_ref, yb_ref, g_ref, sh_ref, sc_ref, dk_ref):
    u = _norm_mod(x_ref[...], g_ref[...], sc_ref[...], sh_ref[...])
    return _gelu(yf_ref[...] + yb_ref[...] + dk_ref[...] * u)


def _s5_mixer(x, x_ctx, modt, modt_ctx, norm_g, prm, *, ctx_out):
    d = x.shape[-1]
    w = [_s5_weights(prm['lam_re'][dr], prm['lam_im'][dr], prm['log_step'][dr], prm['b_re'][dr], prm['b_im'][dr],
                     prm['c_re'][dr], prm['c_im'][dr]) for dr in range(2)]
    zero = jnp.zeros(w[0][2].shape, F32)
    ys, ys_ctx = [], []
    for dr in range(2):
        bm, cm, lam = w[dr]
        yc, sc = _s5_scan(x_ctx, modt_ctx, norm_g, bm, cm, lam, zero, rev=dr == 1, emit=ctx_out)
        yl, _ = _s5_scan(x, modt, norm_g, bm, cm, lam, sc, rev=dr == 1)
        ys.append(yl)
        ys_ctx.append(yc)
    w_glu = prm['w_glu'].astype(BF16)
    b_glu = prm['b_glu'].reshape(1, -1)

    def out(xa, ya, mt):
        tm = _row_tile(xa.shape[0], cap=ROW_TILE_CAP // 2)
        pro_args = [xa, ya[0], ya[1], norm_g.reshape(1, d), mt, mt, prm['d'].reshape(1, d)]
        pro_specs = [_row_spec(tm, d), _row_spec(tm, d), _row_spec(tm, d), _const_spec((1, d)),
                     _mod_spec(0, d, 10 ** 9), _mod_spec(1, d, 10 ** 9), _const_spec((1, d))]
        return _outproj(_s5_out_prologue, pro_args, pro_specs, d, xa, mt, 10 ** 9, w_glu, b_glu, tm=tm, name="s5_out")

    return out(x, ys, modt), (out(x_ctx, ys_ctx, modt_ctx) if ctx_out else None)


def _lru_in_kernel(x_ref, xb_ref, xa_ref, g_ref, sh_ref, sc_ref, wr_ref, wy_ref, cw_ref, cb_ref, xc_ref, yg_ref, h_ref,
                   *, rs, hb, ha):
    i, j = pl.program_id(0), pl.program_id(1)
    tm = x_ref.shape[0]

    @pl.when(j == 0)
    def _():
        _fill_h(h_ref, x_ref, xb_ref, xa_ref, g_ref, sc_ref, sh_ref, i == 0, i == pl.num_programs(0) - 1, hb, ha)

    p = _dot(h_ref[...], wr_ref[...])
    xc_ref[...] = _conv_rows(p, cw_ref[...], tm, rs, hb) + cb_ref[...]
    yg_ref[...] = _gelu(_dot(h_ref[hb:hb + tm, :], wy_ref[...]))


def _lru_in(x, modt, norm_g, w_in, conv_w, conv_b):
    n_rows, d = x.shape
    wd = conv_w.shape[-1]
    rs = SUBLANES
    hb, ha = 2 * rs, rs
    tm = _row_tile(n_rows)
    tn = 4 * LANES
    nj = wd // tn
    specs = _halo_specs(tm, hb, ha, d, n_rows) + [
        _const_spec((1, d)), _mod_spec(0, d, 10 ** 9), _mod_spec(1, d, 10 ** 9),
        pl.BlockSpec((d, tn), lambda i, j: (0, j)), pl.BlockSpec((d, tn), lambda i, j: (0, j + nj)),
        pl.BlockSpec((conv_w.shape[0], tn), lambda i, j: (0, j)), pl.BlockSpec((1, tn), lambda i, j: (0, j))]
    out_spec = pl.BlockSpec((tm, tn), lambda i, j: (i, j))
    return pl.pallas_call(
        functools.partial(_lru_in_kernel, rs=rs, hb=hb, ha=ha),
        grid=(n_rows // tm, nj),
        in_specs=specs,
        out_specs=[out_spec, out_spec],
        out_shape=[jax.ShapeDtypeStruct((n_rows, wd), F32)] * 2,
        scratch_shapes=[pltpu.VMEM((tm + hb + ha, d), BF16)],
        compiler_params=_cparams("parallel", "arbitrary"),
        name="lru_in",
    )(x, x, x, norm_g.reshape(1, d), modt, modt, w_in, w_in, conv_w, conv_b.reshape(1, wd))


def _lru_scan_kernel(xc_ref, wa_ref, wx_ref, ba_ref, bx_ref, nsp_ref, h0_ref, hs_ref, hf_ref, st_ref, a_ref, b_ref,
                     *, rev, emit):
    i = pl.program_id(0)
    tm = xc_ref.shape[0]
    nt = tm // SUBLANES
    nblk, bw, _ = wa_ref.shape

    @pl.when(i == 0)
    def _():
        st_ref[...] = h0_ref[...]
        if not emit:
            hs_ref[...] = jnp.zeros_like(hs_ref)

    for k in range(nblk):
        cols = slice(k * bw, (k + 1) * bw)
        xc = xc_ref[:, cols]
        xcb = xc.astype(BF16)
        r = jax.nn.sigmoid(_dot(xcb, wa_ref[k]) + ba_ref[:, cols])
        gi = jax.nn.sigmoid(_dot(xcb, wx_ref[k]) + bx_ref[:, cols])
        log_a = nsp_ref[:, cols] * r
        a = jnp.exp(log_a)
        a_ref[:, cols] = a
        b_ref[:, cols] = jnp.sqrt(1.0 - a * a) * (gi * xc)

    def step(t, h):
        tt = nt - 1 - t if rev else t
        r0 = pl.multiple_of(tt * SUBLANES, SUBLANES)
        h = a_ref[pl.ds(r0, SUBLANES), :] * h + b_ref[pl.ds(r0, SUBLANES), :]
        if emit:
            hs_ref[pl.ds(r0, SUBLANES), :] = h
        return h

    st_ref[...] = lax.fori_loop(0, nt, step, st_ref[...], unroll=8)

    @pl.when(i == pl.num_programs(0) - 1)
    def _():
        hf_ref[...] = st_ref[...]


def _lru_scan(xc, wa, wx, ba, bx, nsp, h0, *, rev, emit=True):
    n_rows, wd = xc.shape
    tm = _row_tile(n_rows)
    ntile = n_rows // tm
    tile = (lambda i: (ntile - 1 - i, 0)) if rev else (lambda i: (i, 0))
    full = lambda a: pl.BlockSpec(a.shape, lambda i: (0,) * a.ndim)
    hs, hf = pl.pallas_call(
        functools.partial(_lru_scan_kernel, rev=rev, emit=emit),
        grid=(ntile,),
        in_specs=[pl.BlockSpec((tm, wd), tile), full(wa), full(wx), full(ba), full(bx), full(nsp), full(h0)],
        out_specs=[pl.BlockSpec((tm, wd), tile if emit else (lambda i: (0, 0))), full(h0)],
        out_shape=[jax.ShapeDtypeStruct((n_rows if emit else tm, wd), F32), jax.ShapeDtypeStruct(h0.shape, F32)],
        scratch_shapes=[pltpu.VMEM(h0.shape, F32), pltpu.VMEM((tm, wd), F32), pltpu.VMEM((tm, wd), F32)],
        compiler_params=_cparams("arbitrary"),
        name="lru_scan_bwd" if rev else "lru_scan_fwd",
    )(xc, wa, wx, ba, bx, nsp, h0)
    return (hs if emit else None), hf


def _lru_out_prologue(yg_ref, hf_ref, hb_ref):
    return yg_ref[...] * (hf_ref[...] + hb_ref[...])


def _lru_mixer(x, x_ctx, modt, modt_ctx, norm_g, prm, *, ctx_out):
    d = x.shape[-1]
    w_in = prm['w_in'].astype(BF16)
    wd = prm['conv_w'].shape[-1]
    xc_l, yg_l = _lru_in(x, modt, norm_g, w_in, prm['conv_w'], prm['conv_b'])
    xc_c, yg_c = _lru_in(x_ctx, modt_ctx, norm_g, w_in, prm['conv_w'], prm['conv_b'])
    zero = jnp.zeros((SUBLANES, wd), F32)
    hs_l, hs_c = [], []
    for dr in range(2):
        wa, wx = prm['w_a'][dr].astype(BF16), prm['w_x'][dr].astype(BF16)
        ba, bx = prm['b_a'][dr].reshape(1, wd), prm['b_x'][dr].reshape(1, wd)
        nsp = (-LRU_C * jax.nn.softplus(-prm['lam'][dr].astype(F32))).reshape(1, wd)
        hc, hfin = _lru_scan(xc_c, wa, wx, ba, bx, nsp, zero, rev=dr == 1, emit=ctx_out)
        hl, _ = _lru_scan(xc_l, wa, wx, ba, bx, nsp, hfin, rev=dr == 1)
        hs_l.append(hl)
        hs_c.append(hc)
    w_out = prm['w_out'].astype(BF16)

    def out(xa, yg, hs, mt):
        tm = _row_tile(xa.shape[0], cap=ROW_TILE_CAP // 2)
        specs = [_row_spec(tm, wd)] * 3
        return _outproj(_lru_out_prologue, [yg, hs[0], hs[1]], specs, wd, xa, mt, 10 ** 9, w_out, tm=tm, name="lru_out")

    return out(x, yg_l, hs_l, modt), (out(x_ctx, yg_c, hs_c, modt_ctx) if ctx_out else None)


def _rope_tables(length):
    rows = length // GRID_W
    t = jnp.arange(length, dtype=jnp.int32)
    row = (t // GRID_W).astype(F32) - (rows - 1) / 2.0
    col = (t % GRID_W).astype(F32) - (GRID_W - 1) / 2.0
    quarter = RET_DK // 4
    inv_freq = ROPE_BASE ** (-jnp.arange(quarter, dtype=F32) / quarter)
    ar, ac = row[:, None] * inv_freq[None, :], col[:, None] * inv_freq[None, :]
    cos = jnp.concatenate([jnp.cos(ar), jnp.cos(ar), jnp.cos(ac), jnp.cos(ac)], axis=-1)
    sin = jnp.concatenate([-jnp.sin(ar), jnp.sin(ar), -jnp.sin(ac), jnp.sin(ac)], axis=-1)
    return cos, sin


def _ret_in_kernel(x_ref, g_ref, sh_ref, sc_ref, w_ref, cos_ref, sin_ref, p_ref, h_ref, *, rotate, n_k, n_v):
    j = pl.program_id(1)
    tn = w_ref.shape[1]

    @pl.when(j == 0)
    def _():
        h_ref[...] = _norm_mod(x_ref[...], g_ref[...], sc_ref[...], sh_ref[...]).astype(BF16)

    is_k = j < n_k
    is_q = (j >= n_k + n_v) & (j < 2 * n_k + n_v)

    @pl.when(is_k | is_q)
    def _():
        scale = jnp.where(is_q, RET_DK ** -0.5, 1.0)
        for hd in range(tn // RET_DK):
            ph = _dot(h_ref[...], w_ref[:, hd * RET_DK:(hd + 1) * RET_DK])
            for s in range(RET_DK // LANES):
                tcols = slice(s * LANES, (s + 1) * LANES)
                ps = ph[:, tcols]
                if rotate:
                    rot = pltpu.roll(ps, LANES // 2, axis=1)
                    ps = ps * cos_ref[:, tcols] + rot * sin_ref[:, tcols]
                p_ref[:, hd * RET_DK + s * LANES:hd * RET_DK + (s + 1) * LANES] = (ps * scale).astype(p_ref.dtype)

    @pl.when(jnp.logical_not(is_k | is_q))
    def _():
        p_ref[...] = _dot(h_ref[...], w_ref[...]).astype(p_ref.dtype)


def _ret_in(x, modt, tiles_per_mod, norm_g, w_in, seq_rows, *, rotate):
    n_rows, d = x.shape
    n_out = w_in.shape[1]
    tm = _row_tile(seq_rows)
    tn = RET_HEADS * RET_DK
    tiles_per_seq = seq_rows // tm
    cos, sin = _rope_tables(seq_rows)
    hk, hv = RET_HEADS * RET_DK, RET_HEADS * RET_DV
    tab_spec = pl.BlockSpec((tm, RET_DK), lambda i, j: (i % tiles_per_seq, 0))
    return pl.pallas_call(
        functools.partial(_ret_in_kernel, rotate=rotate, n_k=hk // tn, n_v=hv // tn),
        grid=(n_rows // tm, n_out // tn),
        in_specs=[_row_spec(tm, d), _const_spec((1, d)), _mod_spec(0, d, tiles_per_mod), _mod_spec(1, d, tiles_per_mod),
                  pl.BlockSpec((d, tn), lambda i, j: (0, j)), tab_spec, tab_spec],
        out_specs=pl.BlockSpec((tm, tn), lambda i, j: (i, j)),
        out_shape=jax.ShapeDtypeStruct((n_rows, n_out), BF16),
        scratch_shapes=[pltpu.VMEM((tm, d), BF16)],
        compiler_params=_cparams("parallel", "arbitrary"),
        name="ret_in",
    )(x, norm_g.reshape(1, d), modt, modt, w_in, cos, sin)


def _ret_tables(c):
    log_g = np.log1p(-np.power(2.0, -5.0 - np.arange(RET_HEADS, dtype=np.float64)))
    idx = np.arange(c, dtype=np.float64)
    diff = idx[:, None] - idx[None, :]
    fwd = np.where(diff >= 0, np.exp(np.where(diff >= 0, diff, 0.0)[None] * log_g[:, None, None]), 0.0)
    bwd = np.where(diff < 0, np.exp(np.where(diff < 0, -diff, 0.0)[None] * log_g[:, None, None]), 0.0)
    xi_f = np.exp((idx + 1.0)[None, :] * log_g[:, None])
    zeta_f = np.exp((c - 1.0 - idx)[None, :] * log_g[:, None])
    xi_b = np.exp((c - idx)[None, :] * log_g[:, None])
    zeta_b = np.exp(idx[None, :] * log_g[:, None])
    dmask = np.stack([fwd, bwd]).astype(np.float32)
    xi = np.stack([xi_f, xi_b])[..., None].astype(np.float32)
    zeta = np.stack([zeta_f, zeta_b])[..., None].astype(np.float32)
    g_blk = [float(v) for v in np.exp(c * log_g).astype(np.float32)]
    return jnp.asarray(dmask), jnp.asarray(xi), jnp.asarray(zeta), g_blk


def _ret_scan_kernel(kf_ref, v0f_ref, v1f_ref, qf_ref, kb_ref, v0b_ref, v1b_ref, qb_ref, dm_ref, xi_ref, zt_ref, r0_ref,
                     of_ref, ob_ref, rf_ref, r_ref, *, g_blk, emit):
    c = pl.program_id(1)
    dir_refs = ((kf_ref, v0f_ref, v1f_ref, qf_ref, of_ref), (kb_ref, v0b_ref, v1b_ref, qb_ref, ob_ref))

    @pl.when(c == 0)
    def _():
        r_ref[...] = r0_ref[...]
        if not emit:
            of_ref[...] = jnp.zeros_like(of_ref)
            ob_ref[...] = jnp.zeros_like(ob_ref)

    hv_half = v0f_ref.shape[1] // RET_DV
    r_olds = [[r_ref[d, h] for h in range(RET_HEADS)] for d in range(2)]
    r_news = [[None] * RET_HEADS for _ in range(2)]
    for h in range(RET_HEADS):
        for d in range(2):
            k_ref, v0_ref, v1_ref, q_ref, o_ref = dir_refs[d]
            kh = k_ref[:, h * RET_DK:(h + 1) * RET_DK]
            v_ref = v0_ref if h < hv_half else v1_ref
            hh = h % hv_half
            vh = v_ref[:, hh * RET_DV:(hh + 1) * RET_DV]
            r_old = r_olds[d][h]
            if emit:
                qh = q_ref[:, h * RET_DK:(h + 1) * RET_DK]
                s = _dot_nt(qh, kh) * dm_ref[d, h]
                o = _dot(s.astype(BF16), vh) + _dot(qh, r_old.astype(BF16)) * xi_ref[d, h]
                o_ref[:, h * RET_DV:(h + 1) * RET_DV] = o.astype(o_ref.dtype)
            r_news[d][h] = g_blk[h] * r_old + _dot_tn((kh * zt_ref[d, h]).astype(BF16), vh)
    for d in range(2):
        for h in range(RET_HEADS):
            r_ref[d, h] = r_news[d][h]

    @pl.when(c == pl.num_programs(1) - 1)
    def _():
        rf_ref[...] = r_ref[...]


def _ret_scan(p, r0, batch, seq_rows, *, emit=True):
    c = min(RET_CHUNK, seq_rows)
    nc = seq_rows // c
    hk, hv = RET_HEADS * RET_DK, RET_HEADS * RET_DV
    dmask, xi, zeta, g_blk = _ret_tables(c)
    n_rows = p.shape[0]
    rows = (lambda b, cc: b * nc + cc, lambda b, cc: b * nc + nc - 1 - cc)
    kcols = lambda rw: [pl.BlockSpec((c, hk), lambda b, cc, blk=blk: (rw(b, cc), blk)) for blk in range(4)]
    full = lambda a: pl.BlockSpec(a.shape, lambda b, cc: (0,) * a.ndim)
    st_spec = pl.BlockSpec((2, None, RET_HEADS, RET_DK, RET_DV), lambda b, cc: (0, b, 0, 0, 0))
    o_spec = lambda rw: pl.BlockSpec((c, hv), (lambda b, cc: (rw(b, cc), 0)) if emit else (lambda b, cc: (0, 0)))
    o_shape = jax.ShapeDtypeStruct((n_rows if emit else c, hv), BF16)
    assert hv == 2 * hk
    of, ob, rf = pl.pallas_call(
        functools.partial(_ret_scan_kernel, g_blk=g_blk, emit=emit),
        grid=(batch, nc),
        in_specs=kcols(rows[0]) + kcols(rows[1]) + [full(dmask), full(xi), full(zeta), st_spec],
        out_specs=[o_spec(rows[0]), o_spec(rows[1]), st_spec],
        out_shape=[o_shape, o_shape, jax.ShapeDtypeStruct(r0.shape, F32)],
        scratch_shapes=[pltpu.VMEM((2, RET_HEADS, RET_DK, RET_DV), F32)],
        compiler_params=_cparams("arbitrary", "arbitrary"),
        name="ret_scan",
    )(*([p] * 8), dmask, xi, zeta, r0)
    return of, ob, rf


def _ret_out_prologue(of_ref, ob_ref, gate_ref, ng_ref):
    o = of_ref[...].astype(F32) + ob_ref[...].astype(F32)
    parts = []
    for h in range(RET_HEADS):
        oh = o[:, h * RET_DV:(h + 1) * RET_DV]
        parts.append(oh * lax.rsqrt(jnp.mean(oh * oh, axis=-1, keepdims=True) + NORM_EPS))
    on = (jnp.concatenate(parts, axis=-1) * ng_ref[...])
    return jax.nn.silu(gate_ref[...].astype(F32)) * on


def _ret_mixer(x, x_ctx, modt, modt_ctx, norm_g, prm, batch, seq, seq_ctx, *, ctx_out):
    d = x.shape[-1]
    w_in = prm['w_in'].astype(BF16)
    hk, hv = RET_HEADS * RET_DK, RET_HEADS * RET_DV
    tiles_per_mod = seq // _row_tile(seq)
    p_l = _ret_in(x, modt, tiles_per_mod, norm_g, w_in, seq, rotate=True)
    p_c = _ret_in(x_ctx, modt_ctx, 10 ** 9, norm_g, w_in, seq_ctx, rotate=False)
    r0 = jnp.zeros((2, batch, RET_HEADS, RET_DK, RET_DV), F32)
    of_c, ob_c, r_c = _ret_scan(p_c, r0, batch, seq_ctx, emit=ctx_out)
    of_l, ob_l, _ = _ret_scan(p_l, r_c, batch, seq)
    w_out = prm['w_out'].astype(BF16)
    ng = prm['norm_g'].reshape(1, hv)

    def out(xa, pa, of, ob, mt, tpm, seq_rows):
        tm = _row_tile(seq_rows, cap=ROW_TILE_CAP // 2)
        gcol = (2 * hk + hv) // hv
        specs = [_row_spec(tm, hv), _row_spec(tm, hv), pl.BlockSpec((tm, hv), lambda i, j: (i, gcol)), _const_spec((1, hv))]
        return _outproj(_ret_out_prologue, [of, ob, pa, ng], specs, hv, xa, mt, tpm * (_row_tile(seq_rows) // tm), w_out,
                        tm=tm, name="ret_out")

    y_l = out(x, p_l, of_l, ob_l, modt, tiles_per_mod, seq)
    y_c = out(x_ctx, p_c, of_c, ob_c, modt_ctx, 10 ** 9, seq_ctx) if ctx_out else None
    return y_l, y_c


GDN_IN_TILE = 8 * LANES


def _gdn_in_kernel(x_ref, xb_ref, xa_ref, g_ref, sh_ref, sc_ref, w_ref, cw_ref, wba_ref, o_ref, ba_ref, h_ref,
                   *, tiles_per_seq, n_kq, n_conv):
    i, j = pl.program_id(0), pl.program_id(1)
    tm = x_ref.shape[0]
    hb = ha = SUBLANES
    heads = o_ref.shape[0]

    @pl.when(j == 0)
    def _():
        first = i % tiles_per_seq == 0
        last = i % tiles_per_seq == tiles_per_seq - 1
        _fill_h(h_ref, x_ref, xb_ref, xa_ref, g_ref, sc_ref, sh_ref, first, last, hb, ha)

    pair = 2 * LANES

    def conv_silu(s2):
        p = _dot(h_ref[...], w_ref[:, s2 * pair:(s2 + 1) * pair])
        return jax.nn.silu(_conv_rows(p, cw_ref[:, s2 * pair:(s2 + 1) * pair], tm, 1, hb))

    @pl.when(j < n_kq)
    def _():
        scale = jnp.where(j >= n_kq // 2, GDN_DK ** -0.5, 1.0)
        for s2 in range(heads // 2):
            a = conv_silu(s2)
            for e in range(2):
                ah = a[:, e * LANES:(e + 1) * LANES]
                o_ref[2 * s2 + e] = (ah * (lax.rsqrt(jnp.sum(ah * ah, axis=-1, keepdims=True) + 1e-6) * scale)
                                     ).astype(o_ref.dtype)

    @pl.when((j >= n_kq) & (j < n_conv))
    def _():
        for s2 in range(heads // 2):
            a = conv_silu(s2)
            for e in range(2):
                o_ref[2 * s2 + e] = a[:, e * LANES:(e + 1) * LANES].astype(o_ref.dtype)

    @pl.when(j >= n_conv)
    def _():
        for s2 in range(heads // 2):
            p = _dot(h_ref[hb:hb + tm, :], w_ref[:, s2 * pair:(s2 + 1) * pair])
            for e in range(2):
                o_ref[2 * s2 + e] = p[:, e * LANES:(e + 1) * LANES].astype(o_ref.dtype)

    @pl.when(j == pl.num_programs(1) - 1)
    def _():
        ba_ref[...] = _dot(h_ref[hb:hb + tm, :], wba_ref[...])


def _gdn_in(x, modt, tiles_per_mod, norm_g, w_all, cw_all, w_ba, seq_rows):
    n_rows, d = x.shape
    n_out = w_all.shape[1]
    tm = _row_tile(seq_rows)
    tn = GDN_IN_TILE
    hb = ha = SUBLANES
    heads = tn // LANES
    n_kq = 2 * GDN_K_HEADS * GDN_DK // tn
    n_conv = n_kq + GDN_V_HEADS * GDN_DV // tn
    kern = functools.partial(_gdn_in_kernel, tiles_per_seq=seq_rows // tm, n_kq=n_kq, n_conv=n_conv)
    specs = _halo_specs(tm, hb, ha, d, n_rows) + [
        _const_spec((1, d)), _mod_spec(0, d, tiles_per_mod), _mod_spec(1, d, tiles_per_mod),
        pl.BlockSpec((d, tn), lambda i, j: (0, j)), pl.BlockSpec((cw_all.shape[0], tn), lambda i, j: (0, j)),
        _const_spec(w_ba.shape)]
    return pl.pallas_call(
        kern,
        grid=(n_rows // tm, n_out // tn),
        in_specs=specs,
        out_specs=[pl.BlockSpec((heads, tm, LANES), lambda i, j: (j, i, 0)), pl.BlockSpec((tm, LANES), lambda i, j: (i, 0))],
        out_shape=[jax.ShapeDtypeStruct((n_out // LANES, n_rows, LANES), BF16), jax.ShapeDtypeStruct((n_rows, LANES), F32)],
        scratch_shapes=[pltpu.VMEM((tm + hb + ha, d), BF16)],
        compiler_params=_cparams("parallel", "arbitrary"),
        name="gdn_in",
    )(x, x, x, norm_g.reshape(1, d), modt, modt, w_all, cw_all, w_ba)


def _split3(a):
    hi = a.astype(BF16)
    lo = (a - hi.astype(F32)).astype(BF16)
    return hi, lo


def _dot3s(a, b):
    ah, al = a
    bh, bl = b
    m = ah.shape[0]
    both = _dot(jnp.concatenate([ah, al], axis=0), bh)
    return both[:m] + (both[m:] + _dot(ah, bl))


def _cumsum_rows(x, rev):
    n = x.shape[0]
    row = lax.broadcasted_iota(jnp.int32, x.shape, 0)
    shift = 1
    while shift < n:
        if rev:
            x = x + jnp.where(row < n - shift, pltpu.roll(x, n - shift, axis=0), 0.0)
        else:
            x = x + jnp.where(row >= shift, pltpu.roll(x, shift, axis=0), 0.0)
        shift *= 2
    return x


GDN_GROUP = 4


def _gdn_scan_kernel(kf_ref, qf_ref, vf_ref, baf_ref, kb_ref, qb_ref, vb_ref, bab_ref, an_ref, dtb_ref,
                     s0_ref, of_ref, ob_ref, sf_ref, s_ref, *, emit):
    cidx = pl.program_id(1)
    c = kf_ref.shape[1]
    rep = GDN_V_HEADS // GDN_K_HEADS
    dir_refs = ((kf_ref, qf_ref, vf_ref, baf_ref, of_ref), (kb_ref, qb_ref, vb_ref, bab_ref, ob_ref))

    @pl.when(cidx == 0)
    def _():
        s_ref[...] = s0_ref[...]
        if not emit:
            of_ref[...] = jnp.zeros_like(of_ref)
            ob_ref[...] = jnp.zeros_like(ob_ref)

    def expand(src, first_lane, g):
        cols = [jnp.broadcast_to(src[:, first_lane + gs * g + h:first_lane + gs * g + h + 1], (c, GDN_DV)) for h in range(gs)]
        per = LANES // c
        narrow = []
        for p in range(gs // per):
            blk = cols[per * p + per - 1]
            for h in range(per - 2, -1, -1):
                blk = jnp.where(lane_in_vreg < (h + 1) * c, cols[per * p + h], blk)
            narrow.append(blk)
        return jnp.concatenate(cols, axis=1), jnp.concatenate(narrow, axis=1)

    gs = GDN_GROUP
    lane_in_vreg = lax.broadcasted_iota(jnp.int32, (c, LANES), 1)
    wide = gs * c
    ii = lax.broadcasted_iota(jnp.int32, (c, wide), 0)
    jj = lax.broadcasted_iota(jnp.int32, (c, wide), 1) % c
    br = lax.broadcasted_iota(jnp.int32, (wide, wide), 0) // c
    bc = lax.broadcasted_iota(jnp.int32, (wide, wide), 1) // c
    blockmask = (br == bc).astype(F32).astype(BF16)
    nk, nv = GDN_K_HEADS, GDN_V_HEADS
    zc = jnp.zeros((c, GDN_DK), BF16)
    zs = jnp.zeros((GDN_DK, GDN_DV), BF16)
    zr = jnp.zeros((c, GDN_DV), BF16)

    gcs, bts, incl, strict, eye, last = [], [], [], [], [], []
    for d in range(2):
        rev = d == 1
        ba = dir_refs[d][3][...]
        z = ba + dtb_ref[d]
        sp = jnp.maximum(z, 0.0) + jnp.log(1.0 + jnp.exp(-jnp.abs(z)))
        gcs.append(_cumsum_rows(an_ref[d] * sp, rev))
        bts.append(jax.nn.sigmoid(ba))
        diff = (jj - ii) if rev else (ii - jj)
        incl.append(diff >= 0)
        strict.append(diff > 0)
        eye.append((diff == 0).astype(F32))
        last.append(0 if rev else c - 1)

    def bdiag(blocks, zero):
        n = len(blocks)
        return jnp.concatenate([jnp.concatenate([blocks[r] if r == q else zero for q in range(n)], axis=1)
                                for r in range(n)], axis=0)

    def bd_tile(a16):
        return jnp.concatenate([a16] * gs, axis=0) * blockmask

    def dot3_bd(lhs, rhs_hi, rhs_lo):
        lh, ll = _split3(lhs)
        m = lhs.shape[0]
        both = _dot(jnp.concatenate([lh, ll], axis=0), rhs_hi)
        return both[:m] + (both[m:] + _dot(lh, rhs_lo))

    pieces = lambda a: [a[:, h * GDN_DV:(h + 1) * GDN_DV] for h in range(gs)]
    chains = [(d, g) for g in range(nv // gs) for d in range(2)]
    ks = [[dir_refs[d][0][h].astype(F32) for h in range(nk)] for d in range(2)]
    qs = [[dir_refs[d][1][h].astype(F32) for h in range(nk)] for d in range(2)] if emit else None
    vs = [[dir_refs[d][2][h].astype(F32) for h in range(nv)] for d in range(2)]
    s_olds = [[s_ref[d, h] for h in range(nv)] for d in range(2)]

    xs, ts, attns, gc5s, bt5s = [], [], [], [], []
    for d, g in chains:
        k0, k1 = ks[d][2 * g], ks[d][2 * g + 1]
        k016, k116 = k0.astype(BF16), k1.astype(BF16)
        rhs_t = jnp.concatenate([jnp.concatenate([k016, zc], axis=1)] * rep
                                + [jnp.concatenate([zc, k116], axis=1)] * rep, axis=0)
        if emit:
            lhs = jnp.concatenate([jnp.concatenate([k0, qs[d][2 * g]], axis=0),
                                   jnp.concatenate([k1, qs[d][2 * g + 1]], axis=0)], axis=1).astype(BF16)
        else:
            lhs = jnp.concatenate([k016, k116], axis=1)
        gram = _dot_nt(lhs, rhs_t)
        gc5, gcm = expand(gcs[d], d * 2 * nv + nv, g)
        bt5, btm = expand(bts[d], d * 2 * nv, g)
        gc5s.append(gc5)
        bt5s.append(bt5)
        grow = jnp.sum(gcm * eye[d], axis=0, keepdims=True)
        dec = jnp.where(incl[d], jnp.exp(jnp.where(incl[d], gcm - grow, 0.0)), 0.0)
        x = jnp.where(strict[d], -(gram[:c] * btm) * dec, 0.0)
        xs.append(x)
        ts.append(eye[d] + x)
        attns.append(gram[c:] * dec if emit else None)

    splits = [_split3(x) for x in xs]
    xs = [dot3_bd(x, bd_tile(xh), bd_tile(xl)) for x, (xh, xl) in zip(xs, splits)]
    nlev = int(math.log2(c)) - 1
    for lvl in range(1, nlev + 1):
        splits = [_split3(x) for x in xs]
        if lvl < nlev:
            ps = [dot3_bd(jnp.concatenate([t, x], axis=0), bd_tile(xh), bd_tile(xl))
                  for t, x, (xh, xl) in zip(ts, xs, splits)]
            ts = [t + p[:c] for t, p in zip(ts, ps)]
            xs = [p[c:] for p in ps]
        else:
            ts = [t + dot3_bd(t, bd_tile(xh), bd_tile(xl)) for t, (xh, xl) in zip(ts, splits)]

    kcats, p1s = [], []
    for n, (d, g) in enumerate(chains):
        k0, k1 = ks[d][2 * g], ks[d][2 * g + 1]
        gc5, bt5 = gc5s[n], bt5s[n]
        eg5 = jnp.exp(gc5)
        kcat = jnp.concatenate([k0] * rep + [k1] * rep, axis=1)
        kbe = kcat * bt5 * eg5
        s16 = [s_olds[d][gs * g + h].astype(BF16) for h in range(gs)]
        if emit:
            qe = jnp.concatenate([qs[d][2 * g]] * rep + [qs[d][2 * g + 1]] * rep, axis=1) * eg5
            top = jnp.concatenate([kbe, qe], axis=0).astype(BF16)
        else:
            top = kbe.astype(BF16)
        p1s.append(jnp.concatenate([_dot(top[:, 2 * pr * GDN_DK:(2 * pr + 2) * GDN_DK], bdiag(s16[2 * pr:2 * pr + 2], zs))
                                    for pr in range(gs // 2)], axis=1))
        kcats.append(kcat)

    bdvs = []
    for n, (d, g) in enumerate(chains):
        vcat = jnp.concatenate(vs[d][gs * g:gs * (g + 1)], axis=1)
        rh, rl = _split3(vcat * bt5s[n] - p1s[n][:c])
        vn16 = dot3_bd(ts[n], bdiag(pieces(rh), zr), bdiag(pieces(rl), zr)).astype(BF16)
        bdvs.append(bdiag(pieces(vn16), zr))

    outs = [[None] * nv for _ in range(2)]
    s_news = [[None] * nv for _ in range(2)]
    for n, (d, g) in enumerate(chains):
        gl5 = gc5s[n][last[d]:last[d] + 1, :]
        if emit:
            o = pieces(p1s[n][c:] + _dot(attns[n].astype(BF16), bdvs[n]))
        kd = (kcats[n] * jnp.exp(gl5 - gc5s[n])).astype(BF16)
        sn = _dot_tn(jnp.concatenate(pieces(kd), axis=0), bdvs[n])
        egl = jnp.exp(gl5)
        for h in range(gs):
            cols = slice(h * GDN_DV, (h + 1) * GDN_DV)
            s_news[d][gs * g + h] = s_olds[d][gs * g + h] * egl[:, cols] + sn[:, cols]
            if emit:
                outs[d][gs * g + h] = o[h]
    for d in range(2):
        for hv in range(nv):
            if emit:
                dir_refs[d][4][hv] = outs[d][hv].astype(dir_refs[d][4].dtype)
            s_ref[d, hv] = s_news[d][hv]

    @pl.when(cidx == pl.num_programs(1) - 1)
    def _():
        sf_ref[...] = s_ref[...]


def _gdn_scan(kqvz, ba, an, dtb, s0, batch, seq_rows, *, emit=True):
    c = min(GDN_CHUNK, seq_rows)
    nc = seq_rows // c
    n_rows = kqvz.shape[1]
    rows = (lambda b, cc: b * nc + cc, lambda b, cc: b * nc + nc - 1 - cc)
    kh, vh = GDN_K_HEADS, GDN_V_HEADS
    st_spec = pl.BlockSpec((2, None, vh, GDN_DK, GDN_DV), lambda b, cc: (0, b, 0, 0, 0))
    full = lambda a: pl.BlockSpec(a.shape, lambda b, cc: (0,) * a.ndim)

    def dir_specs(rw):
        return [pl.BlockSpec((kh, c, LANES), lambda b, cc: (0, rw(b, cc), 0)),
                pl.BlockSpec((kh, c, LANES), lambda b, cc: (1, rw(b, cc), 0)),
                pl.BlockSpec((vh, c, LANES), lambda b, cc: (1, rw(b, cc), 0)),
                pl.BlockSpec((c, LANES), lambda b, cc: (rw(b, cc), 0))]

    def o_spec(rw):
        return pl.BlockSpec((vh, c, LANES), (lambda b, cc: (0, rw(b, cc), 0)) if emit else (lambda b, cc: (0, 0, 0)))

    o_shape = jax.ShapeDtypeStruct((vh, n_rows if emit else c, LANES), BF16)
    of, ob, sf = pl.pallas_call(
        functools.partial(_gdn_scan_kernel, emit=emit),
        grid=(batch, nc),
        in_specs=dir_specs(rows[0]) + dir_specs(rows[1]) + [full(an), full(dtb), st_spec],
        out_specs=[o_spec(rows[0]), o_spec(rows[1]), st_spec],
        out_shape=[o_shape, o_shape, jax.ShapeDtypeStruct(s0.shape, F32)],
        scratch_shapes=[pltpu.VMEM((2, vh, GDN_DK, GDN_DV), F32)],
        compiler_params=_cparams("arbitrary", "arbitrary"),
        name="gdn_scan",
    )(kqvz, kqvz, kqvz, ba, kqvz, kqvz, kqvz, ba, an, dtb, s0)
    return of, ob, sf


def _gdn_out_prologue(of_ref, ob_ref, z_ref, ng_ref):
    parts = []
    for h in range(GDN_V_HEADS):
        o = of_ref[h].astype(F32) + ob_ref[h].astype(F32)
        on = o * lax.rsqrt(jnp.mean(o * o, axis=-1, keepdims=True) + NORM_EPS) * ng_ref[...]
        parts.append(on * jax.nn.silu(z_ref[h].astype(F32)))
    return jnp.concatenate(parts, axis=-1)


def _gdn_mixer(x, x_ctx, modt, modt_ctx, norm_g, prm, batch, seq, seq_ctx, *, ctx_out):
    d = x.shape[-1]
    qk, vv = GDN_K_HEADS * GDN_DK, GDN_V_HEADS * GDN_DV
    ng2 = 2 * 2 * GDN_V_HEADS
    w_in, conv_w = prm['w_in'], prm['conv_w']
    w_all = jnp.concatenate([w_in[:, :qk], w_in[:, qk + vv + ng2:2 * qk + vv + ng2], w_in[:, qk:qk + vv],
                             w_in[:, 2 * qk + vv + ng2:]], axis=1).astype(BF16)
    w_ba = jnp.concatenate([w_in[:, qk + vv:qk + vv + ng2], jnp.zeros((d, LANES - ng2), w_in.dtype)], axis=1).astype(BF16)
    cw_all = jnp.concatenate([conv_w[:, :qk], conv_w[:, qk + vv:], conv_w[:, qk:qk + vv],
                              jnp.zeros((conv_w.shape[0], vv), conv_w.dtype)], axis=1)
    tiles_per_mod = seq // _row_tile(seq)
    p_l, ba_l = _gdn_in(x, modt, tiles_per_mod, norm_g, w_all, cw_all, w_ba, seq)
    p_c, ba_c = _gdn_in(x_ctx, modt_ctx, 10 ** 9, norm_g, w_all, cw_all, w_ba, seq_ctx)
    s0 = jnp.zeros((2, batch, GDN_V_HEADS, GDN_DK, GDN_DV), F32)

    def gate_lanes(p):
        rows = [jnp.zeros((LANES,), F32).at[dr * 2 * GDN_V_HEADS + GDN_V_HEADS:(dr + 1) * 2 * GDN_V_HEADS].set(p[dr])
                for dr in range(2)]
        return jnp.stack(rows).reshape(2, 1, LANES)

    an = gate_lanes(-jnp.exp(prm['a_log'].astype(F32)))
    dtb = gate_lanes(prm['dt_bias'].astype(F32))
    ocf, ocb, sc = _gdn_scan(p_c, ba_c, an, dtb, s0, batch, seq_ctx, emit=ctx_out)
    olf, olb, _ = _gdn_scan(p_l, ba_l, an, dtb, sc, batch, seq)
    os_l, os_c = [olf, olb], [ocf, ocb]
    w_out = prm['w_out'].astype(BF16)
    ng = prm['norm_g'].reshape(1, GDN_DV)
    vh = GDN_V_HEADS
    z_blk = (2 * GDN_K_HEADS + vh) // vh

    def out(xa, pa, oa, mt, tpm, seq_rows):
        tm = _row_tile(seq_rows, cap=ROW_TILE_CAP // 2)
        hspec = lambda blk: pl.BlockSpec((vh, tm, LANES), lambda i, j: (blk, i, 0))
        specs = [hspec(0), hspec(0), hspec(z_blk), _const_spec((1, GDN_DV))]
        return _outproj(_gdn_out_prologue, [oa[0], oa[1], pa, ng], specs, vv, xa, mt, tpm * (_row_tile(seq_rows) // tm),
                        w_out, tm=tm, name="gdn_out")

    y_l = out(x, p_l, os_l, modt, tiles_per_mod, seq)
    y_c = out(x_ctx, p_c, os_c, modt_ctx, 10 ** 9, seq_ctx) if ctx_out else None
    return y_l, y_c


def kernel(x, c, ctx, c_ctx, norm1_g, norm2_g, mod_w, mod_b, ffn_w_in, ffn_conv_w, ffn_conv_b, ffn_w_out, s5_lam_re, s5_lam_im, s5_log_step, s5_b_re, s5_b_im, s5_c_re, s5_c_im, s5_d, s5_w_glu, s5_b_glu, lru_w_in, lru_conv_w, lru_conv_b, lru_w_a, lru_b_a, lru_w_x, lru_b_x, lru_lam, lru_w_out, ret_w_in, ret_norm_g, ret_w_out, gdn_w_in, gdn_conv_w, gdn_a_log, gdn_dt_bias, gdn_norm_g, gdn_w_out, final_norm_g):
    B, L, D = x.shape
    Lc = ctx.shape[1]
    depth = mod_w.shape[0]
    assert B == SUBLANES and depth == 4, "time-major layers put the batch on the 8 sublanes; four mixer kinds"
    c16 = jnp.concatenate([c, c_ctx[None], jnp.zeros((16 - B - 1, D), F32)], 0)
    mods = _modulation(c16, mod_w, mod_b)

    def ffn(i, xa, mt, tiles_per_mod, rs, seq_rows, final_norm=False):
        return _ffn(xa, mt, tiles_per_mod, norm2_g[i], ffn_w_in[i].astype(BF16), ffn_conv_w[i], ffn_conv_b[i],
                    ffn_w_out[i].astype(BF16), final_norm_g, rs=rs, seq_rows=seq_rows, final_norm=final_norm)

    xt = jnp.swapaxes(x, 0, 1).reshape(L * B, D)
    ct = jnp.swapaxes(ctx, 0, 1).reshape(Lc * B, D)
    s5p = dict(lam_re=s5_lam_re[0], lam_im=s5_lam_im[0], log_step=s5_log_step[0], b_re=s5_b_re[0], b_im=s5_b_im[0],
               c_re=s5_c_re[0], c_im=s5_c_im[0], d=s5_d[0], w_glu=s5_w_glu[0], b_glu=s5_b_glu[0])
    lrup = dict(w_in=lru_w_in[0], conv_w=lru_conv_w[0], conv_b=lru_conv_b[0], w_a=lru_w_a[0], b_a=lru_b_a[0],
                w_x=lru_w_x[0], b_x=lru_b_x[0], lam=lru_lam[0], w_out=lru_w_out[0])
    for i, (mixer, prm) in enumerate(((_s5_mixer, s5p), (_lru_mixer, lrup))):
        mt = mods[i, :B][None]
        mtc = jnp.broadcast_to(mods[i, B][None, None, :], (1, SUBLANES, N_MOD * D))
        xt, ct = mixer(xt, ct, mt, mtc, norm1_g[i], prm, ctx_out=True)
        xt = ffn(i, xt, mt, 10 ** 9, SUBLANES, L * B)
        ct = ffn(i, ct, mtc, 10 ** 9, SUBLANES, Lc * B)

    xb = jnp.swapaxes(xt.reshape(L, B, D), 0, 1).reshape(B * L, D)
    cb = jnp.swapaxes(ct.reshape(Lc, B, D), 0, 1).reshape(B * Lc, D)
    retp = dict(w_in=ret_w_in[0], norm_g=ret_norm_g[0], w_out=ret_w_out[0])
    gdnp = dict(w_in=gdn_w_in[0], conv_w=gdn_conv_w[0], a_log=gdn_a_log[0], dt_bias=gdn_dt_bias[0],
                norm_g=gdn_norm_g[0], w_out=gdn_w_out[0])
    tiles_per_batch = L // _row_tile(L)
    for i, (mixer, prm) in ((2, (_ret_mixer, retp)), (3, (_gdn_mixer, gdnp))):
        ctx_out = i < depth - 1
        mt = jnp.broadcast_to(mods[i, :B][:, None, :], (B, SUBLANES, N_MOD * D))
        mtc = jnp.broadcast_to(mods[i, B][None, None, :], (1, SUBLANES, N_MOD * D))
        xb, cb_new = mixer(xb, cb, mt, mtc, norm1_g[i], prm, B, L, Lc, ctx_out=ctx_out)
        xb = ffn(i, xb, mt, tiles_per_batch, 1, L, final_norm=not ctx_out)
        if ctx_out:
            cb = ffn(i, cb_new, mtc, 10 ** 9, 1, Lc)
    return xb.reshape(B, L, D)
```
